```python
import jax
import jax.numpy as jnp
from jax import lax
import numpy as np

D_MODEL = 1024
BATCH = 4
SEQ = 4096
DEPTH = 1
DEC_BATCH = 128
DEC_SEQ = 8
PAST_LEN = 16384
PAGE_SIZE = 128

N_META = 16
MLA_HEADS = 8
MLA_Q_RANK = 256
MLA_KV_RANK = 128
MLA_NOPE = 64
MLA_ROPE = 32
MLA_V = 64
MLA_WIDTH = MLA_HEADS * MLA_V
MLA_SCALE = (MLA_NOPE + MLA_ROPE) ** -0.5
ROPE_THETA = 10000.0
Q_BLOCK = 128
HG_HEADS = 4
HG_DK = 128
HG_DV = 128
HG_WIDTH = HG_HEADS * HG_DV
HG_CHUNK = 64
N_EXPERTS = 64
TOP_K = 8
D_EXPERT = 256
D_SHARED = 256
ROUTED_SCALE = 2.5
MOE_BLOCK = 128
NORM_EPS = 1e-6
ALPHA = (2.0 * DEPTH) ** 0.25
BETA = (8.0 * DEPTH) ** -0.25
IN_SPLITS = (MLA_Q_RANK, MLA_KV_RANK, MLA_ROPE, HG_HEADS * HG_DK, HG_HEADS * HG_DK,
             HG_WIDTH, HG_WIDTH, D_MODEL, D_MODEL)
D_IN = MLA_Q_RANK + MLA_KV_RANK + MLA_ROPE + 2 * HG_HEADS * HG_DK + 2 * HG_WIDTH + 2 * D_MODEL

kernel_name = 'hybrid_mla_hgrn2_moe_step'


def rms_norm(x, g):
    xf = x.astype(jnp.float32)
    y = xf * lax.rsqrt(jnp.mean(xf * xf, axis=-1, keepdims=True) + NORM_EPS)
    return (y * g.astype(jnp.float32)).astype(x.dtype)


def layer_norm(x, g, b):
    xf = x.astype(jnp.float32)
    mu = jnp.mean(xf, axis=-1, keepdims=True)
    xc = xf - mu
    var = jnp.mean(xc * xc, axis=-1, keepdims=True)
    y = xc * lax.rsqrt(var + NORM_EPS)
    return (y * g.astype(jnp.float32) + b.astype(jnp.float32)).astype(x.dtype)


def rope_angles(pos):
    half = MLA_ROPE // 2
    inv = ROPE_THETA ** (-jnp.arange(half, dtype=jnp.float32) / half)
    ang = pos.astype(jnp.float32)[:, None] * inv[None, :]
    return jnp.cos(ang), jnp.sin(ang)


def apply_rope(x, cos, sin):
    half = MLA_ROPE // 2
    shape = (1, cos.shape[0]) + (1,) * (x.ndim - 3) + (half,)
    c = cos.reshape(shape).astype(x.dtype)
    s = sin.reshape(shape).astype(x.dtype)
    x1, x2 = x[..., :half], x[..., half:]
    return jnp.concatenate([x1 * c - x2 * s, x2 * c + x1 * s], axis=-1)


def mixer_inputs(x, cos, sin, lb, lp):
    B, T, _ = x.shape
    bounds = np.cumsum(np.array(IN_SPLITS))[:-1].tolist()
    cq_raw, ckv_raw, kr_raw, hq, hf, hi, hg_gate, gate_mla, gate_hg = jnp.split(x @ lp['w_in'], bounds, axis=-1)
    c_q = rms_norm(cq_raw, lp['q_norm'])
    q = (c_q @ lp['w_uq']).reshape(B, T, MLA_HEADS, MLA_NOPE + MLA_ROPE)
    q_rope = apply_rope(q[..., MLA_NOPE:], cos, sin)
    q_lat = jnp.einsum('bthn,chn->bthc', q[..., :MLA_NOPE], lp['w_uk'])
    c_kv = rms_norm(ckv_raw, lp['kv_norm'])
    k_rope = apply_rope(kr_raw, cos, sin)
    f = lb + (1.0 - lb) * jax.nn.sigmoid(hf.astype(jnp.float32))
    log_f = jnp.log(f)

    def heads(a):
        return a.reshape(B, T, HG_HEADS, -1).transpose(0, 2, 1, 3)

    hg = (heads(hq), heads((1.0 - f).astype(x.dtype)), heads(hi), heads(log_f))
    return (q_lat, q_rope, c_kv, k_rope), hg, (hg_gate, gate_mla, gate_hg)


def mla_prompt_attention(q_lat, q_rope, c_kv, k_rope):
    B, T = q_lat.shape[:2]
    nb = -(-T // Q_BLOCK)
    pad = nb * Q_BLOCK - T

    def padt(a):
        return jnp.pad(a, [(0, 0), (0, pad)] + [(0, 0)] * (a.ndim - 2))

    ql_p, qr_p, ckv_p, kr_p = padt(q_lat), padt(q_rope), padt(c_kv), padt(k_rope)
    k_idx = jnp.arange(nb * Q_BLOCK)

    def block(i):
        s0 = i * Q_BLOCK
        ql = lax.dynamic_slice_in_dim(ql_p, s0, Q_BLOCK, axis=1)
        qr = lax.dynamic_slice_in_dim(qr_p, s0, Q_BLOCK, axis=1)
        s = jnp.einsum('bqhc,bkc->bhqk', ql, ckv_p) + jnp.einsum('bqhr,bkr->bhqk', qr, kr_p)
        s = s.astype(jnp.float32) * MLA_SCALE
        q_idx = s0 + jnp.arange(Q_BLOCK)
        s = jnp.where(k_idx[None, :] <= q_idx[:, None], s, -jnp.inf)
        p = jax.nn.softmax(s, axis=-1).astype(ckv_p.dtype)
        return jnp.einsum('bhqk,bkc->bqhc', p, ckv_p)

    o = lax.map(block, jnp.arange(nb))
    o = jnp.moveaxis(o, 0, 1).reshape(B, nb * Q_BLOCK, MLA_HEADS, MLA_KV_RANK)
    return o[:, :T]


def mla_sample_attention(q_lat, q_rope, c_kv, k_rope, cache_ckv, cache_kr, page_table):
    DB, S = q_lat.shape[:2]
    c_past = cache_ckv[page_table].reshape(DB, -1, MLA_KV_RANK)
    kr_past = cache_kr[page_table].reshape(DB, -1, MLA_ROPE)
    past = c_past.shape[1]
    s_past = jnp.einsum('bqhc,bkc->bhqk', q_lat, c_past) + jnp.einsum('bqhr,bkr->bhqk', q_rope, kr_past)
    s_new = jnp.einsum('bqhc,bkc->bhqk', q_lat, c_kv) + jnp.einsum('bqhr,bkr->bhqk', q_rope, k_rope)
    causal = jnp.tril(jnp.ones((S, S), dtype=bool))
    s_new = jnp.where(causal, s_new.astype(jnp.float32), -jnp.inf)
    s = jnp.concatenate([s_past.astype(jnp.float32), s_new], axis=-1) * MLA_SCALE
    p = jax.nn.softmax(s, axis=-1).astype(c_kv.dtype)
    return (jnp.einsum('bhqk,bkc->bqhc', p[..., :past], c_past)
            + jnp.einsum('bhqk,bkc->bqhc', p[..., past:], c_kv))


def hgrn_chunk(S0, q, k, v, log_f):
    dt = q.dtype
    C = q.shape[2]
    cum = jnp.cumsum(log_f, axis=2)
    causal = jnp.tril(jnp.ones((C, C), dtype=bool))
    diff = cum[:, :, :, None, :] - cum[:, :, None, :, :]
    decay = jnp.exp(jnp.where(causal[None, None, :, :, None], diff, -jnp.inf)).astype(dt)
    attn = jnp.einsum('bhtd,bhsd,bhtsd->bhts', q, k, decay)
    o = (jnp.einsum('bhts,bhsv->bhtv', attn, v)
         + jnp.einsum('bhtd,bhdv->bhtv', q * jnp.exp(cum).astype(dt), S0))
    last = cum[:, :, -1:, :]
    S = (jnp.exp(last[:, :, 0, :]).astype(dt)[..., None] * S0
         + jnp.einsum('bhsd,bhsv->bhdv', k * jnp.exp(last - cum).astype(dt), v))
    return o, S.astype(S0.dtype)


def hgrn_prompt(q, k, v, log_f):
    B, H, T, _ = q.shape
    S0 = jnp.zeros((B, H, HG_DK, HG_DV), q.dtype)
    o_meta, S = hgrn_chunk(S0, q[:, :, :N_META], k[:, :, :N_META], v[:, :, :N_META], log_f[:, :, :N_META])
    L = T - N_META
    nc = L // HG_CHUNK

    def to_chunks(a):
        return jnp.moveaxis(a[:, :, N_META:].reshape(B, H, nc, HG_CHUNK, a.shape[-1]), 2, 0)

    def body(state, xs):
        o, state = hgrn_chunk(state, *xs)
        return state, o

    S, o_c = lax.scan(body, S, (to_chunks(q), to_chunks(k), to_chunks(v), to_chunks(log_f)))
    o_real = jnp.moveaxis(o_c, 0, 2).reshape(B, H, L, HG_DV)
    return jnp.concatenate([o_meta, o_real], axis=2), S


def swiglu(x, w_gate, w_up, w_down):
    return (jax.nn.silu(x @ w_gate) * (x @ w_up)) @ w_down


def moe(x, lp):
    N, D = x.shape
    scores = jax.nn.sigmoid((x @ lp['router_w']).astype(jnp.float32))
    _, idx = lax.top_k(scores + lp['router_bias'].astype(jnp.float32), TOP_K)
    gate = jnp.take_along_axis(scores, idx, axis=-1)
    gate = gate / jnp.sum(gate, axis=-1, keepdims=True) * ROUTED_SCALE
    A = N * TOP_K
    flat_e = idx.reshape(A)
    order = jnp.argsort(flat_e)
    sorted_e = flat_e[order]
    counts = jnp.bincount(flat_e, length=N_EXPERTS)
    starts = jnp.cumsum(counts) - counts
    padded = (counts + MOE_BLOCK - 1) // MOE_BLOCK * MOE_BLOCK
    pends = jnp.cumsum(padded)
    pstarts = pends - padded
    dest = pstarts[sorted_e] + (jnp.arange(A) - starts[sorted_e])
    n_blocks = -(-A // MOE_BLOCK) + N_EXPERTS
    R = n_blocks * MOE_BLOCK
    tok = order // TOP_K
    rows = jnp.zeros((R, D), x.dtype).at[dest].set(x[tok])
    block_e = jnp.minimum(jnp.searchsorted(pends, jnp.arange(n_blocks) * MOE_BLOCK, side='right'), N_EXPERTS - 1)
    w_g, w_u, w_d = lp['w_exp_gate'], lp['w_exp_up'], lp['w_exp_down']

    def expert_block(args):
        xb, e = args
        return swiglu(xb, w_g[e], w_u[e], w_d[e])

    y_rows = lax.map(expert_block, (rows.reshape(n_blocks, MOE_BLOCK, D), block_e)).reshape(R, D)
    y = y_rows[dest] * gate.reshape(A)[order][:, None].astype(x.dtype)
    routed = jax.ops.segment_sum(y, tok, num_segments=N)
    return routed + swiglu(x, lp['w_sh_gate'], lp['w_sh_up'], lp['w_sh_down'])


def layer_tail(x, o_lat, o_hg, gates, lp):
    B, T, _ = x.shape
    hg_gate, gate_mla, gate_hg = gates
    mla_o = jnp.einsum('bthc,chv->bthv', o_lat, lp['w_uv']).reshape(B, T, MLA_WIDTH)
    hg = o_hg.transpose(0, 2, 1, 3)
    hg = rms_norm(hg, lp['hg_norm']) * jax.nn.silu(hg_gate.reshape(B, T, HG_HEADS, HG_DV))
    hg = hg.reshape(B, T, HG_WIDTH)
    merged = (jax.nn.sigmoid(gate_mla) * (mla_o @ lp['w_br_mla'])
              + jax.nn.sigmoid(gate_hg) * (hg @ lp['w_br_hg']))
    x = layer_norm(ALPHA * x + merged @ lp['w_out'], lp['ln1_g'], lp['ln1_b'])
    ff = moe(x.reshape(B * T, D_MODEL), lp).reshape(B, T, D_MODEL)
    return layer_norm(ALPHA * x + ff, lp['ln2_g'], lp['ln2_b'])


def setup_inputs(seed: int = 0) -> dict:
    key = jax.random.key(seed)
    ks = jax.random.split(key, 32)
    f32 = jnp.float32

    def nrm(k, shape, scale):
        return jax.random.normal(k, shape, f32) * scale

    def gain(k, shape):
        return 1.0 + 0.02 * jax.random.normal(k, shape, f32)

    n_pages = PAST_LEN // PAGE_SIZE
    n_pool = (5 * DEC_BATCH * n_pages) // 4
    page_table = jax.random.permutation(ks[5], n_pool)[:DEC_BATCH * n_pages]
    page_table = page_table.reshape(DEC_BATCH, n_pages).astype(jnp.int32)
    return {
        'x_prompt': nrm(ks[0], (BATCH, SEQ, D_MODEL), 1.0),
        'x_sample': nrm(ks[1], (DEC_BATCH, DEC_SEQ, D_MODEL), 1.0),
        'cache_mla_ckv': nrm(ks[2], (DEPTH, n_pool, PAGE_SIZE, MLA_KV_RANK), 1.0),
        'cache_mla_krope': nrm(ks[3], (DEPTH, n_pool, PAGE_SIZE, MLA_ROPE), 1.0),
        'state_hgrn': nrm(ks[4], (DEPTH, DEC_BATCH, HG_HEADS, HG_DK, HG_DV), 0.5),
        'page_table': page_table,
        'meta_tokens': nrm(ks[6], (N_META, D_MODEL), 1.0),
        'hg_lb_logits': nrm(ks[7], (DEPTH + 1, HG_HEADS * HG_DK), 0.5),
        'w_in': nrm(ks[8], (DEPTH, D_MODEL, D_IN), D_MODEL ** -0.5),
        'q_norm': gain(ks[9], (DEPTH, MLA_Q_RANK)),
        'kv_norm': gain(ks[10], (DEPTH, MLA_KV_RANK)),
        'w_uq': nrm(ks[11], (DEPTH, MLA_Q_RANK, MLA_HEADS * (MLA_NOPE + MLA_ROPE)), MLA_Q_RANK ** -0.5),
        'w_uk': nrm(ks[12], (DEPTH, MLA_KV_RANK, MLA_HEADS, MLA_NOPE), MLA_KV_RANK ** -0.5),
        'w_uv': nrm(ks[13], (DEPTH, MLA_KV_RANK, MLA_HEADS, MLA_V), MLA_KV_RANK ** -0.5 * BETA),
        'hg_norm': gain(ks[14], (DEPTH, HG_DV)),
        'w_br_mla': nrm(ks[15], (DEPTH, MLA_WIDTH, D_MODEL), MLA_WIDTH ** -0.5 * BETA),
        'w_br_hg': nrm(ks[16], (DEPTH, HG_WIDTH, D_MODEL), HG_WIDTH ** -0.5 * BETA),
        'w_out': nrm(ks[17], (DEPTH, D_MODEL, D_MODEL), D_MODEL ** -0.5 * BETA),
        'ln1_g': gain(ks[18], (DEPTH, D_MODEL)),
        'ln1_b': nrm(ks[19], (DEPTH, D_MODEL), 0.02),
        'router_w': nrm(ks[20], (DEPTH, D_MODEL, N_EXPERTS), D_MODEL ** -0.5),
        'router_bias': nrm(ks[21], (DEPTH, N_EXPERTS), 0.01),
        'w_exp_gate': nrm(ks[22], (DEPTH, N_EXPERTS, D_MODEL, D_EXPERT), D_MODEL ** -0.5),
        'w_exp_up': nrm(ks[23], (DEPTH, N_EXPERTS, D_MODEL, D_EXPERT), D_MODEL ** -0.5),
        'w_exp_down': nrm(ks[24], (DEPTH, N_EXPERTS, D_EXPERT, D_MODEL), D_EXPERT ** -0.5 * BETA),
        'w_sh_gate': nrm(ks[25], (DEPTH, D_MODEL, D_SHARED), D_MODEL ** -0.5),
        'w_sh_up': nrm(ks[26], (DEPTH, D_MODEL, D_SHARED), D_MODEL ** -0.5),
        'w_sh_down': nrm(ks[27], (DEPTH, D_SHARED, D_MODEL), D_SHARED ** -0.5 * BETA),
        'ln2_g': gain(ks[28], (DEPTH, D_MODEL)),
        'ln2_b': nrm(ks[29], (DEPTH, D_MODEL), 0.02),
    }


def reference(x_prompt, x_sample, cache_mla_ckv, cache_mla_krope, state_hgrn, page_table,
              meta_tokens, hg_lb_logits, w_in, q_norm, kv_norm, w_uq, w_uk, w_uv, hg_norm,
              w_br_mla, w_br_hg, w_out, ln1_g, ln1_b, router_w, router_bias,
              w_exp_gate, w_exp_up, w_exp_down, w_sh_gate, w_sh_up, w_sh_down, ln2_g, ln2_b):
    B = x_prompt.shape[0]
    T = N_META + x_prompt.shape[1]
    meta = jnp.broadcast_to(meta_tokens.astype(x_prompt.dtype)[None], (B, N_META, D_MODEL))
    xp = jnp.concatenate([meta, x_prompt], axis=1)
    xs = x_sample
    cos_p, sin_p = rope_angles(jnp.arange(T))
    cos_s, sin_s = rope_angles(PAST_LEN + jnp.arange(x_sample.shape[1]))
    lb_all = jnp.cumsum(jax.nn.softmax(hg_lb_logits.astype(jnp.float32), axis=0), axis=0)

    ckv_p_l, kr_p_l, hg_p_l, ckv_s_l, kr_s_l, hg_s_l = [], [], [], [], [], []
    for l in range(DEPTH):
        lp = {
            'w_in': w_in[l], 'q_norm': q_norm[l], 'kv_norm': kv_norm[l], 'w_uq': w_uq[l],
            'w_uk': w_uk[l], 'w_uv': w_uv[l], 'hg_norm': hg_norm[l], 'w_br_mla': w_br_mla[l],
            'w_br_hg': w_br_hg[l], 'w_out': w_out[l], 'ln1_g': ln1_g[l], 'ln1_b': ln1_b[l],
            'router_w': router_w[l], 'router_bias': router_bias[l], 'w_exp_gate': w_exp_gate[l],
            'w_exp_up': w_exp_up[l], 'w_exp_down': w_exp_down[l], 'w_sh_gate': w_sh_gate[l],
            'w_sh_up': w_sh_up[l], 'w_sh_down': w_sh_down[l], 'ln2_g': ln2_g[l], 'ln2_b': ln2_b[l],
        }
        lb = lb_all[l]
        (ql, qr, ckv, kr), (hq, hk, hv, hlf), gates = mixer_inputs(xp, cos_p, sin_p, lb, lp)
        o_lat = mla_prompt_attention(ql, qr, ckv, kr)
        o_hg, S_p = hgrn_prompt(hq, hk, hv, hlf)
        ckv_p_l.append(ckv)
        kr_p_l.append(kr)
        hg_p_l.append(S_p)
        xp = layer_tail(xp, o_lat, o_hg, gates, lp)
        (ql, qr, ckv, kr), (hq, hk, hv, hlf), gates = mixer_inputs(xs, cos_s, sin_s, lb, lp)
        o_lat = mla_sample_attention(ql, qr, ckv, kr, cache_mla_ckv[l], cache_mla_krope[l], page_table)
        o_hg, S_s = hgrn_chunk(state_hgrn[l], hq, hk, hv, hlf)
        ckv_s_l.append(ckv)
        kr_s_l.append(kr)
        hg_s_l.append(S_s)
        xs = layer_tail(xs, o_lat, o_hg, gates, lp)

    y_prompt = xp[:, N_META:]
    y_sample = xs
    new_ckv_prompt = jnp.stack(ckv_p_l, axis=0)
    new_krope_prompt = jnp.stack(kr_p_l, axis=0)
    new_hgrn_prompt = jnp.stack(hg_p_l, axis=0)
    new_ckv_sample = jnp.stack(ckv_s_l, axis=0)
    new_krope_sample = jnp.stack(kr_s_l, axis=0)
    new_hgrn_sample = jnp.stack(hg_s_l, axis=0)
    return (y_prompt, y_sample, new_ckv_prompt, new_krope_prompt, new_hgrn_prompt,
            new_ckv_sample, new_krope_sample, new_hgrn_sample)
```

```python
import functools

import jax
import jax.numpy as jnp
from jax import lax
from jax.experimental import pallas as pl
from jax.experimental.pallas import tpu as pltpu

F32 = jnp.float32
BF16 = jnp.bfloat16
U32 = jnp.uint32
I32 = jnp.int32

D_MODEL = 1024
MLA_HEADS = 8
MLA_Q_RANK = 256
MLA_KV_RANK = 128
MLA_NOPE = 64
MLA_ROPE = 32
MLA_V = 64
MLA_SCALE = (MLA_NOPE + MLA_ROPE) ** -0.5
ROPE_THETA = 10000.0
HG_HEADS = 4
HG_DK = 128
HG_DV = 128
HG_CHUNK = 64
N_EXPERTS = 64
TOP_K = 8
D_EXPERT = 256
ROUTED_SCALE = 2.5
NORM_EPS = 1e-6
DEPTH = 1
ALPHA = (2.0 * DEPTH) ** 0.25

LANES = 128
ROW_TILE = 256
QCAT = 256
ATT_TK = 256
MOE_BLOCK = 256
PAGES_PER_CHUNK = 16
HG_SAMPLE_BATCH = 8
NEG_BIG = -1e30
VMEM_LIMIT = 52 * 1024 * 1024

C_CQ = (0, 256)
C_CKV = (256, 384)
C_KR = (384, 512)
C_HQ = (512, 1024)
C_HF = (1024, 1536)
C_HI = (1536, 2048)
C_HG = (2048, 2560)
C_GM = (2560, 3584)
C_GH = (3584, 4608)
D_IN_PACKED = 4608


def _cparams(*sem):
    return pltpu.CompilerParams(dimension_semantics=sem, vmem_limit_bytes=VMEM_LIMIT)


def _dot(a, b):
    return jnp.dot(a, b, preferred_element_type=F32)


def _dot_nt(a, b):
    return lax.dot_general(a, b, (((1,), (1,)), ((), ())), preferred_element_type=F32)


def _dot_tn(a, b):
    return lax.dot_general(a, b, (((0,), (0,)), ((), ())), preferred_element_type=F32)


def _pack_bf16_pairs(x):
    w = x.shape[1] // 2
    bits = pltpu.bitcast(x.astype(BF16).astype(F32), U32)
    return bits[:, w:] | (bits[:, :w] >> 16)


def _unpack_bf16_pairs(p):
    lo = pltpu.bitcast(p << 16, F32)
    hi = pltpu.bitcast(p & jnp.uint32(0xFFFF0000), F32)
    return jnp.concatenate([lo, hi], axis=1)


def _wprep_kernel(wq_nope_ref, wuk_ref, wuv_ref, wbr_ref, wqlat_ref, wo_ref):
    hp = lax.Precision.HIGHEST
    for h in range(MLA_HEADS):
        a = wq_nope_ref[h]
        b = wuk_ref[h]
        ql = lax.dot_general(a, b, (((1,), (1,)), ((), ())), precision=hp, preferred_element_type=F32)
        wqlat_ref[:, h * MLA_KV_RANK:(h + 1) * MLA_KV_RANK] = (ql * MLA_SCALE).astype(BF16)
        c = wuv_ref[h]
        d = wbr_ref[h]
        wo_ref[h * MLA_KV_RANK:(h + 1) * MLA_KV_RANK, :] = jnp.dot(
            c, d, precision=hp, preferred_element_type=F32).astype(BF16)


def _weight_prep(w_uq, w_uk, w_uv, w_br_mla):
    wq3 = w_uq.reshape(MLA_Q_RANK, MLA_HEADS, MLA_NOPE + MLA_ROPE)
    wq_nope = jnp.transpose(wq3[:, :, :MLA_NOPE], (1, 0, 2))
    wuk = jnp.transpose(w_uk, (1, 0, 2))
    wuv = jnp.transpose(w_uv, (1, 0, 2))
    wbr = w_br_mla.reshape(MLA_HEADS, MLA_V, D_MODEL)
    return pl.pallas_call(
        _wprep_kernel,
        out_shape=(jax.ShapeDtypeStruct((MLA_Q_RANK, MLA_HEADS * MLA_KV_RANK), BF16),
                   jax.ShapeDtypeStruct((MLA_HEADS * MLA_KV_RANK, D_MODEL), BF16)),
        compiler_params=pltpu.CompilerParams(vmem_limit_bytes=VMEM_LIMIT),
        name="weight_prep",
    )(wq_nope, wuk, wuv, wbr)


def _inproj_kernel(x_ref, w_ref, qn_ref, kvn_ref, wqlat_ref, wqr_ref, perm_ref, cos_ref, sin_ref, lb_ref,
                   qcat_ref, kcat_ref, ckv_ref, kr_ref, hq_ref, hk_ref, hv_ref, lf_ref, sg_ref, sm_ref, sh_ref,
                   *, pad_front, tiles_per_batch, n_prompt_tiles):
    i = pl.program_id(0)
    tm = x_ref.shape[0]
    xb = x_ref[...].astype(BF16)

    def proj(c):
        return _dot(xb, w_ref[:, c[0]:c[1]])

    cos8 = cos_ref[...]
    sin8 = sin_ref[...]

    cq = proj(C_CQ)
    cqn = cq * lax.rsqrt(jnp.mean(cq * cq, axis=-1, keepdims=True) + NORM_EPS) * qn_ref[...]
    cqb = cqn.astype(BF16)
    qlat = _dot(cqb, wqlat_ref[...])
    qr = _dot(cqb, wqr_ref[...])
    x1, x2 = qr[:, :LANES], qr[:, LANES:]
    qrot = jnp.concatenate([x1 * cos8 - x2 * sin8, x2 * cos8 + x1 * sin8], axis=1).astype(BF16)
    qrh = _dot(qrot, perm_ref[...])
    for h in range(MLA_HEADS):
        qcat_ref[:, h * QCAT:h * QCAT + LANES] = qlat[:, h * LANES:(h + 1) * LANES].astype(BF16)
        qcat_ref[:, h * QCAT + LANES:(h + 1) * QCAT] = qrh[:, h * LANES:(h + 1) * LANES].astype(BF16)

    kv = proj(C_CKV)
    ckv = kv * lax.rsqrt(jnp.mean(kv * kv, axis=-1, keepdims=True) + NORM_EPS) * kvn_ref[...]
    ckv_ref[...] = ckv
    kcat_ref[:, :LANES] = ckv.astype(BF16)
    krr = proj(C_KR)
    lane = lax.broadcasted_iota(I32, (tm, LANES), 1)
    half = MLA_ROPE // 2
    rot = jnp.where(lane < half, -pltpu.roll(krr, LANES - half, 1), pltpu.roll(krr, half, 1))
    kr = jnp.where(lane < MLA_ROPE, krr * cos8 + rot * sin8, 0.0)
    kr_ref[...] = kr
    kcat_ref[:, LANES:] = kr.astype(BF16)

    row = lax.broadcasted_iota(I32, (tm, 1), 0)
    is_pad = (i < n_prompt_tiles) & (i % tiles_per_batch == 0) & (row < pad_front)
    keep = jnp.where(is_pad, 0.0, 1.0)
    hq_ref[...] = proj(C_HQ).astype(BF16)
    lb = lb_ref[...]
    f = lb + (1.0 - lb) * jax.nn.sigmoid(proj(C_HF))
    lf_ref[...] = jnp.log(f) * keep
    hk_ref[...] = ((1.0 - f) * keep).astype(BF16)
    hv_ref[...] = proj(C_HI).astype(BF16)

    g = proj(C_HG)
    sg_ref[...] = (g * jax.nn.sigmoid(g)).astype(BF16)
    sm_ref[...] = jax.nn.sigmoid(proj(C_GM)).astype(BF16)
    sh_ref[...] = jax.nn.sigmoid(proj(C_GH)).astype(BF16)


def _in_projection(x_flat, w_in_p, q_norm, kv_norm, wqlat, wqr, perm, cos8, sin8, lb, geo):
    R = x_flat.shape[0]
    nt = R // ROW_TILE
    hgw = HG_HEADS * HG_DK

    def rows(w):
        return pl.BlockSpec((ROW_TILE, w), lambda i: (i, 0))

    def full(a):
        return pl.BlockSpec(a.shape, lambda i: (0,) * a.ndim)

    out_widths = [(MLA_HEADS * QCAT, BF16), (2 * LANES, BF16), (LANES, F32), (LANES, F32),
                  (hgw, BF16), (hgw, BF16), (hgw, BF16), (hgw, F32), (hgw, BF16),
                  (D_MODEL, BF16), (D_MODEL, BF16)]
    return pl.pallas_call(
        functools.partial(_inproj_kernel, pad_front=geo["pad_front"], tiles_per_batch=geo["tpb"],
                          n_prompt_tiles=geo["npt"]),
        grid=(nt,),
        in_specs=[rows(D_MODEL), full(w_in_p), full(q_norm), full(kv_norm), full(wqlat), full(wqr), full(perm),
                  rows(LANES), rows(LANES), full(lb)],
        out_specs=[rows(w) for w, _ in out_widths],
        out_shape=[jax.ShapeDtypeStruct((R, w), dt) for w, dt in out_widths],
        compiler_params=_cparams("arbitrary"),
        name="in_projection",
    )(x_flat, w_in_p, q_norm, kv_norm, wqlat, wqr, perm, cos8, sin8, lb)


def _softmax_step(s, v_b, m_ref, l_ref, acc_ref):
    n = s.shape[1] // LANES
    m_prev = m_ref[...]
    m_next = jnp.maximum(m_prev, jnp.max(s, axis=1, keepdims=True))
    p = jnp.concatenate([jnp.exp(s[:, j * LANES:(j + 1) * LANES] - m_next) for j in range(n)], axis=1)
    alpha = jnp.exp(m_prev - m_next)
    l_ref[...] = alpha * l_ref[...] + jnp.sum(p, axis=1, keepdims=True)
    acc_ref[...] = alpha * acc_ref[...] + _dot(p.astype(BF16), v_b)
    m_ref[...] = m_next


def _pattn_kernel(q_ref, k_ref, o_ref, m_ref, l_ref, acc_ref, *, pad_front):
    qi = pl.program_id(1)
    tq = q_ref.shape[0]
    tk = ATT_TK

    for h in range(MLA_HEADS):
        q = q_ref[:, h * QCAT:(h + 1) * QCAT]
        m_ref[...] = jnp.full(m_ref.shape, -jnp.inf, F32)
        l_ref[...] = jnp.zeros(l_ref.shape, F32)
        acc_ref[...] = jnp.zeros(acc_ref.shape, F32)

        def step(kb, masked, q=q):
            kblk = k_ref[pl.ds(pl.multiple_of(kb * tk, tk), tk), :]
            s = _dot_nt(q, kblk)
            if masked:
                qrow = qi * tq + lax.broadcasted_iota(I32, (tq, tk), 0)
                krow = kb * tk + lax.broadcasted_iota(I32, (tq, tk), 1)
                s = jnp.where((krow <= qrow) & (krow >= pad_front), s, NEG_BIG)
            _softmax_step(s, kblk[:, :LANES], m_ref, l_ref, acc_ref)

        step(0, True)

        def body(kb, c):
            step(kb, False)
            return c

        lax.fori_loop(1, qi, body, 0)

        @pl.when(qi > 0)
        def _():
            step(qi, True)

        o_ref[:, h * LANES:(h + 1) * LANES] = (acc_ref[...] / l_ref[...]).astype(BF16)


def _prompt_attention(qcat, kcat, geo):
    B, tpb, tp = geo["B"], geo["tpb"], geo["tp"]
    return pl.pallas_call(
        functools.partial(_pattn_kernel, pad_front=geo["pad_front"]),
        grid=(B, tpb),
        in_specs=[pl.BlockSpec((ROW_TILE, MLA_HEADS * QCAT), lambda b, i: (b * tpb + i, 0)),
                  pl.BlockSpec((tp, 2 * LANES), lambda b, i: (b, 0))],
        out_specs=pl.BlockSpec((ROW_TILE, MLA_HEADS * LANES), lambda b, i: (b * tpb + i, 0)),
        out_shape=jax.ShapeDtypeStruct((B * tp, MLA_HEADS * LANES), BF16),
        scratch_shapes=[pltpu.VMEM((ROW_TILE, LANES), F32)] * 3,
        compiler_params=_cparams("arbitrary", "arbitrary"),
        name="prompt_attention",
    )(qcat, kcat)


def _sattn_kernel(pt_ref, q_ref, knew_ref, ckv_hbm, kr_hbm, o_ref,
                  ckv_buf, kr_buf, sem, m_ref, l_ref, acc_ref, *, n_chunks, page):
    b = pl.program_id(0)
    nb = pl.num_programs(0)
    ch = PAGES_PER_CHUNK

    def chunk_copies(bb, c, slot):
        cps = []
        for j in range(ch):
            pg = pt_ref[bb, c * ch + j]
            cps.append(pltpu.make_async_copy(ckv_hbm.at[pg], ckv_buf.at[slot, pl.ds(j * page, page), :],
                                             sem.at[0, slot]))
            cps.append(pltpu.make_async_copy(kr_hbm.at[pg], kr_buf.at[slot, pl.ds(j * page, page), :],
                                             sem.at[1, slot]))
        return cps

    @pl.when(b == 0)
    def _():
        for cp in chunk_copies(0, 0, 0):
            cp.start()

    q = q_ref[0]
    rows = q.shape[0]
    qlat = q[:, :LANES]
    qrope = q[:, LANES:LANES + MLA_ROPE]

    knew = knew_ref[0]
    S = knew.shape[0]
    s_new = _dot_nt(q, knew)
    qtok = lax.broadcasted_iota(I32, (rows, S), 0) // MLA_HEADS
    ktok = lax.broadcasted_iota(I32, (rows, S), 1)
    s_new = jnp.where(ktok <= qtok, s_new, NEG_BIG)
    m0 = jnp.max(s_new, axis=1, keepdims=True)
    p0 = jnp.exp(s_new - m0)
    m_ref[...] = jnp.broadcast_to(m0, m_ref.shape)
    l_ref[...] = jnp.broadcast_to(jnp.sum(p0, axis=1, keepdims=True), l_ref.shape)
    acc_ref[...] = _dot(p0.astype(BF16), knew[:, :LANES])

    for c in range(n_chunks):
        slot = c % 2 if n_chunks % 2 == 0 else (b * n_chunks + c) % 2
        if c + 1 < n_chunks:
            for cp in chunk_copies(b, c + 1, 1 - slot):
                cp.start()
        else:
            @pl.when(b + 1 < nb)
            def _():
                for cp in chunk_copies(b + 1, 0, 1 - slot):
                    cp.start()
        for cp in chunk_copies(b, c, slot):
            cp.wait()
        ckv_b = ckv_buf[slot].astype(BF16)
        kr_b = kr_buf[slot].astype(BF16)
        s = _dot_nt(qlat, ckv_b) + _dot_nt(qrope, kr_b)
        _softmax_step(s, ckv_b, m_ref, l_ref, acc_ref)

    o_ref[0] = (acc_ref[...] / l_ref[...]).astype(BF16)


def _sample_attention(page_table, q_s, knew_s, cache_ckv, cache_kr):
    DB, rows, _ = q_s.shape
    S = knew_s.shape[1]
    n_pages = page_table.shape[1]
    page = cache_ckv.shape[1]
    assert n_pages % PAGES_PER_CHUNK == 0
    n_chunks = n_pages // PAGES_PER_CHUNK
    ck = PAGES_PER_CHUNK * page
    grid_spec = pltpu.PrefetchScalarGridSpec(
        num_scalar_prefetch=1,
        grid=(DB,),
        in_specs=[pl.BlockSpec((1, rows, QCAT), lambda b, pt: (b, 0, 0)),
                  pl.BlockSpec((1, S, 2 * LANES), lambda b, pt: (b, 0, 0)),
                  pl.BlockSpec(memory_space=pl.ANY),
                  pl.BlockSpec(memory_space=pl.ANY)],
        out_specs=pl.BlockSpec((1, rows, LANES), lambda b, pt: (b, 0, 0)),
        scratch_shapes=[pltpu.VMEM((2, ck, MLA_KV_RANK), F32),
                        pltpu.VMEM((2, ck, MLA_ROPE), F32),
                        pltpu.SemaphoreType.DMA((2, 2)),
                        pltpu.VMEM((rows, LANES), F32),
                        pltpu.VMEM((rows, LANES), F32),
                        pltpu.VMEM((rows, LANES), F32)])
    return pl.pallas_call(
        functools.partial(_sattn_kernel, n_chunks=n_chunks, page=page),
        grid_spec=grid_spec,
        out_shape=jax.ShapeDtypeStruct((DB, rows, LANES), BF16),
        compiler_params=_cparams("arbitrary"),
        name="sample_attention",
    )(page_table, q_s, knew_s, cache_ckv, cache_kr)


def _split3(x):
    hi = x.astype(BF16)
    r1 = x - hi.astype(F32)
    mid = r1.astype(BF16)
    lo = (r1 - mid.astype(F32)).astype(BF16)
    return hi, mid, lo


def _hgrn_chunk(q, k, v, lf, S0, tri, group):
    C = q.shape[0]
    hi, mid, lo = _split3(lf)
    cum = _dot(tri, hi) + _dot(tri, mid) + _dot(tri, lo)
    vb = v.astype(BF16)
    rowi = lax.broadcasted_iota(I32, (C, C), 0)
    coli = lax.broadcasted_iota(I32, (C, C), 1)
    attn = jnp.zeros((C, C), F32)

    bd = min(16, group)
    hs = group // 2
    rid = lax.broadcasted_iota(I32, (C, 1), 0)
    while hs >= bd:
        npair = C // (2 * hs)
        ref = jnp.concatenate(
            [jnp.broadcast_to(cum[(2 * j + 1) * hs - 1:(2 * j + 1) * hs, :], (2 * hs, cum.shape[1]))
             for j in range(npair)], axis=0)
        odd = ((rid // hs) % 2) == 1
        e = jnp.exp(jnp.where(odd, cum - ref, ref - cum))
        qs = jnp.where(odd, q * e, 0.0).astype(BF16)
        ks = jnp.where(odd, 0.0, k * e).astype(BF16)
        a = _dot_nt(qs, ks)
        attn = attn + jnp.where((rowi // (2 * hs)) == (coli // (2 * hs)), a, 0.0)
        hs //= 2

    nblk = C // bd
    k3 = k.reshape(nblk, bd, k.shape[1])
    c3 = cum.reshape(nblk, bd, cum.shape[1])
    tl = rid % bd
    blk_base = (rowi // bd) * bd
    for sl in range(bd):
        ks_b = jnp.broadcast_to(k3[:, sl:sl + 1, :], k3.shape).reshape(C, k.shape[1])
        cs_b = jnp.broadcast_to(c3[:, sl:sl + 1, :], c3.shape).reshape(C, k.shape[1])
        e = jnp.exp(jnp.where(tl >= sl, cum - cs_b, NEG_BIG))
        col = jnp.sum(q * ks_b * e, axis=1, keepdims=True)
        attn = jnp.where(coli == blk_base + sl, col, attn)
    o_intra = _dot(attn.astype(BF16), vb)
    return o_intra, cum


def _tri(C, group):
    r = lax.broadcasted_iota(I32, (C, C), 0)
    c = lax.broadcasted_iota(I32, (C, C), 1)
    return jnp.where((c <= r) & (r // group == c // group), 1.0, 0.0).astype(BF16)


def _state_update(q, k, v, cum, S, o_intra):
    C = q.shape[0]
    last = cum[C - 1:C, :]
    o = o_intra + _dot((q * jnp.exp(cum)).astype(BF16), S.astype(BF16))
    kst = (k * jnp.exp(last - cum)).astype(BF16)
    dfull = jnp.transpose(jnp.broadcast_to(jnp.exp(last), (S.shape[1], S.shape[0])))
    S_new = dfull * S + _dot_tn(kst, v.astype(BF16))
    return o, S_new


def _hgrn_prompt_kernel(q_ref, k_ref, v_ref, lf_ref, o_ref, s_out_ref, s_ref):
    t = pl.program_id(2)

    @pl.when(t == 0)
    def _():
        s_ref[...] = jnp.zeros(s_ref.shape, F32)

    C = HG_CHUNK
    tri = _tri(C, C)
    for c in range(q_ref.shape[0] // C):
        sl = slice(c * C, (c + 1) * C)
        q = q_ref[sl, :].astype(F32)
        k = k_ref[sl, :].astype(F32)
        v = v_ref[sl, :].astype(F32)
        lf = lf_ref[sl, :]
        o_intra, cum = _hgrn_chunk(q, k, v, lf, None, tri, C)
        o, S_new = _state_update(q, k, v, cum, s_ref[...], o_intra)
        o_ref[sl, :] = o.astype(BF16)
        s_ref[...] = S_new

    @pl.when(t == pl.num_programs(2) - 1)
    def _():
        s_out_ref[0, 0] = s_ref[...]


def _hgrn_prompt(hq, hk, hv, lf, geo):
    B, tpb, tp = geo["B"], geo["tpb"], geo["tp"]

    def blk():
        return pl.BlockSpec((ROW_TILE, HG_DK), lambda b, h, t: (b * tpb + t, h))

    return pl.pallas_call(
        _hgrn_prompt_kernel,
        grid=(B, HG_HEADS, tpb),
        in_specs=[blk(), blk(), blk(), blk()],
        out_specs=[blk(), pl.BlockSpec((1, 1, HG_DK, HG_DV), lambda b, h, t: (b, h, 0, 0))],
        out_shape=[jax.ShapeDtypeStruct((B * tp, HG_HEADS * HG_DV), BF16),
                   jax.ShapeDtypeStruct((B, HG_HEADS, HG_DK, HG_DV), F32)],
        scratch_shapes=[pltpu.VMEM((HG_DK, HG_DV), F32)],
        compiler_params=_cparams("arbitrary", "arbitrary", "arbitrary"),
        name="hgrn_prompt",
    )(hq, hk, hv, lf)


def _hgrn_sample_kernel(q_ref, k_ref, v_ref, lf_ref, s_in_ref, o_ref, s_out_ref, *, steps):
    nb = s_in_ref.shape[0]
    C = nb * steps
    tri = _tri(C, steps)
    for h in range(HG_HEADS):
        sl = slice(h * HG_DK, (h + 1) * HG_DK)
        q = q_ref[:, sl].astype(F32)
        k = k_ref[:, sl].astype(F32)
        v = v_ref[:, sl].astype(F32)
        lf = lf_ref[:, sl]
        o_intra, cum = _hgrn_chunk(q, k, v, lf, None, tri, steps)
        for b in range(nb):
            r = slice(b * steps, (b + 1) * steps)
            o, S_new = _state_update(q[r], k[r], v[r], cum[r], s_in_ref[b, h], o_intra[r])
            o_ref[r, sl] = o.astype(BF16)
            s_out_ref[b, h] = S_new


def _hgrn_sample(hq, hk, hv, lf, state, row0, steps):
    DB = state.shape[0]
    nb = HG_SAMPLE_BATCH
    rows = nb * steps
    hgw = HG_HEADS * HG_DK
    blk0 = row0 // rows

    def tok():
        return pl.BlockSpec((rows, hgw), lambda i: (blk0 + i, 0))

    st = pl.BlockSpec((nb, HG_HEADS, HG_DK, HG_DV), lambda i: (i, 0, 0, 0))
    return pl.pallas_call(
        functools.partial(_hgrn_sample_kernel, steps=steps),
        grid=(DB // nb,),
        in_specs=[tok(), tok(), tok(), tok(), st],
        out_specs=[pl.BlockSpec((rows, hgw), lambda i: (i, 0)), st],
        out_shape=[jax.ShapeDtypeStruct((DB * steps, hgw), BF16),
                   jax.ShapeDtypeStruct(state.shape, F32)],
        compiler_params=_cparams("arbitrary"),
        name="hgrn_sample",
    )(hq, hk, hv, lf, state)


def _layer_norm(x, g, b):
    mu = jnp.mean(x, axis=-1, keepdims=True)
    xc = x - mu
    var = jnp.mean(xc * xc, axis=-1, keepdims=True)
    return xc * lax.rsqrt(var + NORM_EPS) * g + b


def _tail_kernel(x_ref, ol_ref, oh_ref, sg_ref, sm_ref, sh_ref, wo_ref, wbh_ref, wout_ref, hgn_ref,
                 g1_ref, b1_ref, rw_ref, rb_ref, x1_ref, x1p_ref, idx_ref, gate_ref):
    tm = x_ref.shape[0]
    mla = _dot(ol_ref[...], wo_ref[...])
    oh = oh_ref[...].astype(F32)
    parts = []
    for h in range(HG_HEADS):
        y = oh[:, h * HG_DV:(h + 1) * HG_DV]
        parts.append(y * lax.rsqrt(jnp.mean(y * y, axis=-1, keepdims=True) + NORM_EPS) * hgn_ref[...])
    hg = (jnp.concatenate(parts, axis=1) * sg_ref[...].astype(F32)).astype(BF16)
    merged = sm_ref[...].astype(F32) * mla + sh_ref[...].astype(F32) * _dot(hg, wbh_ref[...])
    x1 = _layer_norm(ALPHA * x_ref[...] + _dot(merged.astype(BF16), wout_ref[...]), g1_ref[...], b1_ref[...])
    x1_ref[...] = x1
    x1p_ref[...] = _pack_bf16_pairs(x1)

    logits = jnp.dot(x1, rw_ref[...], precision=lax.Precision.HIGHEST, preferred_element_type=F32)
    scores = jax.nn.sigmoid(logits)
    lane = lax.broadcasted_iota(I32, (tm, LANES), 1).astype(F32)
    remaining = jnp.where(lane < N_EXPERTS, scores + rb_ref[...], -jnp.inf)
    idx_out = jnp.zeros((tm, LANES), F32)
    gate_out = jnp.zeros((tm, LANES), F32)
    for kk in range(TOP_K):
        mx = jnp.max(remaining, axis=1, keepdims=True)
        pick = jnp.min(jnp.where(remaining == mx, lane, float(LANES)), axis=1, keepdims=True)
        hit = lane == pick
        gval = jnp.sum(jnp.where(hit, scores, 0.0), axis=1, keepdims=True)
        idx_out = jnp.where(lane == kk, pick, idx_out)
        gate_out = jnp.where(lane == kk, gval, gate_out)
        remaining = jnp.where(hit, -jnp.inf, remaining)
    gate_out = gate_out / jnp.sum(gate_out, axis=1, keepdims=True) * ROUTED_SCALE
    idx_ref[...] = idx_out.astype(I32)
    gate_ref[...] = gate_out


def _layer_tail(x_flat, o_lat, o_hg, sg, sm, sh, wo, w_br_hg, w_out, hg_norm, ln1_g, ln1_b, rw_p, rb_p):
    R = x_flat.shape[0]

    def rows(w):
        return pl.BlockSpec((ROW_TILE, w), lambda i: (i, 0))

    def full(a):
        return pl.BlockSpec(a.shape, lambda i: (0,) * a.ndim)

    hgw = HG_HEADS * HG_DV
    return pl.pallas_call(
        _tail_kernel,
        grid=(R // ROW_TILE,),
        in_specs=[rows(D_MODEL), rows(MLA_HEADS * LANES), rows(hgw), rows(hgw), rows(D_MODEL), rows(D_MODEL),
                  full(wo), full(w_br_hg), full(w_out), full(hg_norm), full(ln1_g), full(ln1_b),
                  full(rw_p), full(rb_p)],
        out_specs=[rows(D_MODEL), rows(D_MODEL // 2), rows(LANES), rows(LANES)],
        out_shape=[jax.ShapeDtypeStruct((R, D_MODEL), F32), jax.ShapeDtypeStruct((R, D_MODEL // 2), U32),
                   jax.ShapeDtypeStruct((R, LANES), I32), jax.ShapeDtypeStruct((R, LANES), F32)],
        compiler_params=_cparams("arbitrary"),
        name="layer_tail",
    )(x_flat, o_lat, o_hg, sg, sm, sh, wo, w_br_hg, w_out, hg_norm, ln1_g, ln1_b, rw_p, rb_p)


def _dispatch_kernel(pends_ref, padded_ref, dest_ref, x_ref, xs_hbm, zero_buf, sem, zsem):
    i = pl.program_id(0)
    tm = x_ref.shape[0]

    @pl.when(i == 0)
    def _():
        zero_buf[...] = jnp.zeros(zero_buf.shape, U32)

        def zcopy(e):
            start = pl.multiple_of(pends_ref[e] - MOE_BLOCK, MOE_BLOCK)
            return pltpu.make_async_copy(zero_buf, xs_hbm.at[pl.ds(start, MOE_BLOCK), :], zsem)

        n_slots = xs_hbm.shape[0]

        def tcopy(j):
            start = pl.multiple_of(pends_ref[N_EXPERTS - 1] + j * MOE_BLOCK, MOE_BLOCK)
            return pltpu.make_async_copy(zero_buf, xs_hbm.at[pl.ds(start, MOE_BLOCK), :], zsem)

        def tail_live(j):
            return pends_ref[N_EXPERTS - 1] + (j + 1) * MOE_BLOCK <= n_slots

        for e in range(N_EXPERTS):
            @pl.when(padded_ref[e] > 0)
            def _():
                zcopy(e).start()

            @pl.when(tail_live(e))
            def _():
                tcopy(e).start()
        for e in range(N_EXPERTS):
            @pl.when(padded_ref[e] > 0)
            def _():
                zcopy(e).wait()

            @pl.when(tail_live(e))
            def _():
                tcopy(e).wait()

    def body(r, c):
        for kk in range(TOP_K):
            d = dest_ref[r * TOP_K + kk]
            pltpu.make_async_copy(x_ref.at[pl.ds(r, 1), :], xs_hbm.at[pl.ds(d, 1), :], sem).start()
        return c

    lax.fori_loop(0, tm, body, 0)
    for kk in range(TOP_K):
        pltpu.make_async_copy(x_ref, xs_hbm.at[pl.ds(0, tm), :], sem).wait()


def _moe_dispatch(pends, padded, dest_flat, x1p, n_slots):
    R, W = x1p.shape
    grid_spec = pltpu.PrefetchScalarGridSpec(
        num_scalar_prefetch=2,
        grid=(R // ROW_TILE,),
        in_specs=[pl.BlockSpec((ROW_TILE * TOP_K,), lambda i, pe, pa: (i,), memory_space=pltpu.SMEM),
                  pl.BlockSpec((ROW_TILE, W), lambda i, pe, pa: (i, 0))],
        out_specs=pl.BlockSpec(memory_space=pl.ANY),
        scratch_shapes=[pltpu.VMEM((MOE_BLOCK, W), U32), pltpu.SemaphoreType.DMA, pltpu.SemaphoreType.DMA])
    return pl.pallas_call(
        _dispatch_kernel,
        grid_spec=grid_spec,
        out_shape=jax.ShapeDtypeStruct((n_slots, W), U32),
        compiler_params=_cparams("arbitrary"),
        name="moe_dispatch",
    )(pends, padded, dest_flat, x1p)


def _expert_kernel(be_ref, nused_ref, xs_ref, wg_ref, wu_ref, wd_ref, ys_ref):
    i = pl.program_id(0)

    @pl.when(i < nused_ref[0])
    def _():
        xb = _unpack_bf16_pairs(xs_ref[...]).astype(BF16)
        g = _dot(xb, wg_ref[0])
        u = _dot(xb, wu_ref[0])
        hmid = (g * jax.nn.sigmoid(g) * u).astype(BF16)
        ys_ref[...] = _pack_bf16_pairs(_dot(hmid, wd_ref[0]))

    @pl.when(i >= nused_ref[0])
    def _():
        ys_ref[...] = jnp.zeros(ys_ref.shape, U32)


def _moe_experts(block_expert, n_used, xs, wg, wu, wd):
    n_slots, W = xs.shape
    nblk = n_slots // MOE_BLOCK

    def xmap(i, be, nu):
        return (jnp.minimum(i, nu[0] - 1), 0)

    def wmap(i, be, nu):
        return (be[i], 0, 0)

    grid_spec = pltpu.PrefetchScalarGridSpec(
        num_scalar_prefetch=2,
        grid=(nblk,),
        in_specs=[pl.BlockSpec((MOE_BLOCK, W), xmap),
                  pl.BlockSpec((1, D_MODEL, D_EXPERT), wmap),
                  pl.BlockSpec((1, D_MODEL, D_EXPERT), wmap),
                  pl.BlockSpec((1, D_EXPERT, D_MODEL), wmap)],
        out_specs=pl.BlockSpec((MOE_BLOCK, W), lambda i, be, nu: (i, 0)))
    return pl.pallas_call(
        _expert_kernel,
        grid_spec=grid_spec,
        out_shape=jax.ShapeDtypeStruct((n_slots, W), U32),
        compiler_params=_cparams("arbitrary"),
        name="moe_experts",
    )(block_expert, n_used, xs, wg, wu, wd)


def _combine_kernel(dest_ref, x1_ref, gate_ref, ys_hbm, wsg_ref, wsu_ref, wsd_ref, g2_ref, b2_ref, y_ref,
                    buf, sem):
    tm = x1_ref.shape[0]

    def body(r, c):
        for kk in range(TOP_K):
            d = dest_ref[r * TOP_K + kk]
            pltpu.make_async_copy(ys_hbm.at[pl.ds(d, 1), :], buf.at[kk, pl.ds(r, 1), :], sem).start()
        return c

    lax.fori_loop(0, tm, body, 0)

    x1 = x1_ref[...]
    xb = x1.astype(BF16)
    g = _dot(xb, wsg_ref[...])
    u = _dot(xb, wsu_ref[...])
    ff = _dot((g * jax.nn.sigmoid(g) * u).astype(BF16), wsd_ref[...])

    for kk in range(TOP_K):
        pltpu.make_async_copy(ys_hbm.at[pl.ds(0, tm), :], buf.at[kk], sem).wait()
    gate = gate_ref[...]
    for kk in range(TOP_K):
        ff = ff + gate[:, kk:kk + 1] * _unpack_bf16_pairs(buf[kk])
    y_ref[...] = _layer_norm(ALPHA * x1 + ff, g2_ref[...], b2_ref[...])


def _moe_combine(dest_flat, x1, gate, ys, wsg, wsu, wsd, ln2_g, ln2_b):
    R = x1.shape[0]
    W = ys.shape[1]

    def rows(w):
        return pl.BlockSpec((ROW_TILE, w), lambda i: (i, 0))

    def full(a):
        return pl.BlockSpec(a.shape, lambda i: (0,) * a.ndim)

    return pl.pallas_call(
        _combine_kernel,
        grid=(R // ROW_TILE,),
        in_specs=[pl.BlockSpec((ROW_TILE * TOP_K,), lambda i: (i,), memory_space=pltpu.SMEM),
                  rows(D_MODEL), rows(LANES), pl.BlockSpec(memory_space=pl.ANY),
                  full(wsg), full(wsu), full(wsd), full(ln2_g), full(ln2_b)],
        out_specs=rows(D_MODEL),
        out_shape=jax.ShapeDtypeStruct((R, D_MODEL), F32),
        scratch_shapes=[pltpu.VMEM((TOP_K, ROW_TILE, W), U32), pltpu.SemaphoreType.DMA],
        compiler_params=_cparams("arbitrary"),
        name="moe_combine",
    )(dest_flat, x1, gate, ys, wsg, wsu, wsd, ln2_g, ln2_b)


def _routing_tables(idx8, n_slots_blocks):
    R = idx8.shape[0]
    onehot = (idx8[:, :, None] == jnp.arange(N_EXPERTS, dtype=I32)[None, None, :]).astype(I32).sum(axis=1)
    counts = onehot.sum(axis=0)
    rank = jnp.cumsum(onehot, axis=0) - onehot
    padded = (counts + MOE_BLOCK - 1) // MOE_BLOCK * MOE_BLOCK
    pends = jnp.cumsum(padded)
    pstarts = pends - padded
    dest = jnp.take_along_axis(rank + pstarts[None, :], idx8, axis=1)
    blk_start = jnp.arange(n_slots_blocks, dtype=I32) * MOE_BLOCK
    block_expert = jnp.minimum(jnp.searchsorted(pends, blk_start, side="right"), N_EXPERTS - 1).astype(I32)
    n_used = (pends[-1] // MOE_BLOCK).astype(I32).reshape(1)
    return dest.reshape(R * TOP_K).astype(I32), pends.astype(I32), padded.astype(I32), block_expert, n_used


def kernel(x_prompt, x_sample, cache_mla_ckv, cache_mla_krope, state_hgrn, page_table, meta_tokens, hg_lb_logits,
           w_in, q_norm, kv_norm, w_uq, w_uk, w_uv, hg_norm, w_br_mla, w_br_hg, w_out, ln1_g, ln1_b, router_w,
           router_bias, w_exp_gate, w_exp_up, w_exp_down, w_sh_gate, w_sh_up, w_sh_down, ln2_g, ln2_b):
    assert w_in.shape[0] == DEPTH
    B, seq, _ = x_prompt.shape
    DB, steps, _ = x_sample.shape
    n_meta = meta_tokens.shape[0]
    n_pages, page = page_table.shape[1], cache_mla_ckv.shape[2]
    past = n_pages * page
    T = n_meta + seq
    tp = -(-T // ROW_TILE) * ROW_TILE
    pad_front = tp - T
    tpb = tp // ROW_TILE
    n_s = DB * steps
    assert n_s % ROW_TILE == 0 and DB % HG_SAMPLE_BATCH == 0
    geo = dict(B=B, tp=tp, tpb=tpb, pad_front=pad_front, npt=B * tpb)
    Rp = B * tp
    R = Rp + n_s

    xp = jnp.concatenate([jnp.zeros((B, pad_front, D_MODEL), F32),
                          jnp.broadcast_to(meta_tokens[None], (B, n_meta, D_MODEL)), x_prompt], axis=1)
    x_flat = jnp.concatenate([xp.reshape(Rp, D_MODEL), x_sample.reshape(n_s, D_MODEL)], axis=0)

    pos_p = jnp.maximum(jnp.arange(tp) - pad_front, 0)
    pos = jnp.concatenate([jnp.tile(pos_p, B), jnp.tile(past + jnp.arange(steps), DB)]).astype(F32)
    half = MLA_ROPE // 2
    inv = ROPE_THETA ** (-jnp.arange(half, dtype=F32) / half)
    ang = pos[:, None] * inv[None, :]
    cos8 = jnp.tile(jnp.cos(ang), (1, LANES // half))
    sin8 = jnp.tile(jnp.sin(ang), (1, LANES // half))

    l = 0
    win = w_in[l]
    kr_end = MLA_Q_RANK + MLA_KV_RANK + MLA_ROPE
    w_in_p = jnp.concatenate([win[:, :kr_end], jnp.zeros((D_MODEL, LANES - MLA_ROPE), F32), win[:, kr_end:]],
                             axis=1).astype(BF16)
    assert w_in_p.shape[1] == D_IN_PACKED
    wq3 = w_uq[l].reshape(MLA_Q_RANK, MLA_HEADS, MLA_NOPE + MLA_ROPE)
    wqr = (jnp.concatenate([wq3[:, :, MLA_NOPE:MLA_NOPE + half].reshape(MLA_Q_RANK, MLA_HEADS * half),
                            wq3[:, :, MLA_NOPE + half:].reshape(MLA_Q_RANK, MLA_HEADS * half)], axis=1)
           * MLA_SCALE).astype(BF16)
    src = jnp.arange(2 * LANES)
    hh, ii = (src % LANES) // half, src % half
    dst = hh * LANES + ii + jnp.where(src >= LANES, half, 0)
    perm = (dst[:, None] == jnp.arange(MLA_HEADS * LANES)[None, :]).astype(BF16)
    lb = jnp.cumsum(jax.nn.softmax(hg_lb_logits.astype(F32), axis=0), axis=0)[l].reshape(1, -1)

    wqlat, wo = _weight_prep(w_uq[l], w_uk[l], w_uv[l], w_br_mla[l])

    (qcat, kcat, ckv, kr, hq, hk, hv, lf, sg, sm, sh) = _in_projection(
        x_flat, w_in_p, q_norm[l].reshape(1, -1), kv_norm[l].reshape(1, -1), wqlat, wqr, perm, cos8, sin8, lb, geo)

    o_lat_p = _prompt_attention(qcat, kcat, geo)
    q_s = qcat[Rp:].reshape(DB, steps * MLA_HEADS, QCAT)
    knew_s = kcat[Rp:].reshape(DB, steps, 2 * LANES)
    o_lat_s = _sample_attention(page_table, q_s, knew_s, cache_mla_ckv[l], cache_mla_krope[l])
    o_lat = jnp.concatenate([o_lat_p, o_lat_s.reshape(n_s, MLA_HEADS * LANES)], axis=0)
    o_hg_p, s_prompt = _hgrn_prompt(hq, hk, hv, lf, geo)
    o_hg_s, s_sample = _hgrn_sample(hq, hk, hv, lf, state_hgrn[l], Rp, steps)
    o_hg = jnp.concatenate([o_hg_p, o_hg_s], axis=0)

    rw_p = jnp.concatenate([router_w[l], jnp.zeros((D_MODEL, LANES - N_EXPERTS), F32)], axis=1)
    rb_p = jnp.concatenate([router_bias[l].astype(F32), jnp.zeros((LANES - N_EXPERTS,), F32)]).reshape(1, -1)
    x1, x1p, idx, gate = _layer_tail(
        x_flat, o_lat, o_hg, sg, sm, sh, wo, w_br_hg[l].astype(BF16), w_out[l].astype(BF16),
        hg_norm[l].reshape(1, -1), ln1_g[l].reshape(1, -1), ln1_b[l].reshape(1, -1), rw_p, rb_p)

    nblk = R * TOP_K // MOE_BLOCK + N_EXPERTS
    dest_flat, pends, padded, block_expert, n_used = _routing_tables(idx[:, :TOP_K], nblk)
    xs = _moe_dispatch(pends, padded, dest_flat, x1p, nblk * MOE_BLOCK)
    ys = _moe_experts(block_expert, n_used, xs, w_exp_gate[l].astype(BF16), w_exp_up[l].astype(BF16),
                      w_exp_down[l].astype(BF16))
    y = _moe_combine(dest_flat, x1, gate, ys, w_sh_gate[l].astype(BF16), w_sh_up[l].astype(BF16),
                     w_sh_down[l].astype(BF16), ln2_g[l].reshape(1, -1), ln2_b[l].reshape(1, -1))

    yp = y[:Rp].reshape(B, tp, D_MODEL)[:, pad_front + n_meta:]
    ys_out = y[Rp:].reshape(DB, steps, D_MODEL)
    ckv_p = ckv[:Rp].reshape(B, tp, MLA_KV_RANK)[:, pad_front:][None]
    kr_p = kr[:Rp].reshape(B, tp, LANES)[:, pad_front:, :MLA_ROPE][None]
    ckv_s = ckv[Rp:].reshape(DB, steps, MLA_KV_RANK)[None]
    kr_s = kr[Rp:, :MLA_ROPE].reshape(DB, steps, MLA_ROPE)[None]
    return (yp, ys_out, ckv_p, kr_p, s_prompt[None], ckv_s, kr_s, s_sample[None])
```

```python
import functools

import jax
import jax.numpy as jnp
from jax import lax
from jax.experimental import pallas as pl
from jax.experimental.pallas import tpu as pltpu

F32 = jnp.float32
BF16 = jnp.bfloat16
U32 = jnp.uint32
I32 = jnp.int32

D_MODEL = 1024
MLA_HEADS = 8
MLA_Q_RANK = 256
MLA_KV_RANK = 128
MLA_NOPE = 64
MLA_ROPE = 32
MLA_V = 64
MLA_SCALE = (MLA_NOPE + MLA_ROPE) ** -0.5
LOG2E = 1.4426950408889634
Q_SCALE = MLA_SCALE * LOG2E
ROPE_THETA = 10000.0
HG_HEADS = 4
HG_DK = 128
HG_DV = 128
HG_CHUNK = 64
N_EXPERTS = 64
TOP_K = 8
D_EXPERT = 256
ROUTED_SCALE = 2.5
NORM_EPS = 1e-6
DEPTH = 1
ALPHA = (2.0 * DEPTH) ** 0.25

LANES = 128
ROW_TILE = 256
QCAT = 256
ATT_TK = 256
MOE_BLOCK = 512
PAGES_PER_CHUNK = 64
ONES_COL = LANES + MLA_ROPE
HG_SAMPLE_BATCH = 8
NEG_BIG = -1e30
VMEM_LIMIT = 52 * 1024 * 1024

C_CQ = (0, 256)
C_CKV = (256, 384)
C_KR = (384, 512)
C_HQ = (512, 1024)
C_HF = (1024, 1536)
C_HI = (1536, 2048)
C_HG = (2048, 2560)
C_GM = (2560, 3584)
C_GH = (3584, 4608)
D_IN_PACKED = 4608


def _cparams(*sem):
    return pltpu.CompilerParams(dimension_semantics=sem, vmem_limit_bytes=VMEM_LIMIT)


def _dot(a, b):
    return jnp.dot(a, b, preferred_element_type=F32)


def _dot_nt(a, b):
    return lax.dot_general(a, b, (((1,), (1,)), ((), ())), preferred_element_type=F32)


def _dot_tn(a, b):
    return lax.dot_general(a, b, (((0,), (0,)), ((), ())), preferred_element_type=F32)


def _pack_bf16_pairs(x):
    w = x.shape[1] // 2
    bits = pltpu.bitcast(x.astype(BF16).astype(F32), U32)
    return bits[:, w:] | (bits[:, :w] >> 16)


def _unpack_bf16_pairs(p):
    lo = pltpu.bitcast(p << 16, F32)
    hi = pltpu.bitcast(p & jnp.uint32(0xFFFF0000), F32)
    return jnp.concatenate([lo, hi], axis=1)


def _x_specs(geo, seq_tiles):
    B, tpb, npt = geo["B"], geo["tpb"], geo["npt"]

    def xp_map(i):
        return (jnp.minimum(i // tpb, B - 1), jnp.where(i < npt, jnp.maximum(i % tpb - 1, 0), seq_tiles - 1), 0)

    return [pl.BlockSpec((1, ROW_TILE, D_MODEL), xp_map),
            pl.BlockSpec((ROW_TILE, D_MODEL), lambda i: (0, 0)),
            pl.BlockSpec((ROW_TILE, D_MODEL), lambda i: (jnp.maximum(i - npt, 0), 0))]


def _select_x(i, xp_ref, xh_ref, xs_ref, tiles_per_batch, n_prompt_tiles):
    is_head = (i < n_prompt_tiles) & (i % tiles_per_batch == 0)
    return jnp.where(i >= n_prompt_tiles, xs_ref[...], jnp.where(is_head, xh_ref[...], xp_ref[0]))


def _split_specs(width, npt):
    return [pl.BlockSpec((ROW_TILE, width), lambda i: (jnp.minimum(i, npt - 1), 0)),
            pl.BlockSpec((ROW_TILE, width), lambda i: (jnp.maximum(i - npt, 0), 0))]


def _wprep_kernel(wq_nope_ref, wuk_ref, wuv_ref, wbr_ref, wqlat_ref, wo_ref):
    hp = lax.Precision.HIGHEST
    for h in range(MLA_HEADS):
        a = wq_nope_ref[h]
        b = wuk_ref[h]
        ql = lax.dot_general(a, b, (((1,), (1,)), ((), ())), precision=hp, preferred_element_type=F32)
        wqlat_ref[:, h * MLA_KV_RANK:(h + 1) * MLA_KV_RANK] = (ql * Q_SCALE).astype(BF16)
        c = wuv_ref[h]
        d = wbr_ref[h]
        wo_ref[h * MLA_KV_RANK:(h + 1) * MLA_KV_RANK, :] = jnp.dot(
            c, d, precision=hp, preferred_element_type=F32).astype(BF16)


def _weight_prep(w_uq, w_uk, w_uv, w_br_mla):
    wq3 = w_uq.reshape(MLA_Q_RANK, MLA_HEADS, MLA_NOPE + MLA_ROPE)
    wq_nope = jnp.transpose(wq3[:, :, :MLA_NOPE], (1, 0, 2))
    wuk = jnp.transpose(w_uk, (1, 0, 2))
    wuv = jnp.transpose(w_uv, (1, 0, 2))
    wbr = w_br_mla.reshape(MLA_HEADS, MLA_V, D_MODEL)
    return pl.pallas_call(
        _wprep_kernel,
        out_shape=(jax.ShapeDtypeStruct((MLA_Q_RANK, MLA_HEADS * MLA_KV_RANK), BF16),
                   jax.ShapeDtypeStruct((MLA_HEADS * MLA_KV_RANK, D_MODEL), BF16)),
        compiler_params=pltpu.CompilerParams(vmem_limit_bytes=VMEM_LIMIT),
        name="weight_prep",
    )(wq_nope, wuk, wuv, wbr)


def _inproj_kernel(xp_ref, xh_ref, xs_ref, w_ref, qn_ref, kvn_ref, wqlat_ref, wqr_ref, perm_ref, cos_ref, sin_ref,
                   lb_ref, qcat_ref, kcat_ref, ckv_ref, kr_ref, hq_ref, hk_ref, hv_ref, lf_ref, sg_ref, sm_ref,
                   sh_ref, *, pad_front, tiles_per_batch, n_prompt_tiles):
    i = pl.program_id(0)
    tm = xh_ref.shape[0]
    xb = _select_x(i, xp_ref, xh_ref, xs_ref, tiles_per_batch, n_prompt_tiles).astype(BF16)

    def proj(c):
        return _dot(xb, w_ref[:, c[0]:c[1]])

    cos8 = cos_ref[...]
    sin8 = sin_ref[...]

    cq = proj(C_CQ)
    cqn = cq * lax.rsqrt(jnp.mean(cq * cq, axis=-1, keepdims=True) + NORM_EPS) * qn_ref[...]
    cqb = cqn.astype(BF16)
    qlat = _dot(cqb, wqlat_ref[...])
    qr = _dot(cqb, wqr_ref[...])
    x1, x2 = qr[:, :LANES], qr[:, LANES:]
    qrot = jnp.concatenate([x1 * cos8 - x2 * sin8, x2 * cos8 + x1 * sin8], axis=1).astype(BF16)
    qrh = _dot(qrot, perm_ref[...])
    for h in range(MLA_HEADS):
        qcat_ref[h, :, :LANES] = qlat[:, h * LANES:(h + 1) * LANES].astype(BF16)
        qcat_ref[h, :, LANES:] = qrh[:, h * LANES:(h + 1) * LANES].astype(BF16)

    kv = proj(C_CKV)
    ckv = kv * lax.rsqrt(jnp.mean(kv * kv, axis=-1, keepdims=True) + NORM_EPS) * kvn_ref[...]
    ckv_ref[...] = ckv
    kcat_ref[:, :LANES] = ckv.astype(BF16)
    krr = proj(C_KR)
    lane = lax.broadcasted_iota(I32, (tm, LANES), 1)
    half = MLA_ROPE // 2
    rot = jnp.where(lane < half, -pltpu.roll(krr, LANES - half, 1), pltpu.roll(krr, half, 1))
    kr = jnp.where(lane < MLA_ROPE, krr * cos8 + rot * sin8, 0.0)
    kr_ref[...] = kr
    kcat_ref[:, LANES:] = jnp.where(lane == ONES_COL - LANES, 1.0, kr).astype(BF16)

    row = lax.broadcasted_iota(I32, (tm, 1), 0)
    is_pad = (i < n_prompt_tiles) & (i % tiles_per_batch == 0) & (row < pad_front)
    keep = jnp.where(is_pad, 0.0, 1.0)
    hq_ref[...] = proj(C_HQ).astype(BF16)
    lb = lb_ref[...]
    f = lb + (1.0 - lb) * jax.nn.sigmoid(proj(C_HF))
    lf_ref[...] = jnp.log(f) * keep
    hk_ref[...] = ((1.0 - f) * keep).astype(BF16)
    hv_ref[...] = proj(C_HI).astype(BF16)

    g = proj(C_HG)
    sg_ref[...] = (g * jax.nn.sigmoid(g)).astype(BF16)
    sm_ref[...] = jax.nn.sigmoid(proj(C_GM)).astype(BF16)
    sh_ref[...] = jax.nn.sigmoid(proj(C_GH)).astype(BF16)


def _in_projection(x_prompt, x_head, x_smp, w_in_p, q_norm, kv_norm, wqlat, wqr, perm, cos8, sin8, lb, geo):
    R = cos8.shape[0]
    nt = R // ROW_TILE
    hgw = HG_HEADS * HG_DK
    x_specs = _x_specs(geo, x_prompt.shape[1] // ROW_TILE)

    def rows(w):
        return pl.BlockSpec((ROW_TILE, w), lambda i: (i, 0))

    def full(a):
        return pl.BlockSpec(a.shape, lambda i: (0,) * a.ndim)

    out_widths = [(2 * LANES, BF16), (LANES, F32), (LANES, F32),
                  (hgw, BF16), (hgw, BF16), (hgw, BF16), (hgw, F32), (hgw, BF16),
                  (D_MODEL, BF16), (D_MODEL, BF16)]
    qcat_spec = pl.BlockSpec((MLA_HEADS, ROW_TILE, QCAT), lambda i: (0, i, 0))
    return pl.pallas_call(
        functools.partial(_inproj_kernel, pad_front=geo["pad_front"], tiles_per_batch=geo["tpb"],
                          n_prompt_tiles=geo["npt"]),
        grid=(nt,),
        in_specs=x_specs + [full(w_in_p), full(q_norm), full(kv_norm), full(wqlat), full(wqr), full(perm),
                            rows(LANES), rows(LANES), full(lb)],
        out_specs=[qcat_spec] + [rows(w) for w, _ in out_widths],
        out_shape=[jax.ShapeDtypeStruct((MLA_HEADS, R, QCAT), BF16)]
        + [jax.ShapeDtypeStruct((R, w), dt) for w, dt in out_widths],
        compiler_params=_cparams("arbitrary"),
        name="in_projection",
    )(x_prompt, x_head, x_smp, w_in_p, q_norm, kv_norm, wqlat, wqr, perm, cos8, sin8, lb)


def _softmax_step(s, v_b, m_ref, l_ref, acc_ref):
    n = s.shape[1] // LANES
    m_prev = m_ref[...]
    m_next = jnp.maximum(m_prev, jnp.max(s, axis=1, keepdims=True))
    p = jnp.concatenate([jnp.exp2(s[:, j * LANES:(j + 1) * LANES] - m_next) for j in range(n)], axis=1)
    alpha = jnp.exp2(m_prev - m_next)
    pv = _dot(p.astype(BF16), v_b)
    if v_b.shape[1] == LANES:
        l_ref[...] = alpha * l_ref[...] + jnp.sum(p, axis=1, keepdims=True)
    else:
        l_ref[...] = alpha * l_ref[...] + pv[:, LANES:]
    acc_ref[...] = alpha * acc_ref[...] + pv[:, :LANES]
    m_ref[...] = m_next


def _pattn_kernel(q_ref, k_ref, o_ref, m_ref, l_ref, acc_ref, *, pad_front):
    qi = pl.program_id(1)
    nh, tq, _ = q_ref.shape
    tk = ATT_TK
    rows = nh * tq

    q = q_ref[...].reshape(rows, QCAT)
    m_ref[...] = jnp.full(m_ref.shape, -jnp.inf, F32)
    l_ref[...] = jnp.zeros(l_ref.shape, F32)
    acc_ref[...] = jnp.zeros(acc_ref.shape, F32)

    def step(kb, masked):
        kblk = k_ref[pl.ds(pl.multiple_of(kb * tk, tk), tk), :]
        s = _dot_nt(q, kblk)
        if masked:
            qrow = qi * tq + lax.broadcasted_iota(I32, (rows, tk), 0) % tq
            krow = kb * tk + lax.broadcasted_iota(I32, (rows, tk), 1)
            s = jnp.where((krow <= qrow) & (krow >= pad_front), s, NEG_BIG)
        _softmax_step(s, kblk, m_ref, l_ref, acc_ref)

    step(0, True)

    def body(kb, c):
        step(kb, False)
        return c

    lax.fori_loop(1, qi, body, 0)

    @pl.when(qi > 0)
    def _():
        step(qi, True)

    row_sum = l_ref[:, ONES_COL - LANES:ONES_COL - LANES + 1]
    o = (acc_ref[...] / row_sum).astype(BF16)
    for h in range(nh):
        o_ref[:, h * LANES:(h + 1) * LANES] = o[h * tq:(h + 1) * tq]


def _prompt_attention(qcat, kcat, geo):
    B, tpb, tp = geo["B"], geo["tpb"], geo["tp"]
    rows = MLA_HEADS * ROW_TILE
    return pl.pallas_call(
        functools.partial(_pattn_kernel, pad_front=geo["pad_front"]),
        grid=(B, tpb),
        in_specs=[pl.BlockSpec((MLA_HEADS, ROW_TILE, QCAT), lambda b, i: (0, b * tpb + i, 0)),
                  pl.BlockSpec((tp, 2 * LANES), lambda b, i: (b, 0))],
        out_specs=pl.BlockSpec((ROW_TILE, MLA_HEADS * LANES), lambda b, i: (b * tpb + i, 0)),
        out_shape=jax.ShapeDtypeStruct((B * tp, MLA_HEADS * LANES), BF16),
        scratch_shapes=[pltpu.VMEM((rows, LANES), F32)] * 3,
        compiler_params=_cparams("arbitrary", "arbitrary"),
        name="prompt_attention",
    )(qcat, kcat)


def _sattn_kernel(pt_ref, q_ref, knew_ref, ckv_hbm, kr_hbm, o_ref,
                  ckv_buf, kr_buf, sem, m_ref, l_ref, acc_ref, *, n_chunks, page):
    b = pl.program_id(0)
    nb = pl.num_programs(0)
    ch = PAGES_PER_CHUNK

    def chunk_copies(bb, c, slot):
        cps = []
        for j in range(ch):
            pg = pt_ref[bb, c * ch + j]
            cps.append(pltpu.make_async_copy(ckv_hbm.at[pg], ckv_buf.at[slot, pl.ds(j * page, page), :],
                                             sem.at[0, slot]))
            cps.append(pltpu.make_async_copy(kr_hbm.at[pg], kr_buf.at[slot, :, pl.ds(j * page, page)],
                                             sem.at[1, slot]))
        return cps

    @pl.when(b == 0)
    def _():
        for cp in chunk_copies(0, 0, 0):
            cp.start()

    nh, S = q_ref.shape[0], q_ref.shape[2]
    rows = nh * S
    q = q_ref[...].reshape(rows, QCAT)
    qlat = q[:, :LANES]
    qrope = q[:, LANES:LANES + MLA_ROPE]

    knew = knew_ref[0]
    s_new = _dot_nt(q, knew)
    qtok = lax.broadcasted_iota(I32, (rows, S), 0) % S
    ktok = lax.broadcasted_iota(I32, (rows, S), 1)
    s_new = jnp.where(ktok <= qtok, s_new, NEG_BIG)
    m0 = jnp.max(s_new, axis=1, keepdims=True)
    p0 = jnp.exp2(s_new - m0)
    m_ref[...] = jnp.broadcast_to(m0, m_ref.shape)
    l_ref[...] = jnp.broadcast_to(jnp.sum(p0, axis=1, keepdims=True), l_ref.shape)
    acc_ref[...] = _dot(p0.astype(BF16), knew[:, :LANES])

    for c in range(n_chunks):
        slot = c % 2 if n_chunks % 2 == 0 else (b * n_chunks + c) % 2
        if c + 1 < n_chunks:
            for cp in chunk_copies(b, c + 1, 1 - slot):
                cp.start()
        else:
            @pl.when(b + 1 < nb)
            def _():
                for cp in chunk_copies(b + 1, 0, 1 - slot):
                    cp.start()
        for cp in chunk_copies(b, c, slot):
            cp.wait()
        ckv_b = ckv_buf[slot].astype(BF16)
        krt_b = kr_buf[slot].astype(BF16)
        s = _dot_nt(qlat, ckv_b) + _dot(qrope, krt_b)
        _softmax_step(s, ckv_b, m_ref, l_ref, acc_ref)

    o = (acc_ref[...] / l_ref[...]).astype(BF16)
    for h in range(nh):
        o_ref[0, :, h * LANES:(h + 1) * LANES] = o[h * S:(h + 1) * S]


def _sample_attention(page_table, q_s, knew_s, cache_ckv, cache_krt):
    nh, DB, S, _ = q_s.shape
    rows = nh * S
    n_pages = page_table.shape[1]
    page = cache_ckv.shape[1]
    assert n_pages % PAGES_PER_CHUNK == 0
    n_chunks = n_pages // PAGES_PER_CHUNK
    ck = PAGES_PER_CHUNK * page
    grid_spec = pltpu.PrefetchScalarGridSpec(
        num_scalar_prefetch=1,
        grid=(DB,),
        in_specs=[pl.BlockSpec((nh, 1, S, QCAT), lambda b, pt: (0, b, 0, 0)),
                  pl.BlockSpec((1, S, 2 * LANES), lambda b, pt: (b, 0, 0)),
                  pl.BlockSpec(memory_space=pl.ANY),
                  pl.BlockSpec(memory_space=pl.ANY)],
        out_specs=pl.BlockSpec((1, S, nh * LANES), lambda b, pt: (b, 0, 0)),
        scratch_shapes=[pltpu.VMEM((2, ck, MLA_KV_RANK), F32),
                        pltpu.VMEM((2, MLA_ROPE, ck), F32),
                        pltpu.SemaphoreType.DMA((2, 2)),
                        pltpu.VMEM((rows, LANES), F32),
                        pltpu.VMEM((rows, LANES), F32),
                        pltpu.VMEM((rows, LANES), F32)])
    return pl.pallas_call(
        functools.partial(_sattn_kernel, n_chunks=n_chunks, page=page),
        grid_spec=grid_spec,
        out_shape=jax.ShapeDtypeStruct((DB, S, nh * LANES), BF16),
        compiler_params=_cparams("arbitrary"),
        name="sample_attention",
    )(page_table, q_s, knew_s, cache_ckv, cache_krt)


def _split3(x):
    hi = x.astype(BF16)
    r1 = x - hi.astype(F32)
    mid = r1.astype(BF16)
    lo = (r1 - mid.astype(F32)).astype(BF16)
    return hi, mid, lo


def _hgrn_chunk(q, k, v, lf, S0, tri, group):
    C = q.shape[0]
    hi, mid, lo = _split3(lf)
    cum = _dot(tri, hi) + _dot(tri, mid) + _dot(tri, lo)
    vb = v.astype(BF16)
    rowi = lax.broadcasted_iota(I32, (C, C), 0)
    coli = lax.broadcasted_iota(I32, (C, C), 1)
    attn = jnp.zeros((C, C), F32)

    bd = min(16, group)
    hs = group // 2
    rid = lax.broadcasted_iota(I32, (C, 1), 0)
    while hs >= bd:
        npair = C // (2 * hs)
        ref = jnp.concatenate(
            [jnp.broadcast_to(cum[(2 * j + 1) * hs - 1:(2 * j + 1) * hs, :], (2 * hs, cum.shape[1]))
             for j in range(npair)], axis=0)
        odd = ((rid // hs) % 2) == 1
        e = jnp.exp(jnp.where(odd, cum - ref, ref - cum))
        qs = jnp.where(odd, q * e, 0.0).astype(BF16)
        ks = jnp.where(odd, 0.0, k * e).astype(BF16)
        a = _dot_nt(qs, ks)
        attn = attn + jnp.where((rowi // (2 * hs)) == (coli // (2 * hs)), a, 0.0)
        hs //= 2

    nblk = C // bd
    k3 = k.reshape(nblk, bd, k.shape[1])
    c3 = cum.reshape(nblk, bd, cum.shape[1])
    tl = rid % bd
    blk_base = (rowi // bd) * bd
    for sl in range(bd):
        ks_b = jnp.broadcast_to(k3[:, sl:sl + 1, :], k3.shape).reshape(C, k.shape[1])
        cs_b = jnp.broadcast_to(c3[:, sl:sl + 1, :], c3.shape).reshape(C, k.shape[1])
        e = jnp.exp(jnp.where(tl >= sl, cum - cs_b, NEG_BIG))
        col = jnp.sum(q * ks_b * e, axis=1, keepdims=True)
        attn = jnp.where(coli == blk_base + sl, col, attn)
    o_intra = _dot(attn.astype(BF16), vb)
    return o_intra, cum


def _tri(C, group):
    r = lax.broadcasted_iota(I32, (C, C), 0)
    c = lax.broadcasted_iota(I32, (C, C), 1)
    return jnp.where((c <= r) & (r // group == c // group), 1.0, 0.0).astype(BF16)


def _state_update(q, k, v, cum, S, o_intra):
    C = q.shape[0]
    last = cum[C - 1:C, :]
    o = o_intra + _dot((q * jnp.exp(cum)).astype(BF16), S.astype(BF16))
    kst = (k * jnp.exp(last - cum)).astype(BF16)
    dfull = jnp.transpose(jnp.broadcast_to(jnp.exp(last), (S.shape[1], S.shape[0])))
    S_new = dfull * S + _dot_tn(kst, v.astype(BF16))
    return o, S_new


def _hgrn_prompt_kernel(q_ref, k_ref, v_ref, lf_ref, o_ref, s_out_ref, s_ref):
    t = pl.program_id(1)

    @pl.when(t == 0)
    def _():
        s_ref[...] = jnp.zeros(s_ref.shape, F32)

    C = HG_CHUNK
    tri = _tri(C, C)
    states = [s_ref[h] for h in range(HG_HEADS)]
    for c in range(q_ref.shape[0] // C):
        sl = slice(c * C, (c + 1) * C)
        for h in range(HG_HEADS):
            hl = slice(h * HG_DK, (h + 1) * HG_DK)
            q = q_ref[sl, hl].astype(F32)
            k = k_ref[sl, hl].astype(F32)
            v = v_ref[sl, hl].astype(F32)
            lf = lf_ref[sl, hl]
            o_intra, cum = _hgrn_chunk(q, k, v, lf, None, tri, C)
            o, states[h] = _state_update(q, k, v, cum, states[h], o_intra)
            o_ref[sl, hl] = o.astype(BF16)
    for h in range(HG_HEADS):
        s_ref[h] = states[h]

    @pl.when(t == pl.num_programs(1) - 1)
    def _():
        s_out_ref[0] = s_ref[...]


def _hgrn_prompt(hq, hk, hv, lf, geo):
    B, tpb, tp = geo["B"], geo["tpb"], geo["tp"]
    hgw = HG_HEADS * HG_DK

    def blk():
        return pl.BlockSpec((ROW_TILE, hgw), lambda b, t: (b * tpb + t, 0))

    return pl.pallas_call(
        _hgrn_prompt_kernel,
        grid=(B, tpb),
        in_specs=[blk(), blk(), blk(), blk()],
        out_specs=[blk(), pl.BlockSpec((1, HG_HEADS, HG_DK, HG_DV), lambda b, t: (b, 0, 0, 0))],
        out_shape=[jax.ShapeDtypeStruct((B * tp, hgw), BF16),
                   jax.ShapeDtypeStruct((B, HG_HEADS, HG_DK, HG_DV), F32)],
        scratch_shapes=[pltpu.VMEM((HG_HEADS, HG_DK, HG_DV), F32)],
        compiler_params=_cparams("arbitrary", "arbitrary"),
        name="hgrn_prompt",
    )(hq, hk, hv, lf)


def _hgrn_sample_kernel(q_ref, k_ref, v_ref, lf_ref, s_in_ref, o_ref, s_out_ref, *, steps):
    nb = s_in_ref.shape[0]
    C = nb * steps
    tri = _tri(C, steps)
    for h in range(HG_HEADS):
        sl = slice(h * HG_DK, (h + 1) * HG_DK)
        q = q_ref[:, sl].astype(F32)
        k = k_ref[:, sl].astype(F32)
        v = v_ref[:, sl].astype(F32)
        lf = lf_ref[:, sl]
        o_intra, cum = _hgrn_chunk(q, k, v, lf, None, tri, steps)
        for b in range(nb):
            r = slice(b * steps, (b + 1) * steps)
            o, S_new = _state_update(q[r], k[r], v[r], cum[r], s_in_ref[b, h], o_intra[r])
            o_ref[r, sl] = o.astype(BF16)
            s_out_ref[b, h] = S_new


def _hgrn_sample(hq, hk, hv, lf, state, row0, steps):
    DB = state.shape[0]
    nb = HG_SAMPLE_BATCH
    rows = nb * steps
    hgw = HG_HEADS * HG_DK
    blk0 = row0 // rows

    def tok():
        return pl.BlockSpec((rows, hgw), lambda i: (blk0 + i, 0))

    st = pl.BlockSpec((nb, HG_HEADS, HG_DK, HG_DV), lambda i: (i, 0, 0, 0))
    return pl.pallas_call(
        functools.partial(_hgrn_sample_kernel, steps=steps),
        grid=(DB // nb,),
        in_specs=[tok(), tok(), tok(), tok(), st],
        out_specs=[pl.BlockSpec((rows, hgw), lambda i: (i, 0)), st],
        out_shape=[jax.ShapeDtypeStruct((DB * steps, hgw), BF16),
                   jax.ShapeDtypeStruct(state.shape, F32)],
        compiler_params=_cparams("arbitrary"),
        name="hgrn_sample",
    )(hq, hk, hv, lf, state)


def _layer_norm(x, g, b):
    mu = jnp.mean(x, axis=-1, keepdims=True)
    xc = x - mu
    var = jnp.mean(xc * xc, axis=-1, keepdims=True)
    return xc * lax.rsqrt(var + NORM_EPS) * g + b


def _tail_kernel(xp_ref, xh_ref, xs_ref, olp_ref, ols_ref, ohp_ref, ohs_ref, sg_ref, sm_ref, sh_ref, wo_ref, wbh_ref,
                 wout_ref, hgn_ref, g1_ref, b1_ref, rw_ref, rb_ref, x1_ref, x1p_ref, idx_ref, gate_ref, rank_ref,
                 count_ref, cnt_ref, *, tiles_per_batch, n_prompt_tiles):
    i = pl.program_id(0)
    tm = xh_ref.shape[0]
    x_in = _select_x(i, xp_ref, xh_ref, xs_ref, tiles_per_batch, n_prompt_tiles)
    is_sample = i >= n_prompt_tiles
    mla = _dot(jnp.where(is_sample, ols_ref[...], olp_ref[...]), wo_ref[...])
    oh = jnp.where(is_sample, ohs_ref[...], ohp_ref[...]).astype(F32)
    parts = []
    for h in range(HG_HEADS):
        y = oh[:, h * HG_DV:(h + 1) * HG_DV]
        parts.append(y * lax.rsqrt(jnp.mean(y * y, axis=-1, keepdims=True) + NORM_EPS) * hgn_ref[...])
    hg = (jnp.concatenate(parts, axis=1) * sg_ref[...].astype(F32)).astype(BF16)
    merged = sm_ref[...].astype(F32) * mla + sh_ref[...].astype(F32) * _dot(hg, wbh_ref[...])
    x1 = _layer_norm(ALPHA * x_in + _dot(merged.astype(BF16), wout_ref[...]), g1_ref[...], b1_ref[...])
    x1_ref[...] = x1
    x1p_ref[...] = _pack_bf16_pairs(x1)

    x1_hi = x1.astype(BF16)
    x1_lo = (x1 - x1_hi.astype(F32)).astype(BF16)
    logits = _dot(x1_hi, rw_ref[0]) + _dot(x1_lo, rw_ref[0]) + _dot(x1_hi, rw_ref[1])
    scores = jax.nn.sigmoid(logits)
    lane = lax.broadcasted_iota(I32, (tm, LANES), 1).astype(F32)
    remaining = jnp.where(lane < N_EXPERTS, scores + rb_ref[...], -jnp.inf)
    idx_out = jnp.zeros((tm, LANES), F32)
    gate_out = jnp.zeros((tm, LANES), F32)
    hits = []
    for kk in range(TOP_K):
        mx = jnp.max(remaining, axis=1, keepdims=True)
        pick = jnp.min(jnp.where(remaining == mx, lane, float(LANES)), axis=1, keepdims=True)
        hit = lane == pick
        hits.append(hit)
        gval = jnp.sum(jnp.where(hit, scores, 0.0), axis=1, keepdims=True)
        idx_out = jnp.where(lane == kk, pick, idx_out)
        gate_out = jnp.where(lane == kk, gval, gate_out)
        remaining = jnp.where(hit, -jnp.inf, remaining)
    gate_out = gate_out / jnp.sum(gate_out, axis=1, keepdims=True) * ROUTED_SCALE
    idx_ref[...] = idx_out.astype(I32)
    gate_ref[...] = gate_out

    @pl.when(pl.program_id(0) == 0)
    def _():
        cnt_ref[...] = jnp.zeros(cnt_ref.shape, F32)

    sel = jnp.where(remaining == -jnp.inf, 1.0, 0.0) * jnp.where(lane < N_EXPERTS, 1.0, 0.0)
    r_i = lax.broadcasted_iota(I32, (tm, tm), 0)
    c_i = lax.broadcasted_iota(I32, (tm, tm), 1)
    before = _dot(jnp.where(c_i < r_i, 1.0, 0.0).astype(BF16), sel.astype(BF16)) + cnt_ref[0:1, :]
    rank_out = jnp.zeros((tm, LANES), F32)
    for kk in range(TOP_K):
        rk = jnp.sum(jnp.where(hits[kk], before, 0.0), axis=1, keepdims=True)
        rank_out = jnp.where(lane == kk, rk, rank_out)
    rank_ref[...] = rank_out.astype(I32)
    total = cnt_ref[0:1, :] + jnp.sum(sel, axis=0, keepdims=True)
    cnt_ref[...] = jnp.broadcast_to(total, cnt_ref.shape)
    count_ref[...] = jnp.broadcast_to(total, count_ref.shape).astype(I32)


def _layer_tail(x_prompt, x_head, x_smp, o_lat_p, o_lat_s, o_hg_p, o_hg_s, sg, sm, sh, wo, w_br_hg, w_out, hg_norm,
                ln1_g, ln1_b, rw_p, rb_p, geo):
    R = sg.shape[0]
    npt = geo["npt"]

    def rows(w):
        return pl.BlockSpec((ROW_TILE, w), lambda i: (i, 0))

    def full(a):
        return pl.BlockSpec(a.shape, lambda i: (0,) * a.ndim)

    hgw = HG_HEADS * HG_DV
    return pl.pallas_call(
        functools.partial(_tail_kernel, tiles_per_batch=geo["tpb"], n_prompt_tiles=npt),
        grid=(R // ROW_TILE,),
        in_specs=_x_specs(geo, x_prompt.shape[1] // ROW_TILE)
        + _split_specs(MLA_HEADS * LANES, npt) + _split_specs(hgw, npt)
        + [rows(hgw), rows(D_MODEL), rows(D_MODEL),
           full(wo), full(w_br_hg), full(w_out), full(hg_norm), full(ln1_g), full(ln1_b),
           full(rw_p), full(rb_p)],
        out_specs=[rows(D_MODEL), rows(D_MODEL // 2), rows(LANES), rows(LANES), rows(LANES),
                   pl.BlockSpec((8, LANES), lambda i: (0, 0))],
        out_shape=[jax.ShapeDtypeStruct((R, D_MODEL), F32), jax.ShapeDtypeStruct((R, D_MODEL // 2), U32),
                   jax.ShapeDtypeStruct((R, LANES), I32), jax.ShapeDtypeStruct((R, LANES), F32),
                   jax.ShapeDtypeStruct((R, LANES), I32), jax.ShapeDtypeStruct((8, LANES), I32)],
        scratch_shapes=[pltpu.VMEM((8, LANES), F32)],
        compiler_params=_cparams("arbitrary"),
        name="layer_tail",
    )(x_prompt, x_head, x_smp, o_lat_p, o_lat_s, o_hg_p, o_hg_s, sg, sm, sh, wo, w_br_hg, w_out, hg_norm,
      ln1_g, ln1_b, rw_p, rb_p)


def _dispatch_kernel(pends_ref, padded_ref, dest_ref, x_ref, xs_hbm, zero_buf, sem, zsem):
    i = pl.program_id(0)
    tm = x_ref.shape[0]

    @pl.when(i == 0)
    def _():
        zero_buf[...] = jnp.zeros(zero_buf.shape, U32)

        def zcopy(e):
            start = pl.multiple_of(pends_ref[e] - MOE_BLOCK, MOE_BLOCK)
            return pltpu.make_async_copy(zero_buf, xs_hbm.at[pl.ds(start, MOE_BLOCK), :], zsem)

        n_slots = xs_hbm.shape[0]

        def tcopy(j):
            start = pl.multiple_of(pends_ref[N_EXPERTS - 1] + j * MOE_BLOCK, MOE_BLOCK)
            return pltpu.make_async_copy(zero_buf, xs_hbm.at[pl.ds(start, MOE_BLOCK), :], zsem)

        def tail_live(j):
            return pends_ref[N_EXPERTS - 1] + (j + 1) * MOE_BLOCK <= n_slots

        for e in range(N_EXPERTS):
            @pl.when(padded_ref[e] > 0)
            def _():
                zcopy(e).start()

            @pl.when(tail_live(e))
            def _():
                tcopy(e).start()
        for e in range(N_EXPERTS):
            @pl.when(padded_ref[e] > 0)
            def _():
                zcopy(e).wait()

            @pl.when(tail_live(e))
            def _():
                tcopy(e).wait()

    for r in range(tm):
        for kk in range(TOP_K):
            d = dest_ref[r * TOP_K + kk]
            pltpu.make_async_copy(x_ref.at[pl.ds(r, 1), :], xs_hbm.at[pl.ds(d, 1), :], sem).start(priority=kk % 2)
    for kk in range(TOP_K):
        pltpu.make_async_copy(x_ref, xs_hbm.at[pl.ds(0, tm), :], sem).wait()


def _moe_dispatch(pends, padded, dest_flat, x1p, n_slots):
    R, W = x1p.shape
    grid_spec = pltpu.PrefetchScalarGridSpec(
        num_scalar_prefetch=2,
        grid=(R // ROW_TILE,),
        in_specs=[pl.BlockSpec((ROW_TILE * TOP_K,), lambda i, pe, pa: (i,), memory_space=pltpu.SMEM),
                  pl.BlockSpec((ROW_TILE, W), lambda i, pe, pa: (i, 0))],
        out_specs=pl.BlockSpec(memory_space=pl.ANY),
        scratch_shapes=[pltpu.VMEM((MOE_BLOCK, W), U32), pltpu.SemaphoreType.DMA, pltpu.SemaphoreType.DMA])
    return pl.pallas_call(
        _dispatch_kernel,
        grid_spec=grid_spec,
        out_shape=jax.ShapeDtypeStruct((n_slots, W), U32),
        compiler_params=_cparams("arbitrary"),
        name="moe_dispatch",
    )(pends, padded, dest_flat, x1p)


def _expert_kernel(be_ref, nused_ref, xs_ref, wg_ref, wu_ref, wd_ref, ys_ref, wgb_ref, wub_ref, wdb_ref):
    i = pl.program_id(0)

    @pl.when((i == 0) | (be_ref[i] != be_ref[jnp.maximum(i - 1, 0)]))
    def _():
        wgb_ref[...] = wg_ref[0].astype(BF16)
        wub_ref[...] = wu_ref[0].astype(BF16)
        wdb_ref[...] = wd_ref[0].astype(BF16)

    @pl.when(i < nused_ref[0])
    def _():
        xb = _unpack_bf16_pairs(xs_ref[...]).astype(BF16)
        g = _dot(xb, wgb_ref[...])
        u = _dot(xb, wub_ref[...])
        hmid = (g * jax.nn.sigmoid(g) * u).astype(BF16)
        ys_ref[...] = _pack_bf16_pairs(_dot(hmid, wdb_ref[...]))

    @pl.when(i >= nused_ref[0])
    def _():
        ys_ref[...] = jnp.zeros(ys_ref.shape, U32)


def _moe_experts(block_expert, n_used, xs, wg, wu, wd):
    n_slots, W = xs.shape
    nblk = n_slots // MOE_BLOCK

    def xmap(i, be, nu):
        return (jnp.minimum(i, nu[0] - 1), 0)

    def wmap(i, be, nu):
        return (be[i], 0, 0)

    grid_spec = pltpu.PrefetchScalarGridSpec(
        num_scalar_prefetch=2,
        grid=(nblk,),
        in_specs=[pl.BlockSpec((MOE_BLOCK, W), xmap),
                  pl.BlockSpec((1, D_MODEL, D_EXPERT), wmap),
                  pl.BlockSpec((1, D_MODEL, D_EXPERT), wmap),
                  pl.BlockSpec((1, D_EXPERT, D_MODEL), wmap)],
        out_specs=pl.BlockSpec((MOE_BLOCK, W), lambda i, be, nu: (i, 0)),
        scratch_shapes=[pltpu.VMEM((D_MODEL, D_EXPERT), BF16), pltpu.VMEM((D_MODEL, D_EXPERT), BF16),
                        pltpu.VMEM((D_EXPERT, D_MODEL), BF16)])
    return pl.pallas_call(
        _expert_kernel,
        grid_spec=grid_spec,
        out_shape=jax.ShapeDtypeStruct((n_slots, W), U32),
        compiler_params=_cparams("arbitrary"),
        name="moe_experts",
    )(block_expert, n_used, xs, wg, wu, wd)


def _combine_kernel(dest_ref, x1_ref, gate_ref, ys_hbm, wsg_ref, wsu_ref, wsd_ref, g2_ref, b2_ref,
                    yp_ref, ysmp_ref, buf, sem, *, n_prompt_tiles):
    i = pl.program_id(0)
    tm = x1_ref.shape[0]

    for r in range(tm):
        for kk in range(TOP_K):
            d = dest_ref[r * TOP_K + kk]
            pltpu.make_async_copy(ys_hbm.at[pl.ds(d, 1), :], buf.at[kk, pl.ds(r, 1), :], sem).start(priority=kk % 2)

    x1 = x1_ref[...]
    xb = x1.astype(BF16)
    g = _dot(xb, wsg_ref[...])
    u = _dot(xb, wsu_ref[...])
    ff = _dot((g * jax.nn.sigmoid(g) * u).astype(BF16), wsd_ref[...])

    for kk in range(TOP_K):
        pltpu.make_async_copy(ys_hbm.at[pl.ds(0, tm), :], buf.at[kk], sem).wait()
    gate = gate_ref[...]
    for kk in range(TOP_K):
        ff = ff + gate[:, kk:kk + 1] * _unpack_bf16_pairs(buf[kk])
    y = _layer_norm(ALPHA * x1 + ff, g2_ref[...], b2_ref[...])

    @pl.when(i < n_prompt_tiles)
    def _():
        yp_ref[0] = y

    @pl.when(i >= n_prompt_tiles)
    def _():
        ysmp_ref[...] = y


def _moe_combine(dest_flat, x1, gate, ys, wsg, wsu, wsd, ln2_g, ln2_b, geo, seq):
    R = x1.shape[0]
    W = ys.shape[1]
    B, tpb, npt = geo["B"], geo["tpb"], geo["npt"]
    assert seq == (tpb - 1) * ROW_TILE

    def rows(w):
        return pl.BlockSpec((ROW_TILE, w), lambda i: (i, 0))

    def full(a):
        return pl.BlockSpec(a.shape, lambda i: (0,) * a.ndim)

    def yp_map(i):
        in_prompt = i < npt
        return (jnp.minimum(i // tpb, B - 1), jnp.where(in_prompt, jnp.maximum(i % tpb - 1, 0), tpb - 2), 0)

    return pl.pallas_call(
        functools.partial(_combine_kernel, n_prompt_tiles=npt),
        grid=(R // ROW_TILE,),
        in_specs=[pl.BlockSpec((ROW_TILE * TOP_K,), lambda i: (i,), memory_space=pltpu.SMEM),
                  rows(D_MODEL), rows(LANES), pl.BlockSpec(memory_space=pl.ANY),
                  full(wsg), full(wsu), full(wsd), full(ln2_g), full(ln2_b)],
        out_specs=[pl.BlockSpec((1, ROW_TILE, D_MODEL), yp_map),
                   pl.BlockSpec((ROW_TILE, D_MODEL), lambda i: (jnp.maximum(i - npt, 0), 0))],
        out_shape=[jax.ShapeDtypeStruct((B, seq, D_MODEL), F32),
                   jax.ShapeDtypeStruct((R - npt * ROW_TILE, D_MODEL), F32)],
        scratch_shapes=[pltpu.VMEM((TOP_K, ROW_TILE, W), U32), pltpu.SemaphoreType.DMA],
        compiler_params=_cparams("arbitrary"),
        name="moe_combine",
    )(dest_flat, x1, gate, ys, wsg, wsu, wsd, ln2_g, ln2_b)


def _routing_tables(idx8, rank8, counts, n_slots_blocks):
    R = idx8.shape[0]
    padded = (counts + MOE_BLOCK - 1) // MOE_BLOCK * MOE_BLOCK
    pends = jnp.cumsum(padded)
    pstarts = pends - padded
    experts = jnp.arange(N_EXPERTS, dtype=I32)
    start8 = jnp.sum(jnp.where(idx8[:, :, None] == experts[None, None, :], pstarts[None, None, :], 0), axis=-1)
    dest = start8 + rank8
    blk_start = jnp.arange(n_slots_blocks, dtype=I32) * MOE_BLOCK
    block_expert = jnp.minimum(jnp.sum((blk_start[:, None] >= pends[None, :]).astype(I32), axis=1), N_EXPERTS - 1)
    n_used = (pends[-1] // MOE_BLOCK).astype(I32).reshape(1)
    return dest.reshape(R * TOP_K).astype(I32), pends.astype(I32), padded.astype(I32), block_expert, n_used


def kernel(x_prompt, x_sample, cache_mla_ckv, cache_mla_krope, state_hgrn, page_table, meta_tokens, hg_lb_logits,
           w_in, q_norm, kv_norm, w_uq, w_uk, w_uv, hg_norm, w_br_mla, w_br_hg, w_out, ln1_g, ln1_b, router_w,
           router_bias, w_exp_gate, w_exp_up, w_exp_down, w_sh_gate, w_sh_up, w_sh_down, ln2_g, ln2_b):
    assert w_in.shape[0] == DEPTH
    B, seq, _ = x_prompt.shape
    DB, steps, _ = x_sample.shape
    n_meta = meta_tokens.shape[0]
    n_pages, page = page_table.shape[1], cache_mla_ckv.shape[2]
    past = n_pages * page
    T = n_meta + seq
    tp = -(-T // ROW_TILE) * ROW_TILE
    pad_front = tp - T
    tpb = tp // ROW_TILE
    n_s = DB * steps
    assert n_s % ROW_TILE == 0 and DB % HG_SAMPLE_BATCH == 0
    geo = dict(B=B, tp=tp, tpb=tpb, pad_front=pad_front, npt=B * tpb)
    Rp = B * tp
    R = Rp + n_s

    assert pad_front + n_meta == ROW_TILE and seq % ROW_TILE == 0
    x_head = jnp.concatenate([jnp.zeros((pad_front, D_MODEL), F32), meta_tokens.astype(F32)], axis=0)
    x_smp = x_sample.reshape(n_s, D_MODEL)

    pos_p = jnp.maximum(jnp.arange(tp) - pad_front, 0)
    pos = jnp.concatenate([jnp.tile(pos_p, B), jnp.tile(past + jnp.arange(steps), DB)]).astype(F32)
    half = MLA_ROPE // 2
    inv = ROPE_THETA ** (-jnp.arange(half, dtype=F32) / half)
    ang = pos[:, None] * inv[None, :]
    cos8 = jnp.tile(jnp.cos(ang), (1, LANES // half))
    sin8 = jnp.tile(jnp.sin(ang), (1, LANES // half))

    l = 0
    win = w_in[l]
    kr_end = MLA_Q_RANK + MLA_KV_RANK + MLA_ROPE
    w_in_p = jnp.concatenate([win[:, :kr_end], jnp.zeros((D_MODEL, LANES - MLA_ROPE), F32), win[:, kr_end:]],
                             axis=1).astype(BF16)
    assert w_in_p.shape[1] == D_IN_PACKED
    wq3 = w_uq[l].reshape(MLA_Q_RANK, MLA_HEADS, MLA_NOPE + MLA_ROPE)
    wqr = (jnp.concatenate([wq3[:, :, MLA_NOPE:MLA_NOPE + half].reshape(MLA_Q_RANK, MLA_HEADS * half),
                            wq3[:, :, MLA_NOPE + half:].reshape(MLA_Q_RANK, MLA_HEADS * half)], axis=1)
           * Q_SCALE).astype(BF16)
    src = jnp.arange(2 * LANES)
    hh, ii = (src % LANES) // half, src % half
    dst = hh * LANES + ii + jnp.where(src >= LANES, half, 0)
    perm = (dst[:, None] == jnp.arange(MLA_HEADS * LANES)[None, :]).astype(BF16)
    lb = jnp.cumsum(jax.nn.softmax(hg_lb_logits.astype(F32), axis=0), axis=0)[l].reshape(1, -1)

    wqlat, wo = _weight_prep(w_uq[l], w_uk[l], w_uv[l], w_br_mla[l])

    (qcat, kcat, ckv, kr, hq, hk, hv, lf, sg, sm, sh) = _in_projection(
        x_prompt, x_head, x_smp, w_in_p, q_norm[l].reshape(1, -1), kv_norm[l].reshape(1, -1), wqlat, wqr, perm,
        cos8, sin8, lb, geo)

    o_lat_p = _prompt_attention(qcat, kcat, geo)
    q_s = qcat[:, Rp:].reshape(MLA_HEADS, DB, steps, QCAT)
    knew_s = kcat[Rp:].reshape(DB, steps, 2 * LANES)
    cache_krt = jnp.swapaxes(cache_mla_krope[l], 1, 2)
    o_lat_s = _sample_attention(page_table, q_s, knew_s, cache_mla_ckv[l], cache_krt)
    o_lat_s = o_lat_s.reshape(n_s, MLA_HEADS * LANES)
    o_hg_p, s_prompt = _hgrn_prompt(hq, hk, hv, lf, geo)
    o_hg_s, s_sample = _hgrn_sample(hq, hk, hv, lf, state_hgrn[l], Rp, steps)

    rw_f = jnp.concatenate([router_w[l], jnp.zeros((D_MODEL, LANES - N_EXPERTS), F32)], axis=1)
    rw_hi = rw_f.astype(BF16)
    rw_p = jnp.stack([rw_hi, (rw_f - rw_hi.astype(F32)).astype(BF16)])
    rb_p = jnp.concatenate([router_bias[l].astype(F32), jnp.zeros((LANES - N_EXPERTS,), F32)]).reshape(1, -1)
    x1, x1p, idx, gate, rank, count = _layer_tail(
        x_prompt, x_head, x_smp, o_lat_p, o_lat_s, o_hg_p, o_hg_s, sg, sm, sh, wo, w_br_hg[l].astype(BF16),
        w_out[l].astype(BF16), hg_norm[l].reshape(1, -1), ln1_g[l].reshape(1, -1), ln1_b[l].reshape(1, -1),
        rw_p, rb_p, geo)

    nblk = R * TOP_K // MOE_BLOCK + N_EXPERTS
    dest_flat, pends, padded, block_expert, n_used = _routing_tables(
        idx[:, :TOP_K], rank[:, :TOP_K], count[0, :N_EXPERTS], nblk)
    xs = _moe_dispatch(pends, padded, dest_flat, x1p, nblk * MOE_BLOCK)
    ys = _moe_experts(block_expert, n_used, xs, w_exp_gate[l], w_exp_up[l], w_exp_down[l])
    yp, y_smp = _moe_combine(dest_flat, x1, gate, ys, w_sh_gate[l].astype(BF16), w_sh_up[l].astype(BF16),
                             w_sh_down[l].astype(BF16), ln2_g[l].reshape(1, -1), ln2_b[l].reshape(1, -1), geo, seq)

    ys_out = y_smp.reshape(DB, steps, D_MODEL)
    ckv_p = ckv[:Rp].reshape(B, tp, MLA_KV_RANK)[:, pad_front:][None]
    kr_p = kr[:Rp].reshape(B, tp, LANES)[:, pad_front:, :MLA_ROPE][None]
    ckv_s = ckv[Rp:].reshape(DB, steps, MLA_KV_RANK)[None]
    kr_s = kr[Rp:, :MLA_ROPE].reshape(DB, steps, MLA_ROPE)[None]
    return (yp, ys_out, ckv_p, kr_p, s_prompt[None], ckv_s, kr_s, s_sample[None])
```

```python
import functools

import jax
import jax.numpy as jnp
from jax import lax
from jax.experimental import pallas as pl
from jax.experimental.pallas import tpu as pltpu

F32 = jnp.float32
BF16 = jnp.bfloat16
U32 = jnp.uint32
I32 = jnp.int32

D_MODEL = 1024
MLA_HEADS = 8
MLA_Q_RANK = 256
MLA_KV_RANK = 128
MLA_NOPE = 64
MLA_ROPE = 32
MLA_V = 64
MLA_SCALE = (MLA_NOPE + MLA_ROPE) ** -0.5
LOG2E = 1.4426950408889634
Q_SCALE = MLA_SCALE * LOG2E
ROPE_THETA = 10000.0
HG_HEADS = 4
HG_DK = 128
HG_DV = 128
HG_CHUNK = 64
N_EXPERTS = 64
TOP_K = 8
D_EXPERT = 256
ROUTED_SCALE = 2.5
NORM_EPS = 1e-6
DEPTH = 1
ALPHA = (2.0 * DEPTH) ** 0.25

LANES = 128
ROW_TILE = 256
QCAT = 256
ATT_TK = 256
MOE_BLOCK = 512
PAGES_PER_CHUNK = 64
ONES_COL = LANES + MLA_ROPE
HG_SAMPLE_BATCH = 8
SATT_BATCH = 1
NEG_BIG = -1e30
VMEM_LIMIT = 52 * 1024 * 1024

C_CQ = (0, 256)
C_CKV = (256, 384)
C_KR = (384, 512)
C_HQ = (512, 1024)
C_HF = (1024, 1536)
C_HI = (1536, 2048)
C_HG = (2048, 2560)
C_GM = (2560, 3584)
C_GH = (3584, 4608)
D_IN_PACKED = 4608


def _cparams(*sem):
    return pltpu.CompilerParams(dimension_semantics=sem, vmem_limit_bytes=VMEM_LIMIT)


def _dot(a, b):
    return jnp.dot(a, b, preferred_element_type=F32)


def _dot_nt(a, b):
    return lax.dot_general(a, b, (((1,), (1,)), ((), ())), preferred_element_type=F32)


def _dot_tn(a, b):
    return lax.dot_general(a, b, (((0,), (0,)), ((), ())), preferred_element_type=F32)


def _pack_bf16_pairs(x):
    w = x.shape[1] // 2
    bits = pltpu.bitcast(x.astype(BF16).astype(F32), U32)
    return bits[:, w:] | (bits[:, :w] >> 16)


def _unpack_bf16_pairs(p):
    lo = pltpu.bitcast(p << 16, F32)
    hi = pltpu.bitcast(p & jnp.uint32(0xFFFF0000), F32)
    return jnp.concatenate([lo, hi], axis=1)


def _x_specs(geo, seq_tiles):
    B, tpb, npt = geo["B"], geo["tpb"], geo["npt"]

    def xp_map(i):
        return (jnp.minimum(i // tpb, B - 1), jnp.where(i < npt, jnp.maximum(i % tpb - 1, 0), seq_tiles - 1), 0)

    return [pl.BlockSpec((1, ROW_TILE, D_MODEL), xp_map),
            pl.BlockSpec((ROW_TILE, D_MODEL), lambda i: (0, 0)),
            pl.BlockSpec((ROW_TILE, D_MODEL), lambda i: (jnp.maximum(i - npt, 0), 0))]


def _select_x(i, xp_ref, xh_ref, xs_ref, tiles_per_batch, n_prompt_tiles):
    is_head = (i < n_prompt_tiles) & (i % tiles_per_batch == 0)
    return jnp.where(i >= n_prompt_tiles, xs_ref[...], jnp.where(is_head, xh_ref[...], xp_ref[0]))


def _split_specs(width, npt):
    return [pl.BlockSpec((ROW_TILE, width), lambda i: (jnp.minimum(i, npt - 1), 0)),
            pl.BlockSpec((ROW_TILE, width), lambda i: (jnp.maximum(i - npt, 0), 0))]


def _wprep_kernel(wq_nope_ref, wuk_ref, wuv_ref, wbr_ref, wqlat_ref, wo_ref):
    hp = lax.Precision.HIGHEST
    for h in range(MLA_HEADS):
        a = wq_nope_ref[h]
        b = wuk_ref[h]
        ql = lax.dot_general(a, b, (((1,), (1,)), ((), ())), precision=hp, preferred_element_type=F32)
        wqlat_ref[:, h * MLA_KV_RANK:(h + 1) * MLA_KV_RANK] = (ql * Q_SCALE).astype(BF16)
        c = wuv_ref[h]
        d = wbr_ref[h]
        wo_ref[h * MLA_KV_RANK:(h + 1) * MLA_KV_RANK, :] = jnp.dot(
            c, d, precision=hp, preferred_element_type=F32).astype(BF16)


def _weight_prep(w_uq, w_uk, w_uv, w_br_mla):
    wq3 = w_uq.reshape(MLA_Q_RANK, MLA_HEADS, MLA_NOPE + MLA_ROPE)
    wq_nope = jnp.transpose(wq3[:, :, :MLA_NOPE], (1, 0, 2))
    wuk = jnp.transpose(w_uk, (1, 0, 2))
    wuv = jnp.transpose(w_uv, (1, 0, 2))
    wbr = w_br_mla.reshape(MLA_HEADS, MLA_V, D_MODEL)
    return pl.pallas_call(
        _wprep_kernel,
        out_shape=(jax.ShapeDtypeStruct((MLA_Q_RANK, MLA_HEADS * MLA_KV_RANK), BF16),
                   jax.ShapeDtypeStruct((MLA_HEADS * MLA_KV_RANK, D_MODEL), BF16)),
        compiler_params=pltpu.CompilerParams(vmem_limit_bytes=VMEM_LIMIT),
        name="weight_prep",
    )(wq_nope, wuk, wuv, wbr)


def _inproj_kernel(xp_ref, xh_ref, xs_ref, w_ref, qn_ref, kvn_ref, wqlat_ref, wqr_ref, perm_ref, cos_ref, sin_ref,
                   lb_ref, qcat_ref, kcat_ref, ckv_ref, kr_ref, hq_ref, hk_ref, hv_ref, lf_ref, sg_ref, sm_ref,
                   sh_ref, *, pad_front, tiles_per_batch, n_prompt_tiles):
    i = pl.program_id(0)
    tm = xh_ref.shape[0]
    xb = _select_x(i, xp_ref, xh_ref, xs_ref, tiles_per_batch, n_prompt_tiles).astype(BF16)

    def proj(c):
        return _dot(xb, w_ref[:, c[0]:c[1]])

    cos8 = cos_ref[...]
    sin8 = sin_ref[...]

    cq = proj(C_CQ)
    cqn = cq * lax.rsqrt(jnp.mean(cq * cq, axis=-1, keepdims=True) + NORM_EPS) * qn_ref[...]
    cqb = cqn.astype(BF16)
    qlat = _dot(cqb, wqlat_ref[...])
    qr = _dot(cqb, wqr_ref[...])
    x1, x2 = qr[:, :LANES], qr[:, LANES:]
    qrot = jnp.concatenate([x1 * cos8 - x2 * sin8, x2 * cos8 + x1 * sin8], axis=1).astype(BF16)
    qrh = _dot(qrot, perm_ref[...])
    for h in range(MLA_HEADS):
        qcat_ref[h, :, :LANES] = qlat[:, h * LANES:(h + 1) * LANES].astype(BF16)
        qcat_ref[h, :, LANES:] = qrh[:, h * LANES:(h + 1) * LANES].astype(BF16)

    kv = proj(C_CKV)
    ckv = kv * lax.rsqrt(jnp.mean(kv * kv, axis=-1, keepdims=True) + NORM_EPS) * kvn_ref[...]
    ckv_ref[...] = ckv
    kcat_ref[:, :LANES] = ckv.astype(BF16)
    krr = proj(C_KR)
    lane = lax.broadcasted_iota(I32, (tm, LANES), 1)
    half = MLA_ROPE // 2
    rot = jnp.where(lane < half, -pltpu.roll(krr, LANES - half, 1), pltpu.roll(krr, half, 1))
    kr = jnp.where(lane < MLA_ROPE, krr * cos8 + rot * sin8, 0.0)
    kr_ref[...] = kr
    kcat_ref[:, LANES:] = jnp.where(lane == ONES_COL - LANES, 1.0, kr).astype(BF16)

    row = lax.broadcasted_iota(I32, (tm, 1), 0)
    is_pad = (i < n_prompt_tiles) & (i % tiles_per_batch == 0) & (row < pad_front)
    keep = jnp.where(is_pad, 0.0, 1.0)
    hq_ref[...] = proj(C_HQ).astype(BF16)
    lb = lb_ref[...]
    f = lb + (1.0 - lb) * jax.nn.sigmoid(proj(C_HF))
    lf_ref[...] = jnp.log(f) * keep
    hk_ref[...] = ((1.0 - f) * keep).astype(BF16)
    hv_ref[...] = proj(C_HI).astype(BF16)

    g = proj(C_HG)
    sg_ref[...] = (g * jax.nn.sigmoid(g)).astype(BF16)
    sm_ref[...] = jax.nn.sigmoid(proj(C_GM)).astype(BF16)
    sh_ref[...] = jax.nn.sigmoid(proj(C_GH)).astype(BF16)


def _in_projection(x_prompt, x_head, x_smp, w_in_p, q_norm, kv_norm, wqlat, wqr, perm, cos8, sin8, lb, geo):
    R = cos8.shape[0]
    nt = R // ROW_TILE
    hgw = HG_HEADS * HG_DK
    x_specs = _x_specs(geo, x_prompt.shape[1] // ROW_TILE)

    def rows(w):
        return pl.BlockSpec((ROW_TILE, w), lambda i: (i, 0))

    def full(a):
        return pl.BlockSpec(a.shape, lambda i: (0,) * a.ndim)

    out_widths = [(2 * LANES, BF16), (LANES, F32), (LANES, F32),
                  (hgw, BF16), (hgw, BF16), (hgw, BF16), (hgw, F32), (hgw, BF16),
                  (D_MODEL, BF16), (D_MODEL, BF16)]
    qcat_spec = pl.BlockSpec((MLA_HEADS, ROW_TILE, QCAT), lambda i: (0, i, 0))
    return pl.pallas_call(
        functools.partial(_inproj_kernel, pad_front=geo["pad_front"], tiles_per_batch=geo["tpb"],
                          n_prompt_tiles=geo["npt"]),
        grid=(nt,),
        in_specs=x_specs + [full(w_in_p), full(q_norm), full(kv_norm), full(wqlat), full(wqr), full(perm),
                            rows(LANES), rows(LANES), full(lb)],
        out_specs=[qcat_spec] + [rows(w) for w, _ in out_widths],
        out_shape=[jax.ShapeDtypeStruct((MLA_HEADS, R, QCAT), BF16)]
        + [jax.ShapeDtypeStruct((R, w), dt) for w, dt in out_widths],
        compiler_params=_cparams("arbitrary"),
        name="in_projection",
    )(x_prompt, x_head, x_smp, w_in_p, q_norm, kv_norm, wqlat, wqr, perm, cos8, sin8, lb)


def _softmax_step(s, v_b, m_ref, l_ref, acc_ref):
    n = s.shape[1] // LANES
    m_prev = m_ref[...]
    m_next = jnp.maximum(m_prev, jnp.max(s, axis=1, keepdims=True))
    p = jnp.concatenate([jnp.exp2(s[:, j * LANES:(j + 1) * LANES] - m_next) for j in range(n)], axis=1)
    alpha = jnp.exp2(m_prev - m_next)
    pv = _dot(p.astype(BF16), v_b)
    if v_b.shape[1] == LANES:
        l_ref[...] = alpha * l_ref[...] + jnp.sum(p, axis=1, keepdims=True)
    else:
        l_ref[...] = alpha * l_ref[...] + pv[:, LANES:]
    acc_ref[...] = alpha * acc_ref[...] + pv[:, :LANES]
    m_ref[...] = m_next


def _pattn_kernel(q_ref, k_ref, o_ref, m_ref, l_ref, acc_ref, *, pad_front):
    qi = pl.program_id(1)
    nh, tq, _ = q_ref.shape
    tk = ATT_TK
    rows = nh * tq

    q = q_ref[...].reshape(rows, QCAT)
    m_ref[...] = jnp.full(m_ref.shape, -jnp.inf, F32)
    l_ref[...] = jnp.zeros(l_ref.shape, F32)
    acc_ref[...] = jnp.zeros(acc_ref.shape, F32)

    def step(kb, masked):
        kblk = k_ref[pl.ds(pl.multiple_of(kb * tk, tk), tk), :]
        s = _dot_nt(q, kblk)
        if masked:
            qrow = qi * tq + lax.broadcasted_iota(I32, (rows, tk), 0) % tq
            krow = kb * tk + lax.broadcasted_iota(I32, (rows, tk), 1)
            s = jnp.where((krow <= qrow) & (krow >= pad_front), s, NEG_BIG)
        _softmax_step(s, kblk, m_ref, l_ref, acc_ref)

    step(0, True)

    def body(kb, c):
        step(kb, False)
        return c

    lax.fori_loop(1, qi, body, 0)

    @pl.when(qi > 0)
    def _():
        step(qi, True)

    row_sum = l_ref[:, ONES_COL - LANES:ONES_COL - LANES + 1]
    o = (acc_ref[...] / row_sum).astype(BF16)
    for h in range(nh):
        o_ref[:, h * LANES:(h + 1) * LANES] = o[h * tq:(h + 1) * tq]


def _prompt_attention(qcat, kcat, geo):
    B, tpb, tp = geo["B"], geo["tpb"], geo["tp"]
    rows = MLA_HEADS * ROW_TILE
    return pl.pallas_call(
        functools.partial(_pattn_kernel, pad_front=geo["pad_front"]),
        grid=(B, tpb),
        in_specs=[pl.BlockSpec((MLA_HEADS, ROW_TILE, QCAT), lambda b, i: (0, b * tpb + i, 0)),
                  pl.BlockSpec((tp, 2 * LANES), lambda b, i: (b, 0))],
        out_specs=pl.BlockSpec((ROW_TILE, MLA_HEADS * LANES), lambda b, i: (b * tpb + i, 0)),
        out_shape=jax.ShapeDtypeStruct((B * tp, MLA_HEADS * LANES), BF16),
        scratch_shapes=[pltpu.VMEM((rows, LANES), F32)] * 3,
        compiler_params=_cparams("arbitrary", "arbitrary"),
        name="prompt_attention",
    )(qcat, kcat)


def _sattn_kernel(pt_ref, q_ref, knew_ref, ckv_hbm, kr_hbm, o_ref,
                  ckv_buf, kr_buf, sem, m_ref, l_ref, acc_ref, *, n_chunks, page):
    g = pl.program_id(0)
    ng = pl.num_programs(0)
    ch = PAGES_PER_CHUNK
    nh, nbt, S = q_ref.shape[0], q_ref.shape[1], q_ref.shape[2]
    rows = nh * S

    def chunk_copies(gg, c, slot):
        cps = []
        for bb in range(nbt):
            for j in range(ch):
                pg = pt_ref[gg * nbt + bb, c * ch + j]
                cps.append(pltpu.make_async_copy(ckv_hbm.at[pg], ckv_buf.at[slot, bb, pl.ds(j * page, page), :],
                                                 sem.at[0, slot]))
                cps.append(pltpu.make_async_copy(kr_hbm.at[pg], kr_buf.at[slot, bb, :, pl.ds(j * page, page)],
                                                 sem.at[1, slot]))
        return cps

    @pl.when(g == 0)
    def _():
        for cp in chunk_copies(0, 0, 0):
            cp.start()

    qlat, qrope = [], []
    for bb in range(nbt):
        q = q_ref[:, bb].reshape(rows, QCAT)
        qlat.append(q[:, :LANES])
        qrope.append(q[:, LANES:LANES + MLA_ROPE])
        knew = knew_ref[bb]
        s_new = _dot_nt(q, knew)
        qtok = lax.broadcasted_iota(I32, (rows, S), 0) % S
        ktok = lax.broadcasted_iota(I32, (rows, S), 1)
        s_new = jnp.where(ktok <= qtok, s_new, NEG_BIG)
        m0 = jnp.max(s_new, axis=1, keepdims=True)
        p0 = jnp.exp2(s_new - m0)
        m_ref[bb] = jnp.broadcast_to(m0, (rows, LANES))
        l_ref[bb] = jnp.broadcast_to(jnp.sum(p0, axis=1, keepdims=True), (rows, LANES))
        acc_ref[bb] = _dot(p0.astype(BF16), knew[:, :LANES])

    for c in range(n_chunks):
        slot = c % 2 if n_chunks % 2 == 0 else (g * n_chunks + c) % 2
        if c + 1 < n_chunks:
            for cp in chunk_copies(g, c + 1, 1 - slot):
                cp.start()
        else:
            @pl.when(g + 1 < ng)
            def _():
                for cp in chunk_copies(g + 1, 0, 1 - slot):
                    cp.start()
        for cp in chunk_copies(g, c, slot):
            cp.wait()
        for bb in range(nbt):
            ckv_b = ckv_buf[slot, bb].astype(BF16)
            krt_b = kr_buf[slot, bb].astype(BF16)
            s = _dot_nt(qlat[bb], ckv_b) + _dot(qrope[bb], krt_b)
            _softmax_step(s, ckv_b, m_ref.at[bb], l_ref.at[bb], acc_ref.at[bb])

    for bb in range(nbt):
        o = (acc_ref[bb] / l_ref[bb]).astype(BF16)
        for h in range(nh):
            o_ref[bb, :, h * LANES:(h + 1) * LANES] = o[h * S:(h + 1) * S]


def _sample_attention(page_table, q_s, knew_s, cache_ckv, cache_krt):
    nh, DB, S, _ = q_s.shape
    rows = nh * S
    n_pages = page_table.shape[1]
    page = cache_ckv.shape[1]
    assert n_pages % PAGES_PER_CHUNK == 0
    n_chunks = n_pages // PAGES_PER_CHUNK
    ck = PAGES_PER_CHUNK * page
    nbt = SATT_BATCH
    assert DB % nbt == 0
    grid_spec = pltpu.PrefetchScalarGridSpec(
        num_scalar_prefetch=1,
        grid=(DB // nbt,),
        in_specs=[pl.BlockSpec((nh, nbt, S, QCAT), lambda g, pt: (0, g, 0, 0)),
                  pl.BlockSpec((nbt, S, 2 * LANES), lambda g, pt: (g, 0, 0)),
                  pl.BlockSpec(memory_space=pl.ANY),
                  pl.BlockSpec(memory_space=pl.ANY)],
        out_specs=pl.BlockSpec((nbt, S, nh * LANES), lambda g, pt: (g, 0, 0)),
        scratch_shapes=[pltpu.VMEM((2, nbt, ck, MLA_KV_RANK), F32),
                        pltpu.VMEM((2, nbt, MLA_ROPE, ck), F32),
                        pltpu.SemaphoreType.DMA((2, 2)),
                        pltpu.VMEM((nbt, rows, LANES), F32),
                        pltpu.VMEM((nbt, rows, LANES), F32),
                        pltpu.VMEM((nbt, rows, LANES), F32)])
    return pl.pallas_call(
        functools.partial(_sattn_kernel, n_chunks=n_chunks, page=page),
        grid_spec=grid_spec,
        out_shape=jax.ShapeDtypeStruct((DB, S, nh * LANES), BF16),
        compiler_params=_cparams("arbitrary"),
        name="sample_attention",
    )(page_table, q_s, knew_s, cache_ckv, cache_krt)


def _split3(x):
    hi = x.astype(BF16)
    r1 = x - hi.astype(F32)
    mid = r1.astype(BF16)
    lo = (r1 - mid.astype(F32)).astype(BF16)
    return hi, mid, lo


def _hgrn_chunk(q, k, v, lf, S0, tri, group):
    C = q.shape[0]
    hi, mid, lo = _split3(lf)
    cum = _dot(tri, hi) + _dot(tri, mid) + _dot(tri, lo)
    vb = v.astype(BF16)
    rowi = lax.broadcasted_iota(I32, (C, C), 0)
    coli = lax.broadcasted_iota(I32, (C, C), 1)
    attn = jnp.zeros((C, C), F32)

    bd = min(16, group)
    hs = group // 2
    rid = lax.broadcasted_iota(I32, (C, 1), 0)
    while hs >= bd:
        npair = C // (2 * hs)
        ref = jnp.concatenate(
            [jnp.broadcast_to(cum[(2 * j + 1) * hs - 1:(2 * j + 1) * hs, :], (2 * hs, cum.shape[1]))
             for j in range(npair)], axis=0)
        odd = ((rid // hs) % 2) == 1
        e = jnp.exp(jnp.where(odd, cum - ref, ref - cum))
        qs = jnp.where(odd, q * e, 0.0).astype(BF16)
        ks = jnp.where(odd, 0.0, k * e).astype(BF16)
        a = _dot_nt(qs, ks)
        attn = attn + jnp.where((rowi // (2 * hs)) == (coli // (2 * hs)), a, 0.0)
        hs //= 2

    nblk = C // bd
    k3 = k.reshape(nblk, bd, k.shape[1])
    c3 = cum.reshape(nblk, bd, cum.shape[1])
    tl = rid % bd
    blk_base = (rowi // bd) * bd
    for sl in range(bd):
        ks_b = jnp.broadcast_to(k3[:, sl:sl + 1, :], k3.shape).reshape(C, k.shape[1])
        cs_b = jnp.broadcast_to(c3[:, sl:sl + 1, :], c3.shape).reshape(C, k.shape[1])
        e = jnp.exp(jnp.where(tl >= sl, cum - cs_b, NEG_BIG))
        col = jnp.sum(q * ks_b * e, axis=1, keepdims=True)
        attn = jnp.where(coli == blk_base + sl, col, attn)
    o_intra = _dot(attn.astype(BF16), vb)
    return o_intra, cum


def _tri(C, group):
    r = lax.broadcasted_iota(I32, (C, C), 0)
    c = lax.broadcasted_iota(I32, (C, C), 1)
    return jnp.where((c <= r) & (r // group == c // group), 1.0, 0.0).astype(BF16)


def _state_update(q, k, v, cum, S, o_intra):
    C = q.shape[0]
    last = cum[C - 1:C, :]
    o = o_intra + _dot((q * jnp.exp(cum)).astype(BF16), S.astype(BF16))
    kst = (k * jnp.exp(last - cum)).astype(BF16)
    dfull = jnp.transpose(jnp.broadcast_to(jnp.exp(last), (S.shape[1], S.shape[0])))
    S_new = dfull * S + _dot_tn(kst, v.astype(BF16))
    return o, S_new


def _hgrn_prompt_kernel(q_ref, k_ref, v_ref, lf_ref, o_ref, s_out_ref, s_ref):
    t = pl.program_id(1)

    @pl.when(t == 0)
    def _():
        s_ref[...] = jnp.zeros(s_ref.shape, F32)

    C = HG_CHUNK
    tri = _tri(C, C)
    states = [s_ref[h] for h in range(HG_HEADS)]
    for c in range(q_ref.shape[0] // C):
        sl = slice(c * C, (c + 1) * C)
        for h in range(HG_HEADS):
            hl = slice(h * HG_DK, (h + 1) * HG_DK)
            q = q_ref[sl, hl].astype(F32)
            k = k_ref[sl, hl].astype(F32)
            v = v_ref[sl, hl].astype(F32)
            lf = lf_ref[sl, hl]
            o_intra, cum = _hgrn_chunk(q, k, v, lf, None, tri, C)
            o, states[h] = _state_update(q, k, v, cum, states[h], o_intra)
            o_ref[sl, hl] = o.astype(BF16)
    for h in range(HG_HEADS):
        s_ref[h] = states[h]

    @pl.when(t == pl.num_programs(1) - 1)
    def _():
        s_out_ref[0] = s_ref[...]


def _hgrn_prompt(hq, hk, hv, lf, geo):
    B, tpb, tp = geo["B"], geo["tpb"], geo["tp"]
    hgw = HG_HEADS * HG_DK

    def blk():
        return pl.BlockSpec((ROW_TILE, hgw), lambda b, t: (b * tpb + t, 0))

    return pl.pallas_call(
        _hgrn_prompt_kernel,
        grid=(B, tpb),
        in_specs=[blk(), blk(), blk(), blk()],
        out_specs=[blk(), pl.BlockSpec((1, HG_HEADS, HG_DK, HG_DV), lambda b, t: (b, 0, 0, 0))],
        out_shape=[jax.ShapeDtypeStruct((B * tp, hgw), BF16),
                   jax.ShapeDtypeStruct((B, HG_HEADS, HG_DK, HG_DV), F32)],
        scratch_shapes=[pltpu.VMEM((HG_HEADS, HG_DK, HG_DV), F32)],
        compiler_params=_cparams("arbitrary", "arbitrary"),
        name="hgrn_prompt",
    )(hq, hk, hv, lf)


def _hgrn_sample_kernel(q_ref, k_ref, v_ref, lf_ref, s_in_ref, o_ref, s_out_ref, *, steps):
    nb = s_in_ref.shape[0]
    C = nb * steps
    tri = _tri(C, steps)
    for h in range(HG_HEADS):
        sl = slice(h * HG_DK, (h + 1) * HG_DK)
        q = q_ref[:, sl].astype(F32)
        k = k_ref[:, sl].astype(F32)
        v = v_ref[:, sl].astype(F32)
        lf = lf_ref[:, sl]
        o_intra, cum = _hgrn_chunk(q, k, v, lf, None, tri, steps)
        for b in range(nb):
            r = slice(b * steps, (b + 1) * steps)
            o, S_new = _state_update(q[r], k[r], v[r], cum[r], s_in_ref[b, h], o_intra[r])
            o_ref[r, sl] = o.astype(BF16)
            s_out_ref[b, h] = S_new


def _hgrn_sample(hq, hk, hv, lf, state, row0, steps):
    DB = state.shape[0]
    nb = HG_SAMPLE_BATCH
    rows = nb * steps
    hgw = HG_HEADS * HG_DK
    blk0 = row0 // rows

    def tok():
        return pl.BlockSpec((rows, hgw), lambda i: (blk0 + i, 0))

    st = pl.BlockSpec((nb, HG_HEADS, HG_DK, HG_DV), lambda i: (i, 0, 0, 0))
    return pl.pallas_call(
        functools.partial(_hgrn_sample_kernel, steps=steps),
        grid=(DB // nb,),
        in_specs=[tok(), tok(), tok(), tok(), st],
        out_specs=[pl.BlockSpec((rows, hgw), lambda i: (i, 0)), st],
        out_shape=[jax.ShapeDtypeStruct((DB * steps, hgw), BF16),
                   jax.ShapeDtypeStruct(state.shape, F32)],
        compiler_params=_cparams("arbitrary"),
        name="hgrn_sample",
    )(hq, hk, hv, lf, state)


def _layer_norm(x, g, b):
    mu = jnp.mean(x, axis=-1, keepdims=True)
    xc = x - mu
    var = jnp.mean(xc * xc, axis=-1, keepdims=True)
    return xc * lax.rsqrt(var + NORM_EPS) * g + b


def _tail_kernel(xp_ref, xh_ref, xs_ref, olp_ref, ols_ref, ohp_ref, ohs_ref, sg_ref, sm_ref, sh_ref, wo_ref, wbh_ref,
                 wout_ref, hgn_ref, g1_ref, b1_ref, rw_ref, rb_ref, x1_ref, x1p_ref, idx_ref, gate_ref, rank_ref,
                 count_ref, cnt_ref, *, tiles_per_batch, n_prompt_tiles):
    i = pl.program_id(0)
    tm = xh_ref.shape[0]
    x_in = _select_x(i, xp_ref, xh_ref, xs_ref, tiles_per_batch, n_prompt_tiles)
    is_sample = i >= n_prompt_tiles
    mla = _dot(jnp.where(is_sample, ols_ref[...], olp_ref[...]), wo_ref[...])
    oh = jnp.where(is_sample, ohs_ref[...], ohp_ref[...]).astype(F32)
    parts = []
    for h in range(HG_HEADS):
        y = oh[:, h * HG_DV:(h + 1) * HG_DV]
        parts.append(y * lax.rsqrt(jnp.mean(y * y, axis=-1, keepdims=True) + NORM_EPS) * hgn_ref[...])
    hg = (jnp.concatenate(parts, axis=1) * sg_ref[...].astype(F32)).astype(BF16)
    merged = sm_ref[...].astype(F32) * mla + sh_ref[...].astype(F32) * _dot(hg, wbh_ref[...])
    x1 = _layer_norm(ALPHA * x_in + _dot(merged.astype(BF16), wout_ref[...]), g1_ref[...], b1_ref[...])
    x1_ref[...] = x1
    x1p_ref[...] = _pack_bf16_pairs(x1)

    x1_hi = x1.astype(BF16)
    x1_lo = (x1 - x1_hi.astype(F32)).astype(BF16)
    logits = _dot(x1_hi, rw_ref[0]) + _dot(x1_lo, rw_ref[0]) + _dot(x1_hi, rw_ref[1])
    scores = jax.nn.sigmoid(logits)
    lane = lax.broadcasted_iota(I32, (tm, LANES), 1).astype(F32)
    remaining = jnp.where(lane < N_EXPERTS, scores + rb_ref[...], -jnp.inf)
    idx_out = jnp.zeros((tm, LANES), F32)
    gate_out = jnp.zeros((tm, LANES), F32)
    hits = []
    for kk in range(TOP_K):
        mx = jnp.max(remaining, axis=1, keepdims=True)
        pick = jnp.min(jnp.where(remaining == mx, lane, float(LANES)), axis=1, keepdims=True)
        hit = lane == pick
        hits.append(hit)
        gval = jnp.sum(jnp.where(hit, scores, 0.0), axis=1, keepdims=True)
        idx_out = jnp.where(lane == kk, pick, idx_out)
        gate_out = jnp.where(lane == kk, gval, gate_out)
        remaining = jnp.where(hit, -jnp.inf, remaining)
    gate_out = gate_out / jnp.sum(gate_out, axis=1, keepdims=True) * ROUTED_SCALE
    idx_ref[...] = idx_out.astype(I32)
    gate_ref[...] = gate_out

    @pl.when(pl.program_id(0) == 0)
    def _():
        cnt_ref[...] = jnp.zeros(cnt_ref.shape, F32)

    sel = jnp.where(remaining == -jnp.inf, 1.0, 0.0) * jnp.where(lane < N_EXPERTS, 1.0, 0.0)
    r_i = lax.broadcasted_iota(I32, (tm, tm), 0)
    c_i = lax.broadcasted_iota(I32, (tm, tm), 1)
    before = _dot(jnp.where(c_i < r_i, 1.0, 0.0).astype(BF16), sel.astype(BF16)) + cnt_ref[0:1, :]
    rank_out = jnp.zeros((tm, LANES), F32)
    for kk in range(TOP_K):
        rk = jnp.sum(jnp.where(hits[kk], before, 0.0), axis=1, keepdims=True)
        rank_out = jnp.where(lane == kk, rk, rank_out)
    rank_ref[...] = rank_out.astype(I32)
    total = cnt_ref[0:1, :] + jnp.sum(sel, axis=0, keepdims=True)
    cnt_ref[...] = jnp.broadcast_to(total, cnt_ref.shape)
    count_ref[...] = jnp.broadcast_to(total, count_ref.shape).astype(I32)


def _layer_tail(x_prompt, x_head, x_smp, o_lat_p, o_lat_s, o_hg_p, o_hg_s, sg, sm, sh, wo, w_br_hg, w_out, hg_norm,
                ln1_g, ln1_b, rw_p, rb_p, geo):
    R = sg.shape[0]
    npt = geo["npt"]

    def rows(w):
        return pl.BlockSpec((ROW_TILE, w), lambda i: (i, 0))

    def full(a):
        return pl.BlockSpec(a.shape, lambda i: (0,) * a.ndim)

    hgw = HG_HEADS * HG_DV
    return pl.pallas_call(
        functools.partial(_tail_kernel, tiles_per_batch=geo["tpb"], n_prompt_tiles=npt),
        grid=(R // ROW_TILE,),
        in_specs=_x_specs(geo, x_prompt.shape[1] // ROW_TILE)
        + _split_specs(MLA_HEADS * LANES, npt) + _split_specs(hgw, npt)
        + [rows(hgw), rows(D_MODEL), rows(D_MODEL),
           full(wo), full(w_br_hg), full(w_out), full(hg_norm), full(ln1_g), full(ln1_b),
           full(rw_p), full(rb_p)],
        out_specs=[rows(D_MODEL), rows(D_MODEL // 2), rows(LANES), rows(LANES), rows(LANES),
                   pl.BlockSpec((8, LANES), lambda i: (0, 0))],
        out_shape=[jax.ShapeDtypeStruct((R, D_MODEL), F32), jax.ShapeDtypeStruct((R, D_MODEL // 2), U32),
                   jax.ShapeDtypeStruct((R, LANES), I32), jax.ShapeDtypeStruct((R, LANES), F32),
                   jax.ShapeDtypeStruct((R, LANES), I32), jax.ShapeDtypeStruct((8, LANES), I32)],
        scratch_shapes=[pltpu.VMEM((8, LANES), F32)],
        compiler_params=_cparams("arbitrary"),
        name="layer_tail",
    )(x_prompt, x_head, x_smp, o_lat_p, o_lat_s, o_hg_p, o_hg_s, sg, sm, sh, wo, w_br_hg, w_out, hg_norm,
      ln1_g, ln1_b, rw_p, rb_p)


def _dispatch_kernel(pends_ref, padded_ref, dest_ref, x_ref, xs_hbm, zero_buf, stage, sem, zsem):
    i = pl.program_id(0)
    tm = x_ref.shape[0]

    @pl.when(i == 0)
    def _():
        zero_buf[...] = jnp.zeros(zero_buf.shape, U32)

        def zcopy(e):
            start = pl.multiple_of(pends_ref[e] - MOE_BLOCK, MOE_BLOCK)
            return pltpu.make_async_copy(zero_buf, xs_hbm.at[pl.ds(start, MOE_BLOCK), :], zsem)

        n_slots = xs_hbm.shape[0]

        def tcopy(j):
            start = pl.multiple_of(pends_ref[N_EXPERTS - 1] + j * MOE_BLOCK, MOE_BLOCK)
            return pltpu.make_async_copy(zero_buf, xs_hbm.at[pl.ds(start, MOE_BLOCK), :], zsem)

        def tail_live(j):
            return pends_ref[N_EXPERTS - 1] + (j + 1) * MOE_BLOCK <= n_slots

        for e in range(N_EXPERTS):
            @pl.when(padded_ref[e] > 0)
            def _():
                zcopy(e).start()

            @pl.when(tail_live(e))
            def _():
                tcopy(e).start()
        for e in range(N_EXPERTS):
            @pl.when(padded_ref[e] > 0)
            def _():
                zcopy(e).wait()

            @pl.when(tail_live(e))
            def _():
                tcopy(e).wait()

    nt = pl.num_programs(0)
    slot = i % 2

    def wait_all(s):
        for kk in range(TOP_K):
            pltpu.make_async_copy(stage.at[s], xs_hbm.at[pl.ds(0, tm), :], sem.at[s]).wait()

    @pl.when(i >= 2)
    def _():
        wait_all(slot)

    stage[slot] = x_ref[...]
    for r in range(tm):
        for kk in range(TOP_K):
            d = dest_ref[r * TOP_K + kk]
            pltpu.make_async_copy(stage.at[slot, pl.ds(r, 1), :], xs_hbm.at[pl.ds(d, 1), :],
                                  sem.at[slot]).start(priority=kk % 2)

    @pl.when(i == nt - 1)
    def _():
        wait_all(slot)

    @pl.when((i == nt - 1) & (i >= 1))
    def _():
        wait_all(1 - slot)


def _moe_dispatch(pends, padded, dest_flat, x1p, n_slots):
    R, W = x1p.shape
    grid_spec = pltpu.PrefetchScalarGridSpec(
        num_scalar_prefetch=2,
        grid=(R // ROW_TILE,),
        in_specs=[pl.BlockSpec((ROW_TILE * TOP_K,), lambda i, pe, pa: (i,), memory_space=pltpu.SMEM),
                  pl.BlockSpec((ROW_TILE, W), lambda i, pe, pa: (i, 0))],
        out_specs=pl.BlockSpec(memory_space=pl.ANY),
        scratch_shapes=[pltpu.VMEM((MOE_BLOCK, W), U32), pltpu.VMEM((2, ROW_TILE, W), U32),
                        pltpu.SemaphoreType.DMA((2,)), pltpu.SemaphoreType.DMA])
    return pl.pallas_call(
        _dispatch_kernel,
        grid_spec=grid_spec,
        out_shape=jax.ShapeDtypeStruct((n_slots, W), U32),
        compiler_params=_cparams("arbitrary"),
        name="moe_dispatch",
    )(pends, padded, dest_flat, x1p)


def _expert_kernel(be_ref, nused_ref, xs_ref, wg_ref, wu_ref, wd_ref, ys_ref, wgb_ref, wub_ref, wdb_ref):
    i = pl.program_id(0)

    @pl.when((i == 0) | (be_ref[i] != be_ref[jnp.maximum(i - 1, 0)]))
    def _():
        wgb_ref[...] = wg_ref[0].astype(BF16)
        wub_ref[...] = wu_ref[0].astype(BF16)
        wdb_ref[...] = wd_ref[0].astype(BF16)

    @pl.when(i < nused_ref[0])
    def _():
        xb = _unpack_bf16_pairs(xs_ref[...]).astype(BF16)
        g = _dot(xb, wgb_ref[...])
        u = _dot(xb, wub_ref[...])
        hmid = (g * jax.nn.sigmoid(g) * u).astype(BF16)
        ys_ref[...] = _pack_bf16_pairs(_dot(hmid, wdb_ref[...]))

    @pl.when(i >= nused_ref[0])
    def _():
        ys_ref[...] = jnp.zeros(ys_ref.shape, U32)


def _moe_experts(block_expert, n_used, xs, wg, wu, wd):
    n_slots, W = xs.shape
    nblk = n_slots // MOE_BLOCK

    def xmap(i, be, nu):
        return (jnp.minimum(i, nu[0] - 1), 0)

    def wmap(i, be, nu):
        return (be[i], 0, 0)

    grid_spec = pltpu.PrefetchScalarGridSpec(
        num_scalar_prefetch=2,
        grid=(nblk,),
        in_specs=[pl.BlockSpec((MOE_BLOCK, W), xmap),
                  pl.BlockSpec((1, D_MODEL, D_EXPERT), wmap),
                  pl.BlockSpec((1, D_MODEL, D_EXPERT), wmap),
                  pl.BlockSpec((1, D_EXPERT, D_MODEL), wmap)],
        out_specs=pl.BlockSpec((MOE_BLOCK, W), lambda i, be, nu: (i, 0)),
        scratch_shapes=[pltpu.VMEM((D_MODEL, D_EXPERT), BF16), pltpu.VMEM((D_MODEL, D_EXPERT), BF16),
                        pltpu.VMEM((D_EXPERT, D_MODEL), BF16)])
    return pl.pallas_call(
        _expert_kernel,
        grid_spec=grid_spec,
        out_shape=jax.ShapeDtypeStruct((n_slots, W), U32),
        compiler_params=_cparams("arbitrary"),
        name="moe_experts",
    )(block_expert, n_used, xs, wg, wu, wd)


def _combine_kernel(dest_ref, dest_next_ref, x1_ref, gate_ref, ys_hbm, wsg_ref, wsu_ref, wsd_ref, g2_ref, b2_ref,
                    yp_ref, ysmp_ref, buf, sem, *, n_prompt_tiles):
    i = pl.program_id(0)
    nt = pl.num_programs(0)
    tm = x1_ref.shape[0]
    slot = i % 2

    def row_copy(dref, s, r, kk):
        d = dref[r * TOP_K + kk]
        return pltpu.make_async_copy(ys_hbm.at[pl.ds(d, 1), :], buf.at[s, kk, pl.ds(r, 1), :], sem.at[s])

    def wait_all(s):
        for kk in range(TOP_K):
            pltpu.make_async_copy(ys_hbm.at[pl.ds(0, tm), :], buf.at[s, kk], sem.at[s]).wait()

    @pl.when(i == 0)
    def _():
        def body(r, c):
            for kk in range(TOP_K):
                row_copy(dest_ref, 0, r, kk).start(priority=kk % 2)
            return c

        lax.fori_loop(0, tm, body, 0)

    wait_all(slot)
    x1 = x1_ref[...]
    xb = x1.astype(BF16)
    g = _dot(xb, wsg_ref[...])
    u = _dot(xb, wsu_ref[...])
    ff = _dot((g * jax.nn.sigmoid(g) * u).astype(BF16), wsd_ref[...])
    gate = gate_ref[...]
    for kk in range(TOP_K):
        ff = ff + gate[:, kk:kk + 1] * _unpack_bf16_pairs(buf[slot, kk])
    y = _layer_norm(ALPHA * x1 + ff, g2_ref[...], b2_ref[...])

    for r in range(tm):
        for kk in range(TOP_K):
            row_copy(dest_next_ref, 1 - slot, r, kk).start(priority=kk % 2)

    @pl.when(i < n_prompt_tiles)
    def _():
        yp_ref[0] = y

    @pl.when(i >= n_prompt_tiles)
    def _():
        ysmp_ref[...] = y

    @pl.when(i + 1 == nt)
    def _():
        wait_all(1 - slot)


def _moe_combine(dest_flat, x1, gate, ys, wsg, wsu, wsd, ln2_g, ln2_b, geo, seq):
    R = x1.shape[0]
    nt = R // ROW_TILE
    W = ys.shape[1]
    B, tpb, npt = geo["B"], geo["tpb"], geo["npt"]
    assert seq == (tpb - 1) * ROW_TILE

    def rows(w):
        return pl.BlockSpec((ROW_TILE, w), lambda i: (i, 0))

    def full(a):
        return pl.BlockSpec(a.shape, lambda i: (0,) * a.ndim)

    def yp_map(i):
        in_prompt = i < npt
        return (jnp.minimum(i // tpb, B - 1), jnp.where(in_prompt, jnp.maximum(i % tpb - 1, 0), tpb - 2), 0)

    return pl.pallas_call(
        functools.partial(_combine_kernel, n_prompt_tiles=npt),
        grid=(R // ROW_TILE,),
        in_specs=[pl.BlockSpec((ROW_TILE * TOP_K,), lambda i: (i,), memory_space=pltpu.SMEM),
                  pl.BlockSpec((ROW_TILE * TOP_K,), lambda i: (jnp.minimum(i + 1, nt - 1),),
                               memory_space=pltpu.SMEM),
                  rows(D_MODEL), rows(LANES), pl.BlockSpec(memory_space=pl.ANY),
                  full(wsg), full(wsu), full(wsd), full(ln2_g), full(ln2_b)],
        out_specs=[pl.BlockSpec((1, ROW_TILE, D_MODEL), yp_map),
                   pl.BlockSpec((ROW_TILE, D_MODEL), lambda i: (jnp.maximum(i - npt, 0), 0))],
        out_shape=[jax.ShapeDtypeStruct((B, seq, D_MODEL), F32),
                   jax.ShapeDtypeStruct((R - npt * ROW_TILE, D_MODEL), F32)],
        scratch_shapes=[pltpu.VMEM((2, TOP_K, ROW_TILE, W), U32), pltpu.SemaphoreType.DMA((2,))],
        compiler_params=_cparams("arbitrary"),
        name="moe_combine",
    )(dest_flat, dest_flat, x1, gate, ys, wsg, wsu, wsd, ln2_g, ln2_b)


def _routing_tables(idx8, rank8, counts, n_slots_blocks):
    R = idx8.shape[0]
    padded = (counts + MOE_BLOCK - 1) // MOE_BLOCK * MOE_BLOCK
    pends = jnp.cumsum(padded)
    pstarts = pends - padded
    experts = jnp.arange(N_EXPERTS, dtype=I32)
    start8 = jnp.sum(jnp.where(idx8[:, :, None] == experts[None, None, :], pstarts[None, None, :], 0), axis=-1)
    dest = start8 + rank8
    blk_start = jnp.arange(n_slots_blocks, dtype=I32) * MOE_BLOCK
    block_expert = jnp.minimum(jnp.sum((blk_start[:, None] >= pends[None, :]).astype(I32), axis=1), N_EXPERTS - 1)
    n_used = (pends[-1] // MOE_BLOCK).astype(I32).reshape(1)
    return dest.reshape(R * TOP_K).astype(I32), pends.astype(I32), padded.astype(I32), block_expert, n_used


def kernel(x_prompt, x_sample, cache_mla_ckv, cache_mla_krope, state_hgrn, page_table, meta_tokens, hg_lb_logits,
           w_in, q_norm, kv_norm, w_uq, w_uk, w_uv, hg_norm, w_br_mla, w_br_hg, w_out, ln1_g, ln1_b, router_w,
           router_bias, w_exp_gate, w_exp_up, w_exp_down, w_sh_gate, w_sh_up, w_sh_down, ln2_g, ln2_b):
    assert w_in.shape[0] == DEPTH
    B, seq, _ = x_prompt.shape
    DB, steps, _ = x_sample.shape
    n_meta = meta_tokens.shape[0]
    n_pages, page = page_table.shape[1], cache_mla_ckv.shape[2]
    past = n_pages * page
    T = n_meta + seq
    tp = -(-T // ROW_TILE) * ROW_TILE
    pad_front = tp - T
    tpb = tp // ROW_TILE
    n_s = DB * steps
    assert n_s % ROW_TILE == 0 and DB % HG_SAMPLE_BATCH == 0
    geo = dict(B=B, tp=tp, tpb=tpb, pad_front=pad_front, npt=B * tpb)
    Rp = B * tp
    R = Rp + n_s

    assert pad_front + n_meta == ROW_TILE and seq % ROW_TILE == 0
    x_head = jnp.concatenate([jnp.zeros((pad_front, D_MODEL), F32), meta_tokens.astype(F32)], axis=0)
    x_smp = x_sample.reshape(n_s, D_MODEL)

    pos_p = jnp.maximum(jnp.arange(tp) - pad_front, 0)
    pos = jnp.concatenate([jnp.tile(pos_p, B), jnp.tile(past + jnp.arange(steps), DB)]).astype(F32)
    half = MLA_ROPE // 2
    inv = ROPE_THETA ** (-jnp.arange(half, dtype=F32) / half)
    ang = pos[:, None] * inv[None, :]
    cos8 = jnp.tile(jnp.cos(ang), (1, LANES // half))
    sin8 = jnp.tile(jnp.sin(ang), (1, LANES // half))

    l = 0
    win = w_in[l]
    kr_end = MLA_Q_RANK + MLA_KV_RANK + MLA_ROPE
    w_in_p = jnp.concatenate([win[:, :kr_end], jnp.zeros((D_MODEL, LANES - MLA_ROPE), F32), win[:, kr_end:]],
                             axis=1).astype(BF16)
    assert w_in_p.shape[1] == D_IN_PACKED
    wq3 = w_uq[l].reshape(MLA_Q_RANK, MLA_HEADS, MLA_NOPE + MLA_ROPE)
    wqr = (jnp.concatenate([wq3[:, :, MLA_NOPE:MLA_NOPE + half].reshape(MLA_Q_RANK, MLA_HEADS * half),
                            wq3[:, :, MLA_NOPE + half:].reshape(MLA_Q_RANK, MLA_HEADS * half)], axis=1)
           * Q_SCALE).astype(BF16)
    src = jnp.arange(2 * LANES)
    hh, ii = (src % LANES) // half, src % half
    dst = hh * LANES + ii + jnp.where(src >= LANES, half, 0)
    perm = (dst[:, None] == jnp.arange(MLA_HEADS * LANES)[None, :]).astype(BF16)
    lb = jnp.cumsum(jax.nn.softmax(hg_lb_logits.astype(F32), axis=0), axis=0)[l].reshape(1, -1)

    wqlat, wo = _weight_prep(w_uq[l], w_uk[l], w_uv[l], w_br_mla[l])

    (qcat, kcat, ckv, kr, hq, hk, hv, lf, sg, sm, sh) = _in_projection(
        x_prompt, x_head, x_smp, w_in_p, q_norm[l].reshape(1, -1), kv_norm[l].reshape(1, -1), wqlat, wqr, perm,
        cos8, sin8, lb, geo)

    o_lat_p = _prompt_attention(qcat, kcat, geo)
    q_s = qcat[:, Rp:].reshape(MLA_HEADS, DB, steps, QCAT)
    knew_s = kcat[Rp:].reshape(DB, steps, 2 * LANES)
    cache_krt = jnp.swapaxes(cache_mla_krope[l], 1, 2)
    o_lat_s = _sample_attention(page_table, q_s, knew_s, cache_mla_ckv[l], cache_krt)
    o_lat_s = o_lat_s.reshape(n_s, MLA_HEADS * LANES)
    o_hg_p, s_prompt = _hgrn_prompt(hq, hk, hv, lf, geo)
    o_hg_s, s_sample = _hgrn_sample(hq, hk, hv, lf, state_hgrn[l], Rp, steps)

    rw_f = jnp.concatenate([router_w[l], jnp.zeros((D_MODEL, LANES - N_EXPERTS), F32)], axis=1)
    rw_hi = rw_f.astype(BF16)
    rw_p = jnp.stack([rw_hi, (rw_f - rw_hi.astype(F32)).astype(BF16)])
    rb_p = jnp.concatenate([router_bias[l].astype(F32), jnp.zeros((LANES - N_EXPERTS,), F32)]).reshape(1, -1)
    x1, x1p, idx, gate, rank, count = _layer_tail(
        x_prompt, x_head, x_smp, o_lat_p, o_lat_s, o_hg_p, o_hg_s, sg, sm, sh, wo, w_br_hg[l].astype(BF16),
        w_out[l].astype(BF16), hg_norm[l].reshape(1, -1), ln1_g[l].reshape(1, -1), ln1_b[l].reshape(1, -1),
        rw_p, rb_p, geo)

    nblk = R * TOP_K // MOE_BLOCK + N_EXPERTS
    dest_flat, pends, padded, block_expert, n_used = _routing_tables(
        idx[:, :TOP_K], rank[:, :TOP_K], count[0, :N_EXPERTS], nblk)
    xs = _moe_dispatch(pends, padded, dest_flat, x1p, nblk * MOE_BLOCK)
    ys = _moe_experts(block_expert, n_used, xs, w_exp_gate[l], w_exp_up[l], w_exp_down[l])
    yp, y_smp = _moe_combine(dest_flat, x1, gate, ys, w_sh_gate[l].astype(BF16), w_sh_up[l].astype(BF16),
                             w_sh_down[l].astype(BF16), ln2_g[l].reshape(1, -1), ln2_b[l].reshape(1, -1), geo, seq)

    ys_out = y_smp.reshape(DB, steps, D_MODEL)
    ckv_p = ckv[:Rp].reshape(B, tp, MLA_KV_RANK)[:, pad_front:][None]
    kr_p = kr[:Rp].reshape(B, tp, LANES)[:, pad_front:, :MLA_ROPE][None]
    ckv_s = ckv[Rp:].reshape(DB, steps, MLA_KV_RANK)[None]
    kr_s = kr[Rp:, :MLA_ROPE].reshape(DB, steps, MLA_ROPE)[None]
    return (yp, ys_out, ckv_p, kr_p, s_prompt[None], ckv_s, kr_s, s_sample[None])
```

```python
import functools

import jax
import jax.numpy as jnp
from jax import lax
from jax.experimental import pallas as pl
from jax.experimental.pallas import tpu as pltpu
from jax.experimental.pallas import tpu_sc as plsc

F32 = jnp.float32
BF16 = jnp.bfloat16
U32 = jnp.uint32
I32 = jnp.int32

D_MODEL = 1024
MLA_HEADS = 8
MLA_Q_RANK = 256
MLA_KV_RANK = 128
MLA_NOPE = 64
MLA_ROPE = 32
MLA_V = 64
MLA_SCALE = (MLA_NOPE + MLA_ROPE) ** -0.5
LOG2E = 1.4426950408889634
Q_SCALE = MLA_SCALE * LOG2E
ROPE_THETA = 10000.0
HG_HEADS = 4
HG_DK = 128
HG_DV = 128
HG_CHUNK = 64
N_EXPERTS = 64
TOP_K = 8
D_EXPERT = 256
ROUTED_SCALE = 2.5
NORM_EPS = 1e-6
DEPTH = 1
ALPHA = (2.0 * DEPTH) ** 0.25

LANES = 128
ROW_TILE = 256
QCAT = 256
ATT_TK = 256
MOE_BLOCK = 512
PAGES_PER_CHUNK = 64
ONES_COL = LANES + MLA_ROPE
HG_SAMPLE_BATCH = 8
SATT_BATCH = 1
SC_CORES = 2
SC_SUBCORES = 16
SC_GATHER_ROWS = 128
NEG_BIG = -1e30
VMEM_LIMIT = 52 * 1024 * 1024

C_CQ = (0, 256)
C_CKV = (256, 384)
C_KR = (384, 512)
C_HQ = (512, 1024)
C_HF = (1024, 1536)
C_HI = (1536, 2048)
C_HG = (2048, 2560)
C_GM = (2560, 3584)
C_GH = (3584, 4608)
D_IN_PACKED = 4608


def _cparams(*sem):
    return pltpu.CompilerParams(dimension_semantics=sem, vmem_limit_bytes=VMEM_LIMIT)


def _dot(a, b):
    return jnp.dot(a, b, preferred_element_type=F32)


def _dot_nt(a, b):
    return lax.dot_general(a, b, (((1,), (1,)), ((), ())), preferred_element_type=F32)


def _dot_tn(a, b):
    return lax.dot_general(a, b, (((0,), (0,)), ((), ())), preferred_element_type=F32)


def _pack_bf16_pairs(x):
    w = x.shape[1] // 2
    bits = pltpu.bitcast(x.astype(BF16).astype(F32), U32)
    return bits[:, w:] | (bits[:, :w] >> 16)


def _unpack_bf16_pairs(p):
    lo = pltpu.bitcast(p << 16, F32)
    hi = pltpu.bitcast(p & jnp.uint32(0xFFFF0000), F32)
    return jnp.concatenate([lo, hi], axis=1)


def _x_specs(geo, seq_tiles):
    B, tpb, npt = geo["B"], geo["tpb"], geo["npt"]

    def xp_map(i):
        return (jnp.minimum(i // tpb, B - 1), jnp.where(i < npt, jnp.maximum(i % tpb - 1, 0), seq_tiles - 1), 0)

    return [pl.BlockSpec((1, ROW_TILE, D_MODEL), xp_map),
            pl.BlockSpec((ROW_TILE, D_MODEL), lambda i: (0, 0)),
            pl.BlockSpec((ROW_TILE, D_MODEL), lambda i: (jnp.maximum(i - npt, 0), 0))]


def _select_x(i, xp_ref, xh_ref, xs_ref, tiles_per_batch, n_prompt_tiles):
    is_head = (i < n_prompt_tiles) & (i % tiles_per_batch == 0)
    return jnp.where(i >= n_prompt_tiles, xs_ref[...], jnp.where(is_head, xh_ref[...], xp_ref[0]))


def _split_specs(width, npt):
    return [pl.BlockSpec((ROW_TILE, width), lambda i: (jnp.minimum(i, npt - 1), 0)),
            pl.BlockSpec((ROW_TILE, width), lambda i: (jnp.maximum(i - npt, 0), 0))]


def _wprep_kernel(wq_nope_ref, wuk_ref, wuv_ref, wbr_ref, wqlat_ref, wo_ref):
    hp = lax.Precision.HIGHEST
    for h in range(MLA_HEADS):
        a = wq_nope_ref[h]
        b = wuk_ref[h]
        ql = lax.dot_general(a, b, (((1,), (1,)), ((), ())), precision=hp, preferred_element_type=F32)
        wqlat_ref[:, h * MLA_KV_RANK:(h + 1) * MLA_KV_RANK] = (ql * Q_SCALE).astype(BF16)
        c = wuv_ref[h]
        d = wbr_ref[h]
        wo_ref[h * MLA_KV_RANK:(h + 1) * MLA_KV_RANK, :] = jnp.dot(
            c, d, precision=hp, preferred_element_type=F32).astype(BF16)


def _weight_prep(w_uq, w_uk, w_uv, w_br_mla):
    wq3 = w_uq.reshape(MLA_Q_RANK, MLA_HEADS, MLA_NOPE + MLA_ROPE)
    wq_nope = jnp.transpose(wq3[:, :, :MLA_NOPE], (1, 0, 2))
    wuk = jnp.transpose(w_uk, (1, 0, 2))
    wuv = jnp.transpose(w_uv, (1, 0, 2))
    wbr = w_br_mla.reshape(MLA_HEADS, MLA_V, D_MODEL)
    return pl.pallas_call(
        _wprep_kernel,
        out_shape=(jax.ShapeDtypeStruct((MLA_Q_RANK, MLA_HEADS * MLA_KV_RANK), BF16),
                   jax.ShapeDtypeStruct((MLA_HEADS * MLA_KV_RANK, D_MODEL), BF16)),
        compiler_params=pltpu.CompilerParams(vmem_limit_bytes=VMEM_LIMIT),
        name="weight_prep",
    )(wq_nope, wuk, wuv, wbr)


def _inproj_kernel(xp_ref, xh_ref, xs_ref, w_ref, qn_ref, kvn_ref, wqlat_ref, wqr_ref, perm_ref, cos_ref, sin_ref,
                   lb_ref, qcat_ref, kcat_ref, ckv_ref, kr_ref, hq_ref, hk_ref, hv_ref, lf_ref, sg_ref, sm_ref,
                   sh_ref, *, pad_front, tiles_per_batch, n_prompt_tiles):
    i = pl.program_id(0)
    tm = xh_ref.shape[0]
    xb = _select_x(i, xp_ref, xh_ref, xs_ref, tiles_per_batch, n_prompt_tiles).astype(BF16)

    def proj(c):
        return _dot(xb, w_ref[:, c[0]:c[1]])

    cos8 = cos_ref[...]
    sin8 = sin_ref[...]

    cq = proj(C_CQ)
    cqn = cq * lax.rsqrt(jnp.mean(cq * cq, axis=-1, keepdims=True) + NORM_EPS) * qn_ref[...]
    cqb = cqn.astype(BF16)
    qlat = _dot(cqb, wqlat_ref[...])
    qr = _dot(cqb, wqr_ref[...])
    x1, x2 = qr[:, :LANES], qr[:, LANES:]
    qrot = jnp.concatenate([x1 * cos8 - x2 * sin8, x2 * cos8 + x1 * sin8], axis=1).astype(BF16)
    qrh = _dot(qrot, perm_ref[...])
    for h in range(MLA_HEADS):
        qcat_ref[h, :, :LANES] = qlat[:, h * LANES:(h + 1) * LANES].astype(BF16)
        qcat_ref[h, :, LANES:] = qrh[:, h * LANES:(h + 1) * LANES].astype(BF16)

    kv = proj(C_CKV)
    ckv = kv * lax.rsqrt(jnp.mean(kv * kv, axis=-1, keepdims=True) + NORM_EPS) * kvn_ref[...]
    ckv_ref[...] = ckv
    kcat_ref[:, :LANES] = ckv.astype(BF16)
    krr = proj(C_KR)
    lane = lax.broadcasted_iota(I32, (tm, LANES), 1)
    half = MLA_ROPE // 2
    rot = jnp.where(lane < half, -pltpu.roll(krr, LANES - half, 1), pltpu.roll(krr, half, 1))
    kr = jnp.where(lane < MLA_ROPE, krr * cos8 + rot * sin8, 0.0)
    kr_ref[...] = kr
    kcat_ref[:, LANES:] = jnp.where(lane == ONES_COL - LANES, 1.0, kr).astype(BF16)

    row = lax.broadcasted_iota(I32, (tm, 1), 0)
    is_pad = (i < n_prompt_tiles) & (i % tiles_per_batch == 0) & (row < pad_front)
    keep = jnp.where(is_pad, 0.0, 1.0)
    hq_ref[...] = proj(C_HQ).astype(BF16)
    lb = lb_ref[...]
    f = lb + (1.0 - lb) * jax.nn.sigmoid(proj(C_HF))
    lf_ref[...] = jnp.log(f) * keep
    hk_ref[...] = ((1.0 - f) * keep).astype(BF16)
    hv_ref[...] = proj(C_HI).astype(BF16)

    g = proj(C_HG)
    sg_ref[...] = (g * jax.nn.sigmoid(g)).astype(BF16)
    sm_ref[...] = jax.nn.sigmoid(proj(C_GM)).astype(BF16)
    sh_ref[...] = jax.nn.sigmoid(proj(C_GH)).astype(BF16)


def _in_projection(x_prompt, x_head, x_smp, w_in_p, q_norm, kv_norm, wqlat, wqr, perm, cos8, sin8, lb, geo):
    R = cos8.shape[0]
    nt = R // ROW_TILE
    hgw = HG_HEADS * HG_DK
    x_specs = _x_specs(geo, x_prompt.shape[1] // ROW_TILE)

    def rows(w):
        return pl.BlockSpec((ROW_TILE, w), lambda i: (i, 0))

    def full(a):
        return pl.BlockSpec(a.shape, lambda i: (0,) * a.ndim)

    out_widths = [(2 * LANES, BF16), (LANES, F32), (LANES, F32),
                  (hgw, BF16), (hgw, BF16), (hgw, BF16), (hgw, F32), (hgw, BF16),
                  (D_MODEL, BF16), (D_MODEL, BF16)]
    qcat_spec = pl.BlockSpec((MLA_HEADS, ROW_TILE, QCAT), lambda i: (0, i, 0))
    return pl.pallas_call(
        functools.partial(_inproj_kernel, pad_front=geo["pad_front"], tiles_per_batch=geo["tpb"],
                          n_prompt_tiles=geo["npt"]),
        grid=(nt,),
        in_specs=x_specs + [full(w_in_p), full(q_norm), full(kv_norm), full(wqlat), full(wqr), full(perm),
                            rows(LANES), rows(LANES), full(lb)],
        out_specs=[qcat_spec] + [rows(w) for w, _ in out_widths],
        out_shape=[jax.ShapeDtypeStruct((MLA_HEADS, R, QCAT), BF16)]
        + [jax.ShapeDtypeStruct((R, w), dt) for w, dt in out_widths],
        compiler_params=_cparams("arbitrary"),
        name="in_projection",
    )(x_prompt, x_head, x_smp, w_in_p, q_norm, kv_norm, wqlat, wqr, perm, cos8, sin8, lb)


def _softmax_step(s, v_b, m_ref, l_ref, acc_ref):
    n = s.shape[1] // LANES
    m_prev = m_ref[...]
    m_next = jnp.maximum(m_prev, jnp.max(s, axis=1, keepdims=True))
    p = jnp.concatenate([jnp.exp2(s[:, j * LANES:(j + 1) * LANES] - m_next) for j in range(n)], axis=1)
    alpha = jnp.exp2(m_prev - m_next)
    pv = _dot(p.astype(BF16), v_b)
    if v_b.shape[1] == LANES:
        l_ref[...] = alpha * l_ref[...] + jnp.sum(p, axis=1, keepdims=True)
    else:
        l_ref[...] = alpha * l_ref[...] + pv[:, LANES:]
    acc_ref[...] = alpha * acc_ref[...] + pv[:, :LANES]
    m_ref[...] = m_next


def _pattn_kernel(q_ref, k_ref, o_ref, m_ref, l_ref, acc_ref, *, pad_front):
    qi = pl.program_id(1)
    nh, tq, _ = q_ref.shape
    tk = ATT_TK
    rows = nh * tq

    q = q_ref[...].reshape(rows, QCAT)
    m_ref[...] = jnp.full(m_ref.shape, -jnp.inf, F32)
    l_ref[...] = jnp.zeros(l_ref.shape, F32)
    acc_ref[...] = jnp.zeros(acc_ref.shape, F32)

    def step(kb, masked):
        kblk = k_ref[pl.ds(pl.multiple_of(kb * tk, tk), tk), :]
        s = _dot_nt(q, kblk)
        if masked:
            qrow = qi * tq + lax.broadcasted_iota(I32, (rows, tk), 0) % tq
            krow = kb * tk + lax.broadcasted_iota(I32, (rows, tk), 1)
            s = jnp.where((krow <= qrow) & (krow >= pad_front), s, NEG_BIG)
        _softmax_step(s, kblk, m_ref, l_ref, acc_ref)

    step(0, True)

    def body(kb, c):
        step(kb, False)
        return c

    lax.fori_loop(1, qi, body, 0)

    @pl.when(qi > 0)
    def _():
        step(qi, True)

    row_sum = l_ref[:, ONES_COL - LANES:ONES_COL - LANES + 1]
    o = (acc_ref[...] / row_sum).astype(BF16)
    for h in range(nh):
        o_ref[:, h * LANES:(h + 1) * LANES] = o[h * tq:(h + 1) * tq]


def _prompt_attention(qcat, kcat, geo):
    B, tpb, tp = geo["B"], geo["tpb"], geo["tp"]
    rows = MLA_HEADS * ROW_TILE
    return pl.pallas_call(
        functools.partial(_pattn_kernel, pad_front=geo["pad_front"]),
        grid=(B, tpb),
        in_specs=[pl.BlockSpec((MLA_HEADS, ROW_TILE, QCAT), lambda b, i: (0, b * tpb + i, 0)),
                  pl.BlockSpec((tp, 2 * LANES), lambda b, i: (b, 0))],
        out_specs=pl.BlockSpec((ROW_TILE, MLA_HEADS * LANES), lambda b, i: (b * tpb + i, 0)),
        out_shape=jax.ShapeDtypeStruct((B * tp, MLA_HEADS * LANES), BF16),
        scratch_shapes=[pltpu.VMEM((rows, LANES), F32)] * 3,
        compiler_params=_cparams("arbitrary", "arbitrary"),
        name="prompt_attention",
    )(qcat, kcat)


def _sattn_kernel(pt_ref, q_ref, knew_ref, ckv_hbm, kr_hbm, o_ref,
                  ckv_buf, kr_buf, sem, m_ref, l_ref, acc_ref, *, n_chunks, page):
    g = pl.program_id(0)
    ng = pl.num_programs(0)
    ch = PAGES_PER_CHUNK
    nh, nbt, S = q_ref.shape[0], q_ref.shape[1], q_ref.shape[2]
    rows = nh * S

    def chunk_copies(gg, c, slot):
        cps = []
        for bb in range(nbt):
            for j in range(ch):
                pg = pt_ref[gg * nbt + bb, c * ch + j]
                cps.append(pltpu.make_async_copy(ckv_hbm.at[pg], ckv_buf.at[slot, bb, pl.ds(j * page, page), :],
                                                 sem.at[0, slot]))
                cps.append(pltpu.make_async_copy(kr_hbm.at[pg], kr_buf.at[slot, bb, :, pl.ds(j * page, page)],
                                                 sem.at[1, slot]))
        return cps

    @pl.when(g == 0)
    def _():
        for cp in chunk_copies(0, 0, 0):
            cp.start()

    qlat, qrope = [], []
    for bb in range(nbt):
        q = q_ref[:, bb].reshape(rows, QCAT)
        qlat.append(q[:, :LANES])
        qrope.append(q[:, LANES:LANES + MLA_ROPE])
        knew = knew_ref[bb]
        s_new = _dot_nt(q, knew)
        qtok = lax.broadcasted_iota(I32, (rows, S), 0) % S
        ktok = lax.broadcasted_iota(I32, (rows, S), 1)
        s_new = jnp.where(ktok <= qtok, s_new, NEG_BIG)
        m0 = jnp.max(s_new, axis=1, keepdims=True)
        p0 = jnp.exp2(s_new - m0)
        m_ref[bb] = jnp.broadcast_to(m0, (rows, LANES))
        l_ref[bb] = jnp.broadcast_to(jnp.sum(p0, axis=1, keepdims=True), (rows, LANES))
        acc_ref[bb] = _dot(p0.astype(BF16), knew[:, :LANES])

    for c in range(n_chunks):
        slot = c % 2 if n_chunks % 2 == 0 else (g * n_chunks + c) % 2
        if c + 1 < n_chunks:
            for cp in chunk_copies(g, c + 1, 1 - slot):
                cp.start()
        else:
            @pl.when(g + 1 < ng)
            def _():
                for cp in chunk_copies(g + 1, 0, 1 - slot):
                    cp.start()
        for cp in chunk_copies(g, c, slot):
            cp.wait()
        for bb in range(nbt):
            ckv_b = ckv_buf[slot, bb].astype(BF16)
            krt_b = kr_buf[slot, bb].astype(BF16)
            s = _dot_nt(qlat[bb], ckv_b) + _dot(qrope[bb], krt_b)
            _softmax_step(s, ckv_b, m_ref.at[bb], l_ref.at[bb], acc_ref.at[bb])

    for bb in range(nbt):
        o = (acc_ref[bb] / l_ref[bb]).astype(BF16)
        for h in range(nh):
            o_ref[bb, :, h * LANES:(h + 1) * LANES] = o[h * S:(h + 1) * S]


def _sample_attention(page_table, q_s, knew_s, cache_ckv, cache_krt):
    nh, DB, S, _ = q_s.shape
    rows = nh * S
    n_pages = page_table.shape[1]
    page = cache_ckv.shape[1]
    assert n_pages % PAGES_PER_CHUNK == 0
    n_chunks = n_pages // PAGES_PER_CHUNK
    ck = PAGES_PER_CHUNK * page
    nbt = SATT_BATCH
    assert DB % nbt == 0
    grid_spec = pltpu.PrefetchScalarGridSpec(
        num_scalar_prefetch=1,
        grid=(DB // nbt,),
        in_specs=[pl.BlockSpec((nh, nbt, S, QCAT), lambda g, pt: (0, g, 0, 0)),
                  pl.BlockSpec((nbt, S, 2 * LANES), lambda g, pt: (g, 0, 0)),
                  pl.BlockSpec(memory_space=pl.ANY),
                  pl.BlockSpec(memory_space=pl.ANY)],
        out_specs=pl.BlockSpec((nbt, S, nh * LANES), lambda g, pt: (g, 0, 0)),
        scratch_shapes=[pltpu.VMEM((2, nbt, ck, MLA_KV_RANK), F32),
                        pltpu.VMEM((2, nbt, MLA_ROPE, ck), F32),
                        pltpu.SemaphoreType.DMA((2, 2)),
                        pltpu.VMEM((nbt, rows, LANES), F32),
                        pltpu.VMEM((nbt, rows, LANES), F32),
                        pltpu.VMEM((nbt, rows, LANES), F32)])
    return pl.pallas_call(
        functools.partial(_sattn_kernel, n_chunks=n_chunks, page=page),
        grid_spec=grid_spec,
        out_shape=jax.ShapeDtypeStruct((DB, S, nh * LANES), BF16),
        compiler_params=_cparams("arbitrary"),
        name="sample_attention",
    )(page_table, q_s, knew_s, cache_ckv, cache_krt)


def _split3(x):
    hi = x.astype(BF16)
    r1 = x - hi.astype(F32)
    mid = r1.astype(BF16)
    lo = (r1 - mid.astype(F32)).astype(BF16)
    return hi, mid, lo


def _hgrn_chunk(q, k, v, lf, S0, tri, group):
    C = q.shape[0]
    hi, mid, lo = _split3(lf)
    cum = _dot(tri, hi) + _dot(tri, mid) + _dot(tri, lo)
    vb = v.astype(BF16)
    rowi = lax.broadcasted_iota(I32, (C, C), 0)
    coli = lax.broadcasted_iota(I32, (C, C), 1)
    attn = jnp.zeros((C, C), F32)

    bd = min(16, group)
    hs = group // 2
    rid = lax.broadcasted_iota(I32, (C, 1), 0)
    while hs >= bd:
        npair = C // (2 * hs)
        ref = jnp.concatenate(
            [jnp.broadcast_to(cum[(2 * j + 1) * hs - 1:(2 * j + 1) * hs, :], (2 * hs, cum.shape[1]))
             for j in range(npair)], axis=0)
        odd = ((rid // hs) % 2) == 1
        e = jnp.exp(jnp.where(odd, cum - ref, ref - cum))
        qs = jnp.where(odd, q * e, 0.0).astype(BF16)
        ks = jnp.where(odd, 0.0, k * e).astype(BF16)
        a = _dot_nt(qs, ks)
        attn = attn + jnp.where((rowi // (2 * hs)) == (coli // (2 * hs)), a, 0.0)
        hs //= 2

    nblk = C // bd
    k3 = k.reshape(nblk, bd, k.shape[1])
    c3 = cum.reshape(nblk, bd, cum.shape[1])
    tl = rid % bd
    blk_base = (rowi // bd) * bd
    for sl in range(bd):
        ks_b = jnp.broadcast_to(k3[:, sl:sl + 1, :], k3.shape).reshape(C, k.shape[1])
        cs_b = jnp.broadcast_to(c3[:, sl:sl + 1, :], c3.shape).reshape(C, k.shape[1])
        e = jnp.exp(jnp.where(tl >= sl, cum - cs_b, NEG_BIG))
        col = jnp.sum(q * ks_b * e, axis=1, keepdims=True)
        attn = jnp.where(coli == blk_base + sl, col, attn)
    o_intra = _dot(attn.astype(BF16), vb)
    return o_intra, cum


def _tri(C, group):
    r = lax.broadcasted_iota(I32, (C, C), 0)
    c = lax.broadcasted_iota(I32, (C, C), 1)
    return jnp.where((c <= r) & (r // group == c // group), 1.0, 0.0).astype(BF16)


def _state_update(q, k, v, cum, S, o_intra):
    C = q.shape[0]
    last = cum[C - 1:C, :]
    o = o_intra + _dot((q * jnp.exp(cum)).astype(BF16), S.astype(BF16))
    kst = (k * jnp.exp(last - cum)).astype(BF16)
    dfull = jnp.transpose(jnp.broadcast_to(jnp.exp(last), (S.shape[1], S.shape[0])))
    S_new = dfull * S + _dot_tn(kst, v.astype(BF16))
    return o, S_new


def _hgrn_prompt_kernel(q_ref, k_ref, v_ref, lf_ref, o_ref, s_out_ref, s_ref):
    t = pl.program_id(1)

    @pl.when(t == 0)
    def _():
        s_ref[...] = jnp.zeros(s_ref.shape, F32)

    C = HG_CHUNK
    tri = _tri(C, C)
    states = [s_ref[h] for h in range(HG_HEADS)]
    for c in range(q_ref.shape[0] // C):
        sl = slice(c * C, (c + 1) * C)
        for h in range(HG_HEADS):
            hl = slice(h * HG_DK, (h + 1) * HG_DK)
            q = q_ref[sl, hl].astype(F32)
            k = k_ref[sl, hl].astype(F32)
            v = v_ref[sl, hl].astype(F32)
            lf = lf_ref[sl, hl]
            o_intra, cum = _hgrn_chunk(q, k, v, lf, None, tri, C)
            o, states[h] = _state_update(q, k, v, cum, states[h], o_intra)
            o_ref[sl, hl] = o.astype(BF16)
    for h in range(HG_HEADS):
        s_ref[h] = states[h]

    @pl.when(t == pl.num_programs(1) - 1)
    def _():
        s_out_ref[0] = s_ref[...]


def _hgrn_prompt(hq, hk, hv, lf, geo):
    B, tpb, tp = geo["B"], geo["tpb"], geo["tp"]
    hgw = HG_HEADS * HG_DK

    def blk():
        return pl.BlockSpec((ROW_TILE, hgw), lambda b, t: (b * tpb + t, 0))

    return pl.pallas_call(
        _hgrn_prompt_kernel,
        grid=(B, tpb),
        in_specs=[blk(), blk(), blk(), blk()],
        out_specs=[blk(), pl.BlockSpec((1, HG_HEADS, HG_DK, HG_DV), lambda b, t: (b, 0, 0, 0))],
        out_shape=[jax.ShapeDtypeStruct((B * tp, hgw), BF16),
                   jax.ShapeDtypeStruct((B, HG_HEADS, HG_DK, HG_DV), F32)],
        scratch_shapes=[pltpu.VMEM((HG_HEADS, HG_DK, HG_DV), F32)],
        compiler_params=_cparams("arbitrary", "arbitrary"),
        name="hgrn_prompt",
    )(hq, hk, hv, lf)


def _hgrn_sample_kernel(q_ref, k_ref, v_ref, lf_ref, s_in_ref, o_ref, s_out_ref, *, steps):
    nb = s_in_ref.shape[0]
    C = nb * steps
    tri = _tri(C, steps)
    for h in range(HG_HEADS):
        sl = slice(h * HG_DK, (h + 1) * HG_DK)
        q = q_ref[:, sl].astype(F32)
        k = k_ref[:, sl].astype(F32)
        v = v_ref[:, sl].astype(F32)
        lf = lf_ref[:, sl]
        o_intra, cum = _hgrn_chunk(q, k, v, lf, None, tri, steps)
        for b in range(nb):
            r = slice(b * steps, (b + 1) * steps)
            o, S_new = _state_update(q[r], k[r], v[r], cum[r], s_in_ref[b, h], o_intra[r])
            o_ref[r, sl] = o.astype(BF16)
            s_out_ref[b, h] = S_new


def _hgrn_sample(hq, hk, hv, lf, state, row0, steps):
    DB = state.shape[0]
    nb = HG_SAMPLE_BATCH
    rows = nb * steps
    hgw = HG_HEADS * HG_DK
    blk0 = row0 // rows

    def tok():
        return pl.BlockSpec((rows, hgw), lambda i: (blk0 + i, 0))

    st = pl.BlockSpec((nb, HG_HEADS, HG_DK, HG_DV), lambda i: (i, 0, 0, 0))
    return pl.pallas_call(
        functools.partial(_hgrn_sample_kernel, steps=steps),
        grid=(DB // nb,),
        in_specs=[tok(), tok(), tok(), tok(), st],
        out_specs=[pl.BlockSpec((rows, hgw), lambda i: (i, 0)), st],
        out_shape=[jax.ShapeDtypeStruct((DB * steps, hgw), BF16),
                   jax.ShapeDtypeStruct(state.shape, F32)],
        compiler_params=_cparams("arbitrary"),
        name="hgrn_sample",
    )(hq, hk, hv, lf, state)


def _layer_norm(x, g, b):
    mu = jnp.mean(x, axis=-1, keepdims=True)
    xc = x - mu
    var = jnp.mean(xc * xc, axis=-1, keepdims=True)
    return xc * lax.rsqrt(var + NORM_EPS) * g + b


def _tail_kernel(xp_ref, xh_ref, xs_ref, olp_ref, ols_ref, ohp_ref, ohs_ref, sg_ref, sm_ref, sh_ref, wo_ref, wbh_ref,
                 wout_ref, hgn_ref, g1_ref, b1_ref, rw_ref, rb_ref, x1_ref, x1p_ref, idx_ref, gate_ref, rank_ref,
                 count_ref, cnt_ref, *, tiles_per_batch, n_prompt_tiles):
    i = pl.program_id(0)
    tm = xh_ref.shape[0]
    x_in = _select_x(i, xp_ref, xh_ref, xs_ref, tiles_per_batch, n_prompt_tiles)
    is_sample = i >= n_prompt_tiles
    mla = _dot(jnp.where(is_sample, ols_ref[...], olp_ref[...]), wo_ref[...])
    oh = jnp.where(is_sample, ohs_ref[...], ohp_ref[...]).astype(F32)
    parts = []
    for h in range(HG_HEADS):
        y = oh[:, h * HG_DV:(h + 1) * HG_DV]
        parts.append(y * lax.rsqrt(jnp.mean(y * y, axis=-1, keepdims=True) + NORM_EPS) * hgn_ref[...])
    hg = (jnp.concatenate(parts, axis=1) * sg_ref[...].astype(F32)).astype(BF16)
    merged = sm_ref[...].astype(F32) * mla + sh_ref[...].astype(F32) * _dot(hg, wbh_ref[...])
    x1 = _layer_norm(ALPHA * x_in + _dot(merged.astype(BF16), wout_ref[...]), g1_ref[...], b1_ref[...])
    x1_ref[...] = x1
    x1p_ref[...] = _pack_bf16_pairs(x1)

    x1_hi = x1.astype(BF16)
    x1_lo = (x1 - x1_hi.astype(F32)).astype(BF16)
    logits = _dot(x1_hi, rw_ref[0]) + _dot(x1_lo, rw_ref[0]) + _dot(x1_hi, rw_ref[1])
    scores = jax.nn.sigmoid(logits)
    lane = lax.broadcasted_iota(I32, (tm, LANES), 1).astype(F32)
    remaining = jnp.where(lane < N_EXPERTS, scores + rb_ref[...], -jnp.inf)
    idx_out = jnp.zeros((tm, LANES), F32)
    gate_out = jnp.zeros((tm, LANES), F32)
    hits = []
    for kk in range(TOP_K):
        mx = jnp.max(remaining, axis=1, keepdims=True)
        pick = jnp.min(jnp.where(remaining == mx, lane, float(LANES)), axis=1, keepdims=True)
        hit = lane == pick
        hits.append(hit)
        gval = jnp.sum(jnp.where(hit, scores, 0.0), axis=1, keepdims=True)
        idx_out = jnp.where(lane == kk, pick, idx_out)
        gate_out = jnp.where(lane == kk, gval, gate_out)
        remaining = jnp.where(hit, -jnp.inf, remaining)
    gate_out = gate_out / jnp.sum(gate_out, axis=1, keepdims=True) * ROUTED_SCALE
    idx_ref[...] = idx_out.astype(I32)
    gate_ref[...] = gate_out

    @pl.when(pl.program_id(0) == 0)
    def _():
        cnt_ref[...] = jnp.zeros(cnt_ref.shape, F32)

    sel = jnp.where(remaining == -jnp.inf, 1.0, 0.0) * jnp.where(lane < N_EXPERTS, 1.0, 0.0)
    r_i = lax.broadcasted_iota(I32, (tm, tm), 0)
    c_i = lax.broadcasted_iota(I32, (tm, tm), 1)
    before = _dot(jnp.where(c_i < r_i, 1.0, 0.0).astype(BF16), sel.astype(BF16)) + cnt_ref[0:1, :]
    rank_out = jnp.zeros((tm, LANES), F32)
    for kk in range(TOP_K):
        rk = jnp.sum(jnp.where(hits[kk], before, 0.0), axis=1, keepdims=True)
        rank_out = jnp.where(lane == kk, rk, rank_out)
    rank_ref[...] = rank_out.astype(I32)
    total = cnt_ref[0:1, :] + jnp.sum(sel, axis=0, keepdims=True)
    cnt_ref[...] = jnp.broadcast_to(total, cnt_ref.shape)
    count_ref[...] = jnp.broadcast_to(total, count_ref.shape).astype(I32)


def _layer_tail(x_prompt, x_head, x_smp, o_lat_p, o_lat_s, o_hg_p, o_hg_s, sg, sm, sh, wo, w_br_hg, w_out, hg_norm,
                ln1_g, ln1_b, rw_p, rb_p, geo):
    R = sg.shape[0]
    npt = geo["npt"]

    def rows(w):
        return pl.BlockSpec((ROW_TILE, w), lambda i: (i, 0))

    def full(a):
        return pl.BlockSpec(a.shape, lambda i: (0,) * a.ndim)

    hgw = HG_HEADS * HG_DV
    return pl.pallas_call(
        functools.partial(_tail_kernel, tiles_per_batch=geo["tpb"], n_prompt_tiles=npt),
        grid=(R // ROW_TILE,),
        in_specs=_x_specs(geo, x_prompt.shape[1] // ROW_TILE)
        + _split_specs(MLA_HEADS * LANES, npt) + _split_specs(hgw, npt)
        + [rows(hgw), rows(D_MODEL), rows(D_MODEL),
           full(wo), full(w_br_hg), full(w_out), full(hg_norm), full(ln1_g), full(ln1_b),
           full(rw_p), full(rb_p)],
        out_specs=[rows(D_MODEL), rows(D_MODEL // 2), rows(LANES), rows(LANES), rows(LANES),
                   pl.BlockSpec((8, LANES), lambda i: (0, 0))],
        out_shape=[jax.ShapeDtypeStruct((R, D_MODEL), F32), jax.ShapeDtypeStruct((R, D_MODEL // 2), U32),
                   jax.ShapeDtypeStruct((R, LANES), I32), jax.ShapeDtypeStruct((R, LANES), F32),
                   jax.ShapeDtypeStruct((R, LANES), I32), jax.ShapeDtypeStruct((8, LANES), I32)],
        scratch_shapes=[pltpu.VMEM((8, LANES), F32)],
        compiler_params=_cparams("arbitrary"),
        name="layer_tail",
    )(x_prompt, x_head, x_smp, o_lat_p, o_lat_s, o_hg_p, o_hg_s, sg, sm, sh, wo, w_br_hg, w_out, hg_norm,
      ln1_g, ln1_b, rw_p, rb_p)


def _dispatch_kernel(pends_ref, padded_ref, dest_ref, x_ref, xs_hbm, zero_buf, stage, sem, zsem):
    i = pl.program_id(0)
    tm = x_ref.shape[0]

    @pl.when(i == 0)
    def _():
        zero_buf[...] = jnp.zeros(zero_buf.shape, U32)

        def zcopy(e):
            start = pl.multiple_of(pends_ref[e] - MOE_BLOCK, MOE_BLOCK)
            return pltpu.make_async_copy(zero_buf, xs_hbm.at[pl.ds(start, MOE_BLOCK), :], zsem)

        n_slots = xs_hbm.shape[0]

        def tcopy(j):
            start = pl.multiple_of(pends_ref[N_EXPERTS - 1] + j * MOE_BLOCK, MOE_BLOCK)
            return pltpu.make_async_copy(zero_buf, xs_hbm.at[pl.ds(start, MOE_BLOCK), :], zsem)

        def tail_live(j):
            return pends_ref[N_EXPERTS - 1] + (j + 1) * MOE_BLOCK <= n_slots

        for e in range(N_EXPERTS):
            @pl.when(padded_ref[e] > 0)
            def _():
                zcopy(e).start()

            @pl.when(tail_live(e))
            def _():
                tcopy(e).start()
        for e in range(N_EXPERTS):
            @pl.when(padded_ref[e] > 0)
            def _():
                zcopy(e).wait()

            @pl.when(tail_live(e))
            def _():
                tcopy(e).wait()

    nt = pl.num_programs(0)
    slot = i % 2

    def wait_all(s):
        for kk in range(TOP_K):
            pltpu.make_async_copy(stage.at[s], xs_hbm.at[pl.ds(0, tm), :], sem.at[s]).wait()

    @pl.when(i >= 2)
    def _():
        wait_all(slot)

    stage[slot] = x_ref[...]
    for r in range(tm):
        for kk in range(TOP_K):
            d = dest_ref[r * TOP_K + kk]
            pltpu.make_async_copy(stage.at[slot, pl.ds(r, 1), :], xs_hbm.at[pl.ds(d, 1), :],
                                  sem.at[slot]).start(priority=kk % 2)

    @pl.when(i == nt - 1)
    def _():
        wait_all(slot)

    @pl.when((i == nt - 1) & (i >= 1))
    def _():
        wait_all(1 - slot)


def _moe_dispatch(pends, padded, dest_flat, x1p, n_slots):
    R, W = x1p.shape
    grid_spec = pltpu.PrefetchScalarGridSpec(
        num_scalar_prefetch=2,
        grid=(R // ROW_TILE,),
        in_specs=[pl.BlockSpec((ROW_TILE * TOP_K,), lambda i, pe, pa: (i,), memory_space=pltpu.SMEM),
                  pl.BlockSpec((ROW_TILE, W), lambda i, pe, pa: (i, 0))],
        out_specs=pl.BlockSpec(memory_space=pl.ANY),
        scratch_shapes=[pltpu.VMEM((MOE_BLOCK, W), U32), pltpu.VMEM((2, ROW_TILE, W), U32),
                        pltpu.SemaphoreType.DMA((2,)), pltpu.SemaphoreType.DMA])
    return pl.pallas_call(
        _dispatch_kernel,
        grid_spec=grid_spec,
        out_shape=jax.ShapeDtypeStruct((n_slots, W), U32),
        compiler_params=_cparams("arbitrary"),
        name="moe_dispatch",
    )(pends, padded, dest_flat, x1p)


def _expert_kernel(be_ref, nused_ref, xs_ref, wg_ref, wu_ref, wd_ref, ys_ref, wgb_ref, wub_ref, wdb_ref):
    i = pl.program_id(0)

    @pl.when((i == 0) | (be_ref[i] != be_ref[jnp.maximum(i - 1, 0)]))
    def _():
        wgb_ref[...] = wg_ref[0].astype(BF16)
        wub_ref[...] = wu_ref[0].astype(BF16)
        wdb_ref[...] = wd_ref[0].astype(BF16)

    @pl.when(i < nused_ref[0])
    def _():
        xb = _unpack_bf16_pairs(xs_ref[...]).astype(BF16)
        g = _dot(xb, wgb_ref[...])
        u = _dot(xb, wub_ref[...])
        hmid = (g * jax.nn.sigmoid(g) * u).astype(BF16)
        ys_ref[...] = _pack_bf16_pairs(_dot(hmid, wdb_ref[...]))

    @pl.when(i >= nused_ref[0])
    def _():
        ys_ref[...] = jnp.zeros(ys_ref.shape, U32)


def _moe_experts(block_expert, n_used, xs, wg, wu, wd):
    n_slots, W = xs.shape
    nblk = n_slots // MOE_BLOCK

    def xmap(i, be, nu):
        return (jnp.minimum(i, nu[0] - 1), 0)

    def wmap(i, be, nu):
        return (be[i], 0, 0)

    grid_spec = pltpu.PrefetchScalarGridSpec(
        num_scalar_prefetch=2,
        grid=(nblk,),
        in_specs=[pl.BlockSpec((MOE_BLOCK, W), xmap),
                  pl.BlockSpec((1, D_MODEL, D_EXPERT), wmap),
                  pl.BlockSpec((1, D_MODEL, D_EXPERT), wmap),
                  pl.BlockSpec((1, D_EXPERT, D_MODEL), wmap)],
        out_specs=pl.BlockSpec((MOE_BLOCK, W), lambda i, be, nu: (i, 0)),
        scratch_shapes=[pltpu.VMEM((D_MODEL, D_EXPERT), BF16), pltpu.VMEM((D_MODEL, D_EXPERT), BF16),
                        pltpu.VMEM((D_EXPERT, D_MODEL), BF16)])
    return pl.pallas_call(
        _expert_kernel,
        grid_spec=grid_spec,
        out_shape=jax.ShapeDtypeStruct((n_slots, W), U32),
        compiler_params=_cparams("arbitrary"),
        name="moe_experts",
    )(block_expert, n_used, xs, wg, wu, wd)


def _sc_row_gather(table, idx):
    n, d = idx.shape[0], table.shape[1]
    nw = SC_CORES * SC_SUBCORES
    per_w = n // nw
    assert n % (nw * SC_GATHER_ROWS) == 0
    n_chunks = per_w // SC_GATHER_ROWS
    mesh = plsc.VectorSubcoreMesh(core_axis_name="c", subcore_axis_name="s",
                                  num_cores=SC_CORES, num_subcores=SC_SUBCORES)

    @functools.partial(
        pl.kernel, mesh=mesh, out_type=jax.ShapeDtypeStruct((n, d), table.dtype),
        scratch_types=[pltpu.VMEM((SC_GATHER_ROWS,), I32), pltpu.VMEM((SC_GATHER_ROWS, d), table.dtype),
                       pltpu.SemaphoreType.DMA],
        name="moe_gather_sc")
    def gather(table_hbm, idx_hbm, out_hbm, idx_v, rows_v, sem):
        wid = lax.axis_index("s") * SC_CORES + lax.axis_index("c")
        base = wid * per_w

        def body(c, carry):
            off = pl.multiple_of(base + c * SC_GATHER_ROWS, SC_GATHER_ROWS)
            pltpu.sync_copy(idx_hbm.at[pl.ds(off, SC_GATHER_ROWS)], idx_v)
            pltpu.async_copy(table_hbm.at[idx_v], rows_v, sem).wait()
            pltpu.sync_copy(rows_v, out_hbm.at[pl.ds(off, SC_GATHER_ROWS)])
            return carry

        lax.fori_loop(0, n_chunks, body, 0)

    return gather(table, idx)


def _combine_kernel(x1_ref, gate_ref, yk_ref, wsg_ref, wsu_ref, wsd_ref, g2_ref, b2_ref, yp_ref, ysmp_ref,
                    *, n_prompt_tiles):
    i = pl.program_id(0)
    tm = x1_ref.shape[0]
    x1 = x1_ref[...]
    xb = x1.astype(BF16)
    g = _dot(xb, wsg_ref[...])
    u = _dot(xb, wsu_ref[...])
    ff = _dot((g * jax.nn.sigmoid(g) * u).astype(BF16), wsd_ref[...])
    gate = gate_ref[...]
    for kk in range(TOP_K):
        ff = ff + gate[:, kk:kk + 1] * _unpack_bf16_pairs(pltpu.bitcast(yk_ref[kk * tm:(kk + 1) * tm, :], U32))
    y = _layer_norm(ALPHA * x1 + ff, g2_ref[...], b2_ref[...])

    @pl.when(i < n_prompt_tiles)
    def _():
        yp_ref[0] = y

    @pl.when(i >= n_prompt_tiles)
    def _():
        ysmp_ref[...] = y


def _moe_combine(dest, x1, gate, ys, wsg, wsu, wsd, ln2_g, ln2_b, geo, seq):
    R = x1.shape[0]
    nt = R // ROW_TILE
    W = ys.shape[1]
    B, tpb, npt = geo["B"], geo["tpb"], geo["npt"]
    assert seq == (tpb - 1) * ROW_TILE
    order = jnp.transpose(dest.reshape(nt, ROW_TILE, TOP_K), (0, 2, 1)).reshape(R * TOP_K)
    yk = _sc_row_gather(lax.bitcast_convert_type(ys, I32), order)

    def rows(w):
        return pl.BlockSpec((ROW_TILE, w), lambda i: (i, 0))

    def full(a):
        return pl.BlockSpec(a.shape, lambda i: (0,) * a.ndim)

    def rows_k(w):
        return pl.BlockSpec((TOP_K * ROW_TILE, w), lambda i: (i, 0))

    def yp_map(i):
        in_prompt = i < npt
        return (jnp.minimum(i // tpb, B - 1), jnp.where(in_prompt, jnp.maximum(i % tpb - 1, 0), tpb - 2), 0)

    return pl.pallas_call(
        functools.partial(_combine_kernel, n_prompt_tiles=npt),
        grid=(nt,),
        in_specs=[rows(D_MODEL), rows(LANES), rows_k(W),
                  full(wsg), full(wsu), full(wsd), full(ln2_g), full(ln2_b)],
        out_specs=[pl.BlockSpec((1, ROW_TILE, D_MODEL), yp_map),
                   pl.BlockSpec((ROW_TILE, D_MODEL), lambda i: (jnp.maximum(i - npt, 0), 0))],
        out_shape=[jax.ShapeDtypeStruct((B, seq, D_MODEL), F32),
                   jax.ShapeDtypeStruct((R - npt * ROW_TILE, D_MODEL), F32)],
        compiler_params=_cparams("arbitrary"),
        name="moe_combine",
    )(x1, gate, yk, wsg, wsu, wsd, ln2_g, ln2_b)


def _routing_tables(idx8, rank8, counts, n_slots_blocks):
    R = idx8.shape[0]
    padded = (counts + MOE_BLOCK - 1) // MOE_BLOCK * MOE_BLOCK
    pends = jnp.cumsum(padded)
    pstarts = pends - padded
    experts = jnp.arange(N_EXPERTS, dtype=I32)
    start8 = jnp.sum(jnp.where(idx8[:, :, None] == experts[None, None, :], pstarts[None, None, :], 0), axis=-1)
    dest = start8 + rank8
    blk_start = jnp.arange(n_slots_blocks, dtype=I32) * MOE_BLOCK
    block_expert = jnp.minimum(jnp.sum((blk_start[:, None] >= pends[None, :]).astype(I32), axis=1), N_EXPERTS - 1)
    n_used = (pends[-1] // MOE_BLOCK).astype(I32).reshape(1)
    return dest.astype(I32), pends.astype(I32), padded.astype(I32), block_expert, n_used


def kernel(x_prompt, x_sample, cache_mla_ckv, cache_mla_krope, state_hgrn, page_table, meta_tokens, hg_lb_logits,
           w_in, q_norm, kv_norm, w_uq, w_uk, w_uv, hg_norm, w_br_mla, w_br_hg, w_out, ln1_g, ln1_b, router_w,
           router_bias, w_exp_gate, w_exp_up, w_exp_down, w_sh_gate, w_sh_up, w_sh_down, ln2_g, ln2_b):
    assert w_in.shape[0] == DEPTH
    B, seq, _ = x_prompt.shape
    DB, steps, _ = x_sample.shape
    n_meta = meta_tokens.shape[0]
    n_pages, page = page_table.shape[1], cache_mla_ckv.shape[2]
    past = n_pages * page
    T = n_meta + seq
    tp = -(-T // ROW_TILE) * ROW_TILE
    pad_front = tp - T
    tpb = tp // ROW_TILE
    n_s = DB * steps
    assert n_s % ROW_TILE == 0 and DB % HG_SAMPLE_BATCH == 0
    geo = dict(B=B, tp=tp, tpb=tpb, pad_front=pad_front, npt=B * tpb)
    Rp = B * tp
    R = Rp + n_s

    assert pad_front + n_meta == ROW_TILE and seq % ROW_TILE == 0
    x_head = jnp.concatenate([jnp.zeros((pad_front, D_MODEL), F32), meta_tokens.astype(F32)], axis=0)
    x_smp = x_sample.reshape(n_s, D_MODEL)

    pos_p = jnp.maximum(jnp.arange(tp) - pad_front, 0)
    pos = jnp.concatenate([jnp.tile(pos_p, B), jnp.tile(past + jnp.arange(steps), DB)]).astype(F32)
    half = MLA_ROPE // 2
    inv = ROPE_THETA ** (-jnp.arange(half, dtype=F32) / half)
    ang = pos[:, None] * inv[None, :]
    cos8 = jnp.tile(jnp.cos(ang), (1, LANES // half))
    sin8 = jnp.tile(jnp.sin(ang), (1, LANES // half))

    l = 0
    win = w_in[l]
    kr_end = MLA_Q_RANK + MLA_KV_RANK + MLA_ROPE
    w_in_p = jnp.concatenate([win[:, :kr_end], jnp.zeros((D_MODEL, LANES - MLA_ROPE), F32), win[:, kr_end:]],
                             axis=1).astype(BF16)
    assert w_in_p.shape[1] == D_IN_PACKED
    wq3 = w_uq[l].reshape(MLA_Q_RANK, MLA_HEADS, MLA_NOPE + MLA_ROPE)
    wqr = (jnp.concatenate([wq3[:, :, MLA_NOPE:MLA_NOPE + half].reshape(MLA_Q_RANK, MLA_HEADS * half),
                            wq3[:, :, MLA_NOPE + half:].reshape(MLA_Q_RANK, MLA_HEADS * half)], axis=1)
           * Q_SCALE).astype(BF16)
    src = jnp.arange(2 * LANES)
    hh, ii = (src % LANES) // half, src % half
    dst = hh * LANES + ii + jnp.where(src >= LANES, half, 0)
    perm = (dst[:, None] == jnp.arange(MLA_HEADS * LANES)[None, :]).astype(BF16)
    lb = jnp.cumsum(jax.nn.softmax(hg_lb_logits.astype(F32), axis=0), axis=0)[l].reshape(1, -1)

    wqlat, wo = _weight_prep(w_uq[l], w_uk[l], w_uv[l], w_br_mla[l])

    (qcat, kcat, ckv, kr, hq, hk, hv, lf, sg, sm, sh) = _in_projection(
        x_prompt, x_head, x_smp, w_in_p, q_norm[l].reshape(1, -1), kv_norm[l].reshape(1, -1), wqlat, wqr, perm,
        cos8, sin8, lb, geo)

    o_lat_p = _prompt_attention(qcat, kcat, geo)
    q_s = qcat[:, Rp:].reshape(MLA_HEADS, DB, steps, QCAT)
    knew_s = kcat[Rp:].reshape(DB, steps, 2 * LANES)
    cache_krt = jnp.swapaxes(cache_mla_krope[l], 1, 2)
    o_lat_s = _sample_attention(page_table, q_s, knew_s, cache_mla_ckv[l], cache_krt)
    o_lat_s = o_lat_s.reshape(n_s, MLA_HEADS * LANES)
    o_hg_p, s_prompt = _hgrn_prompt(hq, hk, hv, lf, geo)
    o_hg_s, s_sample = _hgrn_sample(hq, hk, hv, lf, state_hgrn[l], Rp, steps)

    rw_f = jnp.concatenate([router_w[l], jnp.zeros((D_MODEL, LANES - N_EXPERTS), F32)], axis=1)
    rw_hi = rw_f.astype(BF16)
    rw_p = jnp.stack([rw_hi, (rw_f - rw_hi.astype(F32)).astype(BF16)])
    rb_p = jnp.concatenate([router_bias[l].astype(F32), jnp.zeros((LANES - N_EXPERTS,), F32)]).reshape(1, -1)
    x1, x1p, idx, gate, rank, count = _layer_tail(
        x_prompt, x_head, x_smp, o_lat_p, o_lat_s, o_hg_p, o_hg_s, sg, sm, sh, wo, w_br_hg[l].astype(BF16),
        w_out[l].astype(BF16), hg_norm[l].reshape(1, -1), ln1_g[l].reshape(1, -1), ln1_b[l].reshape(1, -1),
        rw_p, rb_p, geo)

    nblk = R * TOP_K // MOE_BLOCK + N_EXPERTS
    dest, pends, padded, block_expert, n_used = _routing_tables(
        idx[:, :TOP_K], rank[:, :TOP_K], count[0, :N_EXPERTS], nblk)
    xs = _moe_dispatch(pends, padded, dest.reshape(R * TOP_K), x1p, nblk * MOE_BLOCK)
    ys = _moe_experts(block_expert, n_used, xs, w_exp_gate[l], w_exp_up[l], w_exp_down[l])
    yp, y_smp = _moe_combine(dest, x1, gate, ys, w_sh_gate[l].astype(BF16), w_sh_up[l].astype(BF16),
                             w_sh_down[l].astype(BF16), ln2_g[l].reshape(1, -1), ln2_b[l].reshape(1, -1), geo, seq)

    ys_out = y_smp.reshape(DB, steps, D_MODEL)
    ckv_p = ckv[:Rp].reshape(B, tp, MLA_KV_RANK)[:, pad_front:][None]
    kr_p = kr[:Rp].reshape(B, tp, LANES)[:, pad_front:, :MLA_ROPE][None]
    ckv_s = ckv[Rp:].reshape(DB, steps, MLA_KV_RANK)[None]
    kr_s = kr[Rp:, :MLA_ROPE].reshape(DB, steps, MLA_ROPE)[None]
    return (yp, ys_out, ckv_p, kr_p, s_prompt[None], ckv_s, kr_s, s_sample[None])
```

```python
import functools

import jax
import jax.numpy as jnp
from jax import lax
from jax.experimental import pallas as pl
from jax.experimental.pallas import tpu as pltpu
from jax.experimental.pallas import tpu_sc as plsc

F32 = jnp.float32
BF16 = jnp.bfloat16
U32 = jnp.uint32
I32 = jnp.int32

D_MODEL = 1024
MLA_HEADS = 8
MLA_Q_RANK = 256
MLA_KV_RANK = 128
MLA_NOPE = 64
MLA_ROPE = 32
MLA_V = 64
MLA_SCALE = (MLA_NOPE + MLA_ROPE) ** -0.5
LOG2E = 1.4426950408889634
Q_SCALE = MLA_SCALE * LOG2E
ROPE_THETA = 10000.0
HG_HEADS = 4
HG_DK = 128
HG_DV = 128
HG_CHUNK = 64
N_EXPERTS = 64
TOP_K = 8
D_EXPERT = 256
ROUTED_SCALE = 2.5
NORM_EPS = 1e-6
DEPTH = 1
ALPHA = (2.0 * DEPTH) ** 0.25

LANES = 128
ROW_TILE = 256
QCAT = 256
ATT_TK = 256
MOE_BLOCK = 512
PAGES_PER_CHUNK = 64
ONES_COL = LANES + MLA_ROPE
HG_SAMPLE_BATCH = 8
SATT_BATCH = 1
SC_CORES = 2
SC_SUBCORES = 16
SC_GATHER_ROWS = 64
NEG_BIG = -1e30
VMEM_LIMIT = 52 * 1024 * 1024

C_CQ = (0, 256)
C_CKV = (256, 384)
C_KR = (384, 512)
C_HQ = (512, 1024)
C_HF = (1024, 1536)
C_HI = (1536, 2048)
C_HG = (2048, 2560)
C_GM = (2560, 3584)
C_GH = (3584, 4608)
D_IN_PACKED = 4608


def _cparams(*sem):
    return pltpu.CompilerParams(dimension_semantics=sem, vmem_limit_bytes=VMEM_LIMIT)


def _dot(a, b):
    return jnp.dot(a, b, preferred_element_type=F32)


def _dot_nt(a, b):
    return lax.dot_general(a, b, (((1,), (1,)), ((), ())), preferred_element_type=F32)


def _dot_tn(a, b):
    return lax.dot_general(a, b, (((0,), (0,)), ((), ())), preferred_element_type=F32)


def _pack_bf16_pairs(x):
    w = x.shape[1] // 2
    bits = pltpu.bitcast(x.astype(BF16).astype(F32), U32)
    return bits[:, w:] | (bits[:, :w] >> 16)


def _unpack_bf16_pairs(p):
    lo = pltpu.bitcast(p << 16, F32)
    hi = pltpu.bitcast(p & jnp.uint32(0xFFFF0000), F32)
    return jnp.concatenate([lo, hi], axis=1)


def _x_specs(geo, seq_tiles):
    B, tpb, npt = geo["B"], geo["tpb"], geo["npt"]

    def xp_map(i):
        return (jnp.minimum(i // tpb, B - 1), jnp.where(i < npt, jnp.maximum(i % tpb - 1, 0), seq_tiles - 1), 0)

    return [pl.BlockSpec((1, ROW_TILE, D_MODEL), xp_map),
            pl.BlockSpec((ROW_TILE, D_MODEL), lambda i: (0, 0)),
            pl.BlockSpec((ROW_TILE, D_MODEL), lambda i: (jnp.maximum(i - npt, 0), 0))]


def _select_x(i, xp_ref, xh_ref, xs_ref, tiles_per_batch, n_prompt_tiles):
    is_head = (i < n_prompt_tiles) & (i % tiles_per_batch == 0)
    return jnp.where(i >= n_prompt_tiles, xs_ref[...], jnp.where(is_head, xh_ref[...], xp_ref[0]))


def _split_specs(width, npt):
    return [pl.BlockSpec((ROW_TILE, width), lambda i: (jnp.minimum(i, npt - 1), 0)),
            pl.BlockSpec((ROW_TILE, width), lambda i: (jnp.maximum(i - npt, 0), 0))]


def _wprep_kernel(wq_nope_ref, wuk_ref, wuv_ref, wbr_ref, wqlat_ref, wo_ref):
    hp = lax.Precision.HIGHEST
    for h in range(MLA_HEADS):
        a = wq_nope_ref[h]
        b = wuk_ref[h]
        ql = lax.dot_general(a, b, (((1,), (1,)), ((), ())), precision=hp, preferred_element_type=F32)
        wqlat_ref[:, h * MLA_KV_RANK:(h + 1) * MLA_KV_RANK] = (ql * Q_SCALE).astype(BF16)
        c = wuv_ref[h]
        d = wbr_ref[h]
        wo_ref[h * MLA_KV_RANK:(h + 1) * MLA_KV_RANK, :] = jnp.dot(
            c, d, precision=hp, preferred_element_type=F32).astype(BF16)


def _weight_prep(w_uq, w_uk, w_uv, w_br_mla):
    wq3 = w_uq.reshape(MLA_Q_RANK, MLA_HEADS, MLA_NOPE + MLA_ROPE)
    wq_nope = jnp.transpose(wq3[:, :, :MLA_NOPE], (1, 0, 2))
    wuk = jnp.transpose(w_uk, (1, 0, 2))
    wuv = jnp.transpose(w_uv, (1, 0, 2))
    wbr = w_br_mla.reshape(MLA_HEADS, MLA_V, D_MODEL)
    return pl.pallas_call(
        _wprep_kernel,
        out_shape=(jax.ShapeDtypeStruct((MLA_Q_RANK, MLA_HEADS * MLA_KV_RANK), BF16),
                   jax.ShapeDtypeStruct((MLA_HEADS * MLA_KV_RANK, D_MODEL), BF16)),
        compiler_params=pltpu.CompilerParams(vmem_limit_bytes=VMEM_LIMIT),
        name="weight_prep",
    )(wq_nope, wuk, wuv, wbr)


def _inproj_kernel(xp_ref, xh_ref, xs_ref, w_ref, qn_ref, kvn_ref, wqlat_ref, wqr_ref, perm_ref, cos_ref, sin_ref,
                   lb_ref, qcat_ref, kcat_ref, ckv_ref, kr_ref, hq_ref, hk_ref, hv_ref, lf_ref, sg_ref, sm_ref,
                   sh_ref, *, pad_front, tiles_per_batch, n_prompt_tiles):
    i = pl.program_id(0)
    tm = xh_ref.shape[0]
    xb = _select_x(i, xp_ref, xh_ref, xs_ref, tiles_per_batch, n_prompt_tiles).astype(BF16)

    def proj(c):
        return _dot(xb, w_ref[:, c[0]:c[1]])

    cos8 = cos_ref[...]
    sin8 = sin_ref[...]

    cq = proj(C_CQ)
    cqn = cq * lax.rsqrt(jnp.mean(cq * cq, axis=-1, keepdims=True) + NORM_EPS) * qn_ref[...]
    cqb = cqn.astype(BF16)
    qlat = _dot(cqb, wqlat_ref[...])
    qr = _dot(cqb, wqr_ref[...])
    x1, x2 = qr[:, :LANES], qr[:, LANES:]
    qrot = jnp.concatenate([x1 * cos8 - x2 * sin8, x2 * cos8 + x1 * sin8], axis=1).astype(BF16)
    qrh = _dot(qrot, perm_ref[...])
    for h in range(MLA_HEADS):
        qcat_ref[h, :, :LANES] = qlat[:, h * LANES:(h + 1) * LANES].astype(BF16)
        qcat_ref[h, :, LANES:] = qrh[:, h * LANES:(h + 1) * LANES].astype(BF16)

    kv = proj(C_CKV)
    ckv = kv * lax.rsqrt(jnp.mean(kv * kv, axis=-1, keepdims=True) + NORM_EPS) * kvn_ref[...]
    ckv_ref[...] = ckv
    kcat_ref[:, :LANES] = ckv.astype(BF16)
    krr = proj(C_KR)
    lane = lax.broadcasted_iota(I32, (tm, LANES), 1)
    half = MLA_ROPE // 2
    rot = jnp.where(lane < half, -pltpu.roll(krr, LANES - half, 1), pltpu.roll(krr, half, 1))
    kr = jnp.where(lane < MLA_ROPE, krr * cos8 + rot * sin8, 0.0)
    kr_ref[...] = kr
    kcat_ref[:, LANES:] = jnp.where(lane == ONES_COL - LANES, 1.0, kr).astype(BF16)

    row = lax.broadcasted_iota(I32, (tm, 1), 0)
    is_pad = (i < n_prompt_tiles) & (i % tiles_per_batch == 0) & (row < pad_front)
    keep = jnp.where(is_pad, 0.0, 1.0)
    hq_ref[...] = proj(C_HQ).astype(BF16)
    lb = lb_ref[...]
    f = lb + (1.0 - lb) * jax.nn.sigmoid(proj(C_HF))
    lf_ref[...] = jnp.log(f) * keep
    hk_ref[...] = ((1.0 - f) * keep).astype(BF16)
    hv_ref[...] = proj(C_HI).astype(BF16)

    g = proj(C_HG)
    sg_ref[...] = (g * jax.nn.sigmoid(g)).astype(BF16)
    sm_ref[...] = jax.nn.sigmoid(proj(C_GM)).astype(BF16)
    sh_ref[...] = jax.nn.sigmoid(proj(C_GH)).astype(BF16)


def _in_projection(x_prompt, x_head, x_smp, w_in_p, q_norm, kv_norm, wqlat, wqr, perm, cos8, sin8, lb, geo):
    R = cos8.shape[0]
    nt = R // ROW_TILE
    hgw = HG_HEADS * HG_DK
    x_specs = _x_specs(geo, x_prompt.shape[1] // ROW_TILE)

    def rows(w):
        return pl.BlockSpec((ROW_TILE, w), lambda i: (i, 0))

    def full(a):
        return pl.BlockSpec(a.shape, lambda i: (0,) * a.ndim)

    out_widths = [(2 * LANES, BF16), (LANES, F32), (LANES, F32),
                  (hgw, BF16), (hgw, BF16), (hgw, BF16), (hgw, F32), (hgw, BF16),
                  (D_MODEL, BF16), (D_MODEL, BF16)]
    qcat_spec = pl.BlockSpec((MLA_HEADS, ROW_TILE, QCAT), lambda i: (0, i, 0))
    return pl.pallas_call(
        functools.partial(_inproj_kernel, pad_front=geo["pad_front"], tiles_per_batch=geo["tpb"],
                          n_prompt_tiles=geo["npt"]),
        grid=(nt,),
        in_specs=x_specs + [full(w_in_p), full(q_norm), full(kv_norm), full(wqlat), full(wqr), full(perm),
                            rows(LANES), rows(LANES), full(lb)],
        out_specs=[qcat_spec] + [rows(w) for w, _ in out_widths],
        out_shape=[jax.ShapeDtypeStruct((MLA_HEADS, R, QCAT), BF16)]
        + [jax.ShapeDtypeStruct((R, w), dt) for w, dt in out_widths],
        compiler_params=_cparams("arbitrary"),
        name="in_projection",
    )(x_prompt, x_head, x_smp, w_in_p, q_norm, kv_norm, wqlat, wqr, perm, cos8, sin8, lb)


def _softmax_step(s, v_b, m_ref, l_ref, acc_ref):
    n = s.shape[1] // LANES
    m_prev = m_ref[...]
    m_next = jnp.maximum(m_prev, jnp.max(s, axis=1, keepdims=True))
    p = jnp.concatenate([jnp.exp2(s[:, j * LANES:(j + 1) * LANES] - m_next) for j in range(n)], axis=1)
    alpha = jnp.exp2(m_prev - m_next)
    pv = _dot(p.astype(BF16), v_b)
    if v_b.shape[1] == LANES:
        l_ref[...] = alpha * l_ref[...] + jnp.sum(p, axis=1, keepdims=True)
    else:
        l_ref[...] = alpha * l_ref[...] + pv[:, LANES:]
    acc_ref[...] = alpha * acc_ref[...] + pv[:, :LANES]
    m_ref[...] = m_next


def _pattn_kernel(q_ref, k_ref, o_ref, m_ref, l_ref, acc_ref, *, pad_front):
    qi = pl.program_id(1)
    nh, tq, _ = q_ref.shape
    tk = ATT_TK
    rows = nh * tq

    q = q_ref[...].reshape(rows, QCAT)
    m_ref[...] = jnp.full(m_ref.shape, -jnp.inf, F32)
    l_ref[...] = jnp.zeros(l_ref.shape, F32)
    acc_ref[...] = jnp.zeros(acc_ref.shape, F32)

    def step(kb, masked):
        kblk = k_ref[pl.ds(pl.multiple_of(kb * tk, tk), tk), :]
        s = _dot_nt(q, kblk)
        if masked:
            qrow = qi * tq + lax.broadcasted_iota(I32, (rows, tk), 0) % tq
            krow = kb * tk + lax.broadcasted_iota(I32, (rows, tk), 1)
            s = jnp.where((krow <= qrow) & (krow >= pad_front), s, NEG_BIG)
        _softmax_step(s, kblk, m_ref, l_ref, acc_ref)

    step(0, True)

    def body(kb, c):
        step(kb, False)
        return c

    lax.fori_loop(1, qi, body, 0)

    @pl.when(qi > 0)
    def _():
        step(qi, True)

    row_sum = l_ref[:, ONES_COL - LANES:ONES_COL - LANES + 1]
    o = (acc_ref[...] / row_sum).astype(BF16)
    for h in range(nh):
        o_ref[:, h * LANES:(h + 1) * LANES] = o[h * tq:(h + 1) * tq]


def _prompt_attention(qcat, kcat, geo):
    B, tpb, tp = geo["B"], geo["tpb"], geo["tp"]
    rows = MLA_HEADS * ROW_TILE
    return pl.pallas_call(
        functools.partial(_pattn_kernel, pad_front=geo["pad_front"]),
        grid=(B, tpb),
        in_specs=[pl.BlockSpec((MLA_HEADS, ROW_TILE, QCAT), lambda b, i: (0, b * tpb + i, 0)),
                  pl.BlockSpec((tp, 2 * LANES), lambda b, i: (b, 0))],
        out_specs=pl.BlockSpec((ROW_TILE, MLA_HEADS * LANES), lambda b, i: (b * tpb + i, 0)),
        out_shape=jax.ShapeDtypeStruct((B * tp, MLA_HEADS * LANES), BF16),
        scratch_shapes=[pltpu.VMEM((rows, LANES), F32)] * 3,
        compiler_params=_cparams("arbitrary", "arbitrary"),
        name="prompt_attention",
    )(qcat, kcat)


def _sattn_kernel(pt_ref, q_ref, knew_ref, ckv_hbm, kr_hbm, o_ref,
                  ckv_buf, kr_buf, sem, m_ref, l_ref, acc_ref, *, n_chunks, page):
    g = pl.program_id(0)
    ng = pl.num_programs(0)
    ch = PAGES_PER_CHUNK
    nh, nbt, S = q_ref.shape[0], q_ref.shape[1], q_ref.shape[2]
    rows = nh * S

    def chunk_copies(gg, c, slot):
        cps = []
        for bb in range(nbt):
            for j in range(ch):
                pg = pt_ref[gg * nbt + bb, c * ch + j]
                cps.append(pltpu.make_async_copy(ckv_hbm.at[pg], ckv_buf.at[slot, bb, pl.ds(j * page, page), :],
                                                 sem.at[0, slot]))
                cps.append(pltpu.make_async_copy(kr_hbm.at[pg], kr_buf.at[slot, bb, :, pl.ds(j * page, page)],
                                                 sem.at[1, slot]))
        return cps

    @pl.when(g == 0)
    def _():
        for cp in chunk_copies(0, 0, 0):
            cp.start()

    qlat, qrope = [], []
    for bb in range(nbt):
        q = q_ref[:, bb].reshape(rows, QCAT)
        qlat.append(q[:, :LANES])
        qrope.append(q[:, LANES:LANES + MLA_ROPE])
        knew = knew_ref[bb]
        s_new = _dot_nt(q, knew)
        qtok = lax.broadcasted_iota(I32, (rows, S), 0) % S
        ktok = lax.broadcasted_iota(I32, (rows, S), 1)
        s_new = jnp.where(ktok <= qtok, s_new, NEG_BIG)
        m0 = jnp.max(s_new, axis=1, keepdims=True)
        p0 = jnp.exp2(s_new - m0)
        m_ref[bb] = jnp.broadcast_to(m0, (rows, LANES))
        l_ref[bb] = jnp.broadcast_to(jnp.sum(p0, axis=1, keepdims=True), (rows, LANES))
        acc_ref[bb] = _dot(p0.astype(BF16), knew[:, :LANES])

    for c in range(n_chunks):
        slot = c % 2 if n_chunks % 2 == 0 else (g * n_chunks + c) % 2
        if c + 1 < n_chunks:
            for cp in chunk_copies(g, c + 1, 1 - slot):
                cp.start()
        else:
            @pl.when(g + 1 < ng)
            def _():
                for cp in chunk_copies(g + 1, 0, 1 - slot):
                    cp.start()
        for cp in chunk_copies(g, c, slot):
            cp.wait()
        for bb in range(nbt):
            ckv_b = ckv_buf[slot, bb].astype(BF16)
            krt_b = kr_buf[slot, bb].astype(BF16)
            s = _dot_nt(qlat[bb], ckv_b) + _dot(qrope[bb], krt_b)
            _softmax_step(s, ckv_b, m_ref.at[bb], l_ref.at[bb], acc_ref.at[bb])

    for bb in range(nbt):
        o = (acc_ref[bb] / l_ref[bb]).astype(BF16)
        for h in range(nh):
            o_ref[bb, :, h * LANES:(h + 1) * LANES] = o[h * S:(h + 1) * S]


def _sample_attention(page_table, q_s, knew_s, cache_ckv, cache_krt):
    nh, DB, S, _ = q_s.shape
    rows = nh * S
    n_pages = page_table.shape[1]
    page = cache_ckv.shape[1]
    assert n_pages % PAGES_PER_CHUNK == 0
    n_chunks = n_pages // PAGES_PER_CHUNK
    ck = PAGES_PER_CHUNK * page
    nbt = SATT_BATCH
    assert DB % nbt == 0
    grid_spec = pltpu.PrefetchScalarGridSpec(
        num_scalar_prefetch=1,
        grid=(DB // nbt,),
        in_specs=[pl.BlockSpec((nh, nbt, S, QCAT), lambda g, pt: (0, g, 0, 0)),
                  pl.BlockSpec((nbt, S, 2 * LANES), lambda g, pt: (g, 0, 0)),
                  pl.BlockSpec(memory_space=pl.ANY),
                  pl.BlockSpec(memory_space=pl.ANY)],
        out_specs=pl.BlockSpec((nbt, S, nh * LANES), lambda g, pt: (g, 0, 0)),
        scratch_shapes=[pltpu.VMEM((2, nbt, ck, MLA_KV_RANK), F32),
                        pltpu.VMEM((2, nbt, MLA_ROPE, ck), F32),
                        pltpu.SemaphoreType.DMA((2, 2)),
                        pltpu.VMEM((nbt, rows, LANES), F32),
                        pltpu.VMEM((nbt, rows, LANES), F32),
                        pltpu.VMEM((nbt, rows, LANES), F32)])
    return pl.pallas_call(
        functools.partial(_sattn_kernel, n_chunks=n_chunks, page=page),
        grid_spec=grid_spec,
        out_shape=jax.ShapeDtypeStruct((DB, S, nh * LANES), BF16),
        compiler_params=_cparams("arbitrary"),
        name="sample_attention",
    )(page_table, q_s, knew_s, cache_ckv, cache_krt)


def _split3(x):
    hi = x.astype(BF16)
    r1 = x - hi.astype(F32)
    mid = r1.astype(BF16)
    lo = (r1 - mid.astype(F32)).astype(BF16)
    return hi, mid, lo


def _hgrn_chunk(q, k, v, lf, S0, tri, group):
    C = q.shape[0]
    hi, mid, lo = _split3(lf)
    cum = _dot(tri, hi) + _dot(tri, mid) + _dot(tri, lo)
    vb = v.astype(BF16)
    rowi = lax.broadcasted_iota(I32, (C, C), 0)
    coli = lax.broadcasted_iota(I32, (C, C), 1)
    attn = jnp.zeros((C, C), F32)

    bd = min(16, group)
    hs = group // 2
    rid = lax.broadcasted_iota(I32, (C, 1), 0)
    while hs >= bd:
        npair = C // (2 * hs)
        ref = jnp.concatenate(
            [jnp.broadcast_to(cum[(2 * j + 1) * hs - 1:(2 * j + 1) * hs, :], (2 * hs, cum.shape[1]))
             for j in range(npair)], axis=0)
        odd = ((rid // hs) % 2) == 1
        e = jnp.exp(jnp.where(odd, cum - ref, ref - cum))
        qs = jnp.where(odd, q * e, 0.0).astype(BF16)
        ks = jnp.where(odd, 0.0, k * e).astype(BF16)
        a = _dot_nt(qs, ks)
        attn = attn + jnp.where((rowi // (2 * hs)) == (coli // (2 * hs)), a, 0.0)
        hs //= 2

    nblk = C // bd
    k3 = k.reshape(nblk, bd, k.shape[1])
    c3 = cum.reshape(nblk, bd, cum.shape[1])
    tl = rid % bd
    blk_base = (rowi // bd) * bd
    for sl in range(bd):
        ks_b = jnp.broadcast_to(k3[:, sl:sl + 1, :], k3.shape).reshape(C, k.shape[1])
        cs_b = jnp.broadcast_to(c3[:, sl:sl + 1, :], c3.shape).reshape(C, k.shape[1])
        e = jnp.exp(jnp.where(tl >= sl, cum - cs_b, NEG_BIG))
        col = jnp.sum(q * ks_b * e, axis=1, keepdims=True)
        attn = jnp.where(coli == blk_base + sl, col, attn)
    o_intra = _dot(attn.astype(BF16), vb)
    return o_intra, cum


def _tri(C, group):
    r = lax.broadcasted_iota(I32, (C, C), 0)
    c = lax.broadcasted_iota(I32, (C, C), 1)
    return jnp.where((c <= r) & (r // group == c // group), 1.0, 0.0).astype(BF16)


def _state_update(q, k, v, cum, S, o_intra):
    C = q.shape[0]
    last = cum[C - 1:C, :]
    o = o_intra + _dot((q * jnp.exp(cum)).astype(BF16), S.astype(BF16))
    kst = (k * jnp.exp(last - cum)).astype(BF16)
    dfull = jnp.transpose(jnp.broadcast_to(jnp.exp(last), (S.shape[1], S.shape[0])))
    S_new = dfull * S + _dot_tn(kst, v.astype(BF16))
    return o, S_new


def _hgrn_prompt_kernel(q_ref, k_ref, v_ref, lf_ref, o_ref, s_out_ref, s_ref):
    t = pl.program_id(1)

    @pl.when(t == 0)
    def _():
        s_ref[...] = jnp.zeros(s_ref.shape, F32)

    C = HG_CHUNK
    tri = _tri(C, C)
    states = [s_ref[h] for h in range(HG_HEADS)]
    for c in range(q_ref.shape[0] // C):
        sl = slice(c * C, (c + 1) * C)
        for h in range(HG_HEADS):
            hl = slice(h * HG_DK, (h + 1) * HG_DK)
            q = q_ref[sl, hl].astype(F32)
            k = k_ref[sl, hl].astype(F32)
            v = v_ref[sl, hl].astype(F32)
            lf = lf_ref[sl, hl]
            o_intra, cum = _hgrn_chunk(q, k, v, lf, None, tri, C)
            o, states[h] = _state_update(q, k, v, cum, states[h], o_intra)
            o_ref[sl, hl] = o.astype(BF16)
    for h in range(HG_HEADS):
        s_ref[h] = states[h]

    @pl.when(t == pl.num_programs(1) - 1)
    def _():
        s_out_ref[0] = s_ref[...]


def _hgrn_prompt(hq, hk, hv, lf, geo):
    B, tpb, tp = geo["B"], geo["tpb"], geo["tp"]
    hgw = HG_HEADS * HG_DK

    def blk():
        return pl.BlockSpec((ROW_TILE, hgw), lambda b, t: (b * tpb + t, 0))

    return pl.pallas_call(
        _hgrn_prompt_kernel,
        grid=(B, tpb),
        in_specs=[blk(), blk(), blk(), blk()],
        out_specs=[blk(), pl.BlockSpec((1, HG_HEADS, HG_DK, HG_DV), lambda b, t: (b, 0, 0, 0))],
        out_shape=[jax.ShapeDtypeStruct((B * tp, hgw), BF16),
                   jax.ShapeDtypeStruct((B, HG_HEADS, HG_DK, HG_DV), F32)],
        scratch_shapes=[pltpu.VMEM((HG_HEADS, HG_DK, HG_DV), F32)],
        compiler_params=_cparams("arbitrary", "arbitrary"),
        name="hgrn_prompt",
    )(hq, hk, hv, lf)


def _hgrn_sample_kernel(q_ref, k_ref, v_ref, lf_ref, s_in_ref, o_ref, s_out_ref, *, steps):
    nb = s_in_ref.shape[0]
    C = nb * steps
    tri = _tri(C, steps)
    for h in range(HG_HEADS):
        sl = slice(h * HG_DK, (h + 1) * HG_DK)
        q = q_ref[:, sl].astype(F32)
        k = k_ref[:, sl].astype(F32)
        v = v_ref[:, sl].astype(F32)
        lf = lf_ref[:, sl]
        o_intra, cum = _hgrn_chunk(q, k, v, lf, None, tri, steps)
        for b in range(nb):
            r = slice(b * steps, (b + 1) * steps)
            o, S_new = _state_update(q[r], k[r], v[r], cum[r], s_in_ref[b, h], o_intra[r])
            o_ref[r, sl] = o.astype(BF16)
            s_out_ref[b, h] = S_new


def _hgrn_sample(hq, hk, hv, lf, state, row0, steps):
    DB = state.shape[0]
    nb = HG_SAMPLE_BATCH
    rows = nb * steps
    hgw = HG_HEADS * HG_DK
    blk0 = row0 // rows

    def tok():
        return pl.BlockSpec((rows, hgw), lambda i: (blk0 + i, 0))

    st = pl.BlockSpec((nb, HG_HEADS, HG_DK, HG_DV), lambda i: (i, 0, 0, 0))
    return pl.pallas_call(
        functools.partial(_hgrn_sample_kernel, steps=steps),
        grid=(DB // nb,),
        in_specs=[tok(), tok(), tok(), tok(), st],
        out_specs=[pl.BlockSpec((rows, hgw), lambda i: (i, 0)), st],
        out_shape=[jax.ShapeDtypeStruct((DB * steps, hgw), BF16),
                   jax.ShapeDtypeStruct(state.shape, F32)],
        compiler_params=_cparams("arbitrary"),
        name="hgrn_sample",
    )(hq, hk, hv, lf, state)


def _layer_norm(x, g, b):
    mu = jnp.mean(x, axis=-1, keepdims=True)
    xc = x - mu
    var = jnp.mean(xc * xc, axis=-1, keepdims=True)
    return xc * lax.rsqrt(var + NORM_EPS) * g + b


def _tail_kernel(xp_ref, xh_ref, xs_ref, olp_ref, ols_ref, ohp_ref, ohs_ref, sg_ref, sm_ref, sh_ref, wo_ref, wbh_ref,
                 wout_ref, hgn_ref, g1_ref, b1_ref, rw_ref, rb_ref, x1_ref, x1p_ref, idx_ref, gate_ref, rank_ref,
                 count_ref, cnt_ref, *, tiles_per_batch, n_prompt_tiles):
    i = pl.program_id(0)
    tm = xh_ref.shape[0]
    x_in = _select_x(i, xp_ref, xh_ref, xs_ref, tiles_per_batch, n_prompt_tiles)
    is_sample = i >= n_prompt_tiles
    mla = _dot(jnp.where(is_sample, ols_ref[...], olp_ref[...]), wo_ref[...])
    oh = jnp.where(is_sample, ohs_ref[...], ohp_ref[...]).astype(F32)
    parts = []
    for h in range(HG_HEADS):
        y = oh[:, h * HG_DV:(h + 1) * HG_DV]
        parts.append(y * lax.rsqrt(jnp.mean(y * y, axis=-1, keepdims=True) + NORM_EPS) * hgn_ref[...])
    hg = (jnp.concatenate(parts, axis=1) * sg_ref[...].astype(F32)).astype(BF16)
    merged = sm_ref[...].astype(F32) * mla + sh_ref[...].astype(F32) * _dot(hg, wbh_ref[...])
    x1 = _layer_norm(ALPHA * x_in + _dot(merged.astype(BF16), wout_ref[...]), g1_ref[...], b1_ref[...])
    x1_ref[...] = x1
    x1p_ref[...] = _pack_bf16_pairs(x1)

    x1_hi = x1.astype(BF16)
    x1_lo = (x1 - x1_hi.astype(F32)).astype(BF16)
    logits = _dot(x1_hi, rw_ref[0]) + _dot(x1_lo, rw_ref[0]) + _dot(x1_hi, rw_ref[1])
    scores = jax.nn.sigmoid(logits)
    lane = lax.broadcasted_iota(I32, (tm, LANES), 1).astype(F32)
    remaining = jnp.where(lane < N_EXPERTS, scores + rb_ref[...], -jnp.inf)
    idx_out = jnp.zeros((tm, LANES), F32)
    gate_out = jnp.zeros((tm, LANES), F32)
    hits = []
    for kk in range(TOP_K):
        mx = jnp.max(remaining, axis=1, keepdims=True)
        pick = jnp.min(jnp.where(remaining == mx, lane, float(LANES)), axis=1, keepdims=True)
        hit = lane == pick
        hits.append(hit)
        gval = jnp.sum(jnp.where(hit, scores, 0.0), axis=1, keepdims=True)
        idx_out = jnp.where(lane == kk, pick, idx_out)
        gate_out = jnp.where(lane == kk, gval, gate_out)
        remaining = jnp.where(hit, -jnp.inf, remaining)
    gate_out = gate_out / jnp.sum(gate_out, axis=1, keepdims=True) * ROUTED_SCALE
    idx_ref[...] = idx_out.astype(I32)
    gate_ref[...] = gate_out

    @pl.when(pl.program_id(0) == 0)
    def _():
        cnt_ref[...] = jnp.zeros(cnt_ref.shape, F32)

    sel = jnp.where(remaining == -jnp.inf, 1.0, 0.0) * jnp.where(lane < N_EXPERTS, 1.0, 0.0)
    r_i = lax.broadcasted_iota(I32, (tm, tm), 0)
    c_i = lax.broadcasted_iota(I32, (tm, tm), 1)
    before = _dot(jnp.where(c_i < r_i, 1.0, 0.0).astype(BF16), sel.astype(BF16)) + cnt_ref[0:1, :]
    rank_out = jnp.zeros((tm, LANES), F32)
    for kk in range(TOP_K):
        rk = jnp.sum(jnp.where(hits[kk], before, 0.0), axis=1, keepdims=True)
        rank_out = jnp.where(lane == kk, rk, rank_out)
    rank_ref[...] = rank_out.astype(I32)
    total = cnt_ref[0:1, :] + jnp.sum(sel, axis=0, keepdims=True)
    cnt_ref[...] = jnp.broadcast_to(total, cnt_ref.shape)
    count_ref[...] = jnp.broadcast_to(total, count_ref.shape).astype(I32)


def _layer_tail(x_prompt, x_head, x_smp, o_lat_p, o_lat_s, o_hg_p, o_hg_s, sg, sm, sh, wo, w_br_hg, w_out, hg_norm,
                ln1_g, ln1_b, rw_p, rb_p, geo):
    R = sg.shape[0]
    npt = geo["npt"]

    def rows(w):
        return pl.BlockSpec((ROW_TILE, w), lambda i: (i, 0))

    def full(a):
        return pl.BlockSpec(a.shape, lambda i: (0,) * a.ndim)

    hgw = HG_HEADS * HG_DV
    return pl.pallas_call(
        functools.partial(_tail_kernel, tiles_per_batch=geo["tpb"], n_prompt_tiles=npt),
        grid=(R // ROW_TILE,),
        in_specs=_x_specs(geo, x_prompt.shape[1] // ROW_TILE)
        + _split_specs(MLA_HEADS * LANES, npt) + _split_specs(hgw, npt)
        + [rows(hgw), rows(D_MODEL), rows(D_MODEL),
           full(wo), full(w_br_hg), full(w_out), full(hg_norm), full(ln1_g), full(ln1_b),
           full(rw_p), full(rb_p)],
        out_specs=[rows(D_MODEL), rows(D_MODEL // 2), rows(LANES), rows(LANES), rows(LANES),
                   pl.BlockSpec((8, LANES), lambda i: (0, 0))],
        out_shape=[jax.ShapeDtypeStruct((R, D_MODEL), F32), jax.ShapeDtypeStruct((R, D_MODEL // 2), U32),
                   jax.ShapeDtypeStruct((R, LANES), I32), jax.ShapeDtypeStruct((R, LANES), F32),
                   jax.ShapeDtypeStruct((R, LANES), I32), jax.ShapeDtypeStruct((8, LANES), I32)],
        scratch_shapes=[pltpu.VMEM((8, LANES), F32)],
        compiler_params=_cparams("arbitrary"),
        name="layer_tail",
    )(x_prompt, x_head, x_smp, o_lat_p, o_lat_s, o_hg_p, o_hg_s, sg, sm, sh, wo, w_br_hg, w_out, hg_norm,
      ln1_g, ln1_b, rw_p, rb_p)


def _dispatch_kernel(pends_ref, padded_ref, dest_ref, x_ref, xs_hbm, zero_buf, stage, sem, zsem):
    i = pl.program_id(0)
    tm = x_ref.shape[0]

    @pl.when(i == 0)
    def _():
        zero_buf[...] = jnp.zeros(zero_buf.shape, U32)

        def zcopy(e):
            start = pl.multiple_of(pends_ref[e] - MOE_BLOCK, MOE_BLOCK)
            return pltpu.make_async_copy(zero_buf, xs_hbm.at[pl.ds(start, MOE_BLOCK), :], zsem)

        n_slots = xs_hbm.shape[0]

        def tcopy(j):
            start = pl.multiple_of(pends_ref[N_EXPERTS - 1] + j * MOE_BLOCK, MOE_BLOCK)
            return pltpu.make_async_copy(zero_buf, xs_hbm.at[pl.ds(start, MOE_BLOCK), :], zsem)

        def tail_live(j):
            return pends_ref[N_EXPERTS - 1] + (j + 1) * MOE_BLOCK <= n_slots

        for e in range(N_EXPERTS):
            @pl.when(padded_ref[e] > 0)
            def _():
                zcopy(e).start()

            @pl.when(tail_live(e))
            def _():
                tcopy(e).start()
        for e in range(N_EXPERTS):
            @pl.when(padded_ref[e] > 0)
            def _():
                zcopy(e).wait()

            @pl.when(tail_live(e))
            def _():
                tcopy(e).wait()

    nt = pl.num_programs(0)
    slot = i % 2

    def wait_all(s):
        for kk in range(TOP_K):
            pltpu.make_async_copy(stage.at[s], xs_hbm.at[pl.ds(0, tm), :], sem.at[s]).wait()

    @pl.when(i >= 2)
    def _():
        wait_all(slot)

    stage[slot] = x_ref[...]
    for r in range(tm):
        for kk in range(TOP_K):
            d = dest_ref[r * TOP_K + kk]
            pltpu.make_async_copy(stage.at[slot, pl.ds(r, 1), :], xs_hbm.at[pl.ds(d, 1), :],
                                  sem.at[slot]).start(priority=kk % 2)

    @pl.when(i == nt - 1)
    def _():
        wait_all(slot)

    @pl.when((i == nt - 1) & (i >= 1))
    def _():
        wait_all(1 - slot)


def _moe_dispatch(pends, padded, dest_flat, x1p, n_slots):
    R, W = x1p.shape
    grid_spec = pltpu.PrefetchScalarGridSpec(
        num_scalar_prefetch=2,
        grid=(R // ROW_TILE,),
        in_specs=[pl.BlockSpec((ROW_TILE * TOP_K,), lambda i, pe, pa: (i,), memory_space=pltpu.SMEM),
                  pl.BlockSpec((ROW_TILE, W), lambda i, pe, pa: (i, 0))],
        out_specs=pl.BlockSpec(memory_space=pl.ANY),
        scratch_shapes=[pltpu.VMEM((MOE_BLOCK, W), U32), pltpu.VMEM((2, ROW_TILE, W), U32),
                        pltpu.SemaphoreType.DMA((2,)), pltpu.SemaphoreType.DMA])
    return pl.pallas_call(
        _dispatch_kernel,
        grid_spec=grid_spec,
        out_shape=jax.ShapeDtypeStruct((n_slots, W), U32),
        compiler_params=_cparams("arbitrary"),
        name="moe_dispatch",
    )(pends, padded, dest_flat, x1p)


def _expert_kernel(be_ref, nused_ref, xs_ref, wg_ref, wu_ref, wd_ref, ys_ref, wgb_ref, wub_ref, wdb_ref):
    i = pl.program_id(0)

    @pl.when((i == 0) | (be_ref[i] != be_ref[jnp.maximum(i - 1, 0)]))
    def _():
        wgb_ref[...] = wg_ref[0].astype(BF16)
        wub_ref[...] = wu_ref[0].astype(BF16)
        wdb_ref[...] = wd_ref[0].astype(BF16)

    @pl.when(i < nused_ref[0])
    def _():
        xb = _unpack_bf16_pairs(xs_ref[...]).astype(BF16)
        g = _dot(xb, wgb_ref[...])
        u = _dot(xb, wub_ref[...])
        hmid = (g * jax.nn.sigmoid(g) * u).astype(BF16)
        ys_ref[...] = pltpu.bitcast(_pack_bf16_pairs(_dot(hmid, wdb_ref[...])), I32)

    @pl.when(i >= nused_ref[0])
    def _():
        ys_ref[...] = jnp.zeros(ys_ref.shape, I32)


def _moe_experts(block_expert, n_used, xs, wg, wu, wd):
    n_slots, W = xs.shape
    nblk = n_slots // MOE_BLOCK

    def xmap(i, be, nu):
        return (jnp.minimum(i, nu[0] - 1), 0)

    def wmap(i, be, nu):
        return (be[i], 0, 0)

    grid_spec = pltpu.PrefetchScalarGridSpec(
        num_scalar_prefetch=2,
        grid=(nblk,),
        in_specs=[pl.BlockSpec((MOE_BLOCK, W), xmap),
                  pl.BlockSpec((1, D_MODEL, D_EXPERT), wmap),
                  pl.BlockSpec((1, D_MODEL, D_EXPERT), wmap),
                  pl.BlockSpec((1, D_EXPERT, D_MODEL), wmap)],
        out_specs=pl.BlockSpec((MOE_BLOCK, W), lambda i, be, nu: (i, 0)),
        scratch_shapes=[pltpu.VMEM((D_MODEL, D_EXPERT), BF16), pltpu.VMEM((D_MODEL, D_EXPERT), BF16),
                        pltpu.VMEM((D_EXPERT, D_MODEL), BF16)])
    return pl.pallas_call(
        _expert_kernel,
        grid_spec=grid_spec,
        out_shape=jax.ShapeDtypeStruct((n_slots, W), I32),
        compiler_params=_cparams("arbitrary"),
        name="moe_experts",
    )(block_expert, n_used, xs, wg, wu, wd)


def _sc_row_gather(table, idx):
    n, d = idx.shape[0], table.shape[1]
    nw = SC_CORES * SC_SUBCORES
    per_w = n // nw
    rows = SC_GATHER_ROWS
    assert n % (nw * 2 * rows) == 0
    n_pairs = per_w // (2 * rows)
    mesh = plsc.VectorSubcoreMesh(core_axis_name="c", subcore_axis_name="s",
                                  num_cores=SC_CORES, num_subcores=SC_SUBCORES)

    @functools.partial(
        pl.kernel, mesh=mesh, out_type=jax.ShapeDtypeStruct((n, d), table.dtype),
        scratch_types=[pltpu.VMEM((rows,), I32), pltpu.VMEM((rows,), I32),
                       pltpu.VMEM((rows, d), table.dtype), pltpu.VMEM((rows, d), table.dtype),
                       pltpu.SemaphoreType.DMA, pltpu.SemaphoreType.DMA],
        name="moe_gather_sc")
    def gather(table_hbm, idx_hbm, out_hbm, idx_a, idx_b, rows_a, rows_b, sem_a, sem_b):
        wid = lax.axis_index("s") * SC_CORES + lax.axis_index("c")
        base = wid * per_w
        bufs = ((idx_a, rows_a, sem_a), (idx_b, rows_b, sem_b))

        def start(c, buf):
            idx_v, rows_v, sem = buf
            off = pl.multiple_of(base + c * rows, rows)
            pltpu.sync_copy(idx_hbm.at[pl.ds(off, rows)], idx_v)
            pltpu.async_copy(table_hbm.at[idx_v], rows_v, sem)

        def finish(c, buf):
            idx_v, rows_v, sem = buf
            off = pl.multiple_of(base + c * rows, rows)
            pltpu.make_async_copy(table_hbm.at[idx_v], rows_v, sem).wait()
            pltpu.sync_copy(rows_v, out_hbm.at[pl.ds(off, rows)])

        start(0, bufs[0])

        def body(p, carry):
            c = 2 * p
            start(c + 1, bufs[1])
            finish(c, bufs[0])

            @pl.when(p + 1 < n_pairs)
            def _():
                start(c + 2, bufs[0])

            finish(c + 1, bufs[1])
            return carry

        lax.fori_loop(0, n_pairs, body, 0)

    return gather(table, idx)


def _combine_kernel(x1_ref, gate_ref, yk_ref, wsg_ref, wsu_ref, wsd_ref, g2_ref, b2_ref, yp_ref, ysmp_ref,
                    *, n_prompt_tiles):
    i = pl.program_id(0)
    tm = x1_ref.shape[0]
    x1 = x1_ref[...]
    xb = x1.astype(BF16)
    g = _dot(xb, wsg_ref[...])
    u = _dot(xb, wsu_ref[...])
    ff = _dot((g * jax.nn.sigmoid(g) * u).astype(BF16), wsd_ref[...])
    gate = gate_ref[...]
    for kk in range(TOP_K):
        ff = ff + gate[:, kk:kk + 1] * _unpack_bf16_pairs(pltpu.bitcast(yk_ref[kk * tm:(kk + 1) * tm, :], U32))
    y = _layer_norm(ALPHA * x1 + ff, g2_ref[...], b2_ref[...])

    @pl.when(i < n_prompt_tiles)
    def _():
        yp_ref[0] = y

    @pl.when(i >= n_prompt_tiles)
    def _():
        ysmp_ref[...] = y


def _moe_combine(dest, x1, gate, ys, wsg, wsu, wsd, ln2_g, ln2_b, geo, seq):
    R = x1.shape[0]
    nt = R // ROW_TILE
    W = ys.shape[1]
    B, tpb, npt = geo["B"], geo["tpb"], geo["npt"]
    assert seq == (tpb - 1) * ROW_TILE
    order = jnp.transpose(dest.reshape(nt, ROW_TILE, TOP_K), (0, 2, 1)).reshape(R * TOP_K)
    yk = _sc_row_gather(ys, order)

    def rows(w):
        return pl.BlockSpec((ROW_TILE, w), lambda i: (i, 0))

    def full(a):
        return pl.BlockSpec(a.shape, lambda i: (0,) * a.ndim)

    def rows_k(w):
        return pl.BlockSpec((TOP_K * ROW_TILE, w), lambda i: (i, 0))

    def yp_map(i):
        in_prompt = i < npt
        return (jnp.minimum(i // tpb, B - 1), jnp.where(in_prompt, jnp.maximum(i % tpb - 1, 0), tpb - 2), 0)

    return pl.pallas_call(
        functools.partial(_combine_kernel, n_prompt_tiles=npt),
        grid=(nt,),
        in_specs=[rows(D_MODEL), rows(LANES), rows_k(W),
                  full(wsg), full(wsu), full(wsd), full(ln2_g), full(ln2_b)],
        out_specs=[pl.BlockSpec((1, ROW_TILE, D_MODEL), yp_map),
                   pl.BlockSpec((ROW_TILE, D_MODEL), lambda i: (jnp.maximum(i - npt, 0), 0))],
        out_shape=[jax.ShapeDtypeStruct((B, seq, D_MODEL), F32),
                   jax.ShapeDtypeStruct((R - npt * ROW_TILE, D_MODEL), F32)],
        compiler_params=_cparams("arbitrary"),
        name="moe_combine",
    )(x1, gate, yk, wsg, wsu, wsd, ln2_g, ln2_b)


def _routing_tables(idx8, rank8, counts, n_slots_blocks):
    R = idx8.shape[0]
    padded = (counts + MOE_BLOCK - 1) // MOE_BLOCK * MOE_BLOCK
    pends = jnp.cumsum(padded)
    pstarts = pends - padded
    experts = jnp.arange(N_EXPERTS, dtype=I32)
    start8 = jnp.sum(jnp.where(idx8[:, :, None] == experts[None, None, :], pstarts[None, None, :], 0), axis=-1)
    dest = start8 + rank8
    blk_start = jnp.arange(n_slots_blocks, dtype=I32) * MOE_BLOCK
    block_expert = jnp.minimum(jnp.sum((blk_start[:, None] >= pends[None, :]).astype(I32), axis=1), N_EXPERTS - 1)
    n_used = (pends[-1] // MOE_BLOCK).astype(I32).reshape(1)
    return dest.astype(I32), pends.astype(I32), padded.astype(I32), block_expert, n_used


def kernel(x_prompt, x_sample, cache_mla_ckv, cache_mla_krope, state_hgrn, page_table, meta_tokens, hg_lb_logits,
           w_in, q_norm, kv_norm, w_uq, w_uk, w_uv, hg_norm, w_br_mla, w_br_hg, w_out, ln1_g, ln1_b, router_w,
           router_bias, w_exp_gate, w_exp_up, w_exp_down, w_sh_gate, w_sh_up, w_sh_down, ln2_g, ln2_b):
    assert w_in.shape[0] == DEPTH
    B, seq, _ = x_prompt.shape
    DB, steps, _ = x_sample.shape
    n_meta = meta_tokens.shape[0]
    n_pages, page = page_table.shape[1], cache_mla_ckv.shape[2]
    past = n_pages * page
    T = n_meta + seq
    tp = -(-T // ROW_TILE) * ROW_TILE
    pad_front = tp - T
    tpb = tp // ROW_TILE
    n_s = DB * steps
    assert n_s % ROW_TILE == 0 and DB % HG_SAMPLE_BATCH == 0
    geo = dict(B=B, tp=tp, tpb=tpb, pad_front=pad_front, npt=B * tpb)
    Rp = B * tp
    R = Rp + n_s

    assert pad_front + n_meta == ROW_TILE and seq % ROW_TILE == 0
    x_head = jnp.concatenate([jnp.zeros((pad_front, D_MODEL), F32), meta_tokens.astype(F32)], axis=0)
    x_smp = x_sample.reshape(n_s, D_MODEL)

    pos_p = jnp.maximum(jnp.arange(tp) - pad_front, 0)
    pos = jnp.concatenate([jnp.tile(pos_p, B), jnp.tile(past + jnp.arange(steps), DB)]).astype(F32)
    half = MLA_ROPE // 2
    inv = ROPE_THETA ** (-jnp.arange(half, dtype=F32) / half)
    ang = pos[:, None] * inv[None, :]
    cos8 = jnp.tile(jnp.cos(ang), (1, LANES // half))
    sin8 = jnp.tile(jnp.sin(ang), (1, LANES // half))

    l = 0
    win = w_in[l]
    kr_end = MLA_Q_RANK + MLA_KV_RANK + MLA_ROPE
    w_in_p = jnp.concatenate([win[:, :kr_end], jnp.zeros((D_MODEL, LANES - MLA_ROPE), F32), win[:, kr_end:]],
                             axis=1).astype(BF16)
    assert w_in_p.shape[1] == D_IN_PACKED
    wq3 = w_uq[l].reshape(MLA_Q_RANK, MLA_HEADS, MLA_NOPE + MLA_ROPE)
    wqr = (jnp.concatenate([wq3[:, :, MLA_NOPE:MLA_NOPE + half].reshape(MLA_Q_RANK, MLA_HEADS * half),
                            wq3[:, :, MLA_NOPE + half:].reshape(MLA_Q_RANK, MLA_HEADS * half)], axis=1)
           * Q_SCALE).astype(BF16)
    src = jnp.arange(2 * LANES)
    hh, ii = (src % LANES) // half, src % half
    dst = hh * LANES + ii + jnp.where(src >= LANES, half, 0)
    perm = (dst[:, None] == jnp.arange(MLA_HEADS * LANES)[None, :]).astype(BF16)
    lb = jnp.cumsum(jax.nn.softmax(hg_lb_logits.astype(F32), axis=0), axis=0)[l].reshape(1, -1)

    wqlat, wo = _weight_prep(w_uq[l], w_uk[l], w_uv[l], w_br_mla[l])

    (qcat, kcat, ckv, kr, hq, hk, hv, lf, sg, sm, sh) = _in_projection(
        x_prompt, x_head, x_smp, w_in_p, q_norm[l].reshape(1, -1), kv_norm[l].reshape(1, -1), wqlat, wqr, perm,
        cos8, sin8, lb, geo)

    o_lat_p = _prompt_attention(qcat, kcat, geo)
    q_s = qcat[:, Rp:].reshape(MLA_HEADS, DB, steps, QCAT)
    knew_s = kcat[Rp:].reshape(DB, steps, 2 * LANES)
    cache_krt = jnp.swapaxes(cache_mla_krope[l], 1, 2)
    o_lat_s = _sample_attention(page_table, q_s, knew_s, cache_mla_ckv[l], cache_krt)
    o_lat_s = o_lat_s.reshape(n_s, MLA_HEADS * LANES)
    o_hg_p, s_prompt = _hgrn_prompt(hq, hk, hv, lf, geo)
    o_hg_s, s_sample = _hgrn_sample(hq, hk, hv, lf, state_hgrn[l], Rp, steps)

    rw_f = jnp.concatenate([router_w[l], jnp.zeros((D_MODEL, LANES - N_EXPERTS), F32)], axis=1)
    rw_hi = rw_f.astype(BF16)
    rw_p = jnp.stack([rw_hi, (rw_f - rw_hi.astype(F32)).astype(BF16)])
    rb_p = jnp.concatenate([router_bias[l].astype(F32), jnp.zeros((LANES - N_EXPERTS,), F32)]).reshape(1, -1)
    x1, x1p, idx, gate, rank, count = _layer_tail(
        x_prompt, x_head, x_smp, o_lat_p, o_lat_s, o_hg_p, o_hg_s, sg, sm, sh, wo, w_br_hg[l].astype(BF16),
        w_out[l].astype(BF16), hg_norm[l].reshape(1, -1), ln1_g[l].reshape(1, -1), ln1_b[l].reshape(1, -1),
        rw_p, rb_p, geo)

    nblk = R * TOP_K // MOE_BLOCK + N_EXPERTS
    dest, pends, padded, block_expert, n_used = _routing_tables(
        idx[:, :TOP_K], rank[:, :TOP_K], count[0, :N_EXPERTS], nblk)
    xs = _moe_dispatch(pends, padded, dest.reshape(R * TOP_K), x1p, nblk * MOE_BLOCK)
    ys = _moe_experts(block_expert, n_used, xs, w_exp_gate[l], w_exp_up[l], w_exp_down[l])
    yp, y_smp = _moe_combine(dest, x1, gate, ys, w_sh_gate[l].astype(BF16), w_sh_up[l].astype(BF16),
                             w_sh_down[l].astype(BF16), ln2_g[l].reshape(1, -1), ln2_b[l].reshape(1, -1), geo, seq)

    ys_out = y_smp.reshape(DB, steps, D_MODEL)
    ckv_p = ckv[:Rp].reshape(B, tp, MLA_KV_RANK)[:, pad_front:][None]
    kr_p = kr[:Rp].reshape(B, tp, LANES)[:, pad_front:, :MLA_ROPE][None]
    ckv_s = ckv[Rp:].reshape(DB, steps, MLA_KV_RANK)[None]
    kr_s = kr[Rp:, :MLA_ROPE].reshape(DB, steps, MLA_ROPE)[None]
    return (yp, ys_out, ckv_p, kr_p, s_prompt[None], ckv_s, kr_s, s_sample[None])
```

```python
import functools

import jax
import jax.numpy as jnp
from jax import lax
from jax.experimental import pallas as pl
from jax.experimental.pallas import tpu as pltpu
from jax.experimental.pallas import tpu_sc as plsc

F32 = jnp.float32
BF16 = jnp.bfloat16
U32 = jnp.uint32
I32 = jnp.int32

D_MODEL = 1024
MLA_HEADS = 8
MLA_Q_RANK = 256
MLA_KV_RANK = 128
MLA_NOPE = 64
MLA_ROPE = 32
MLA_V = 64
MLA_SCALE = (MLA_NOPE + MLA_ROPE) ** -0.5
LOG2E = 1.4426950408889634
Q_SCALE = MLA_SCALE * LOG2E
ROPE_THETA = 10000.0
HG_HEADS = 4
HG_DK = 128
HG_DV = 128
HG_CHUNK = 64
N_EXPERTS = 64
TOP_K = 8
D_EXPERT = 256
ROUTED_SCALE = 2.5
NORM_EPS = 1e-6
DEPTH = 1
ALPHA = (2.0 * DEPTH) ** 0.25

LANES = 128
ROW_TILE = 256
QCAT = 256
ATT_TK = 256
MOE_BLOCK = 512
PAGES_PER_CHUNK = 64
ONES_COL = LANES + MLA_ROPE
HG_SAMPLE_BATCH = 8
SATT_BATCH = 1
SC_CORES = 2
SC_SUBCORES = 16
SC_GATHER_ROWS = 64
SC_SCATTER_TOKENS = 16
NEG_BIG = -1e30
VMEM_LIMIT = 52 * 1024 * 1024

C_CQ = (0, 256)
C_CKV = (256, 384)
C_KR = (384, 512)
C_HQ = (512, 1024)
C_HF = (1024, 1536)
C_HI = (1536, 2048)
C_HG = (2048, 2560)
C_GM = (2560, 3584)
C_GH = (3584, 4608)
D_IN_PACKED = 4608


def _cparams(*sem):
    return pltpu.CompilerParams(dimension_semantics=sem, vmem_limit_bytes=VMEM_LIMIT)


def _dot(a, b):
    return jnp.dot(a, b, preferred_element_type=F32)


def _dot_nt(a, b):
    return lax.dot_general(a, b, (((1,), (1,)), ((), ())), preferred_element_type=F32)


def _dot_tn(a, b):
    return lax.dot_general(a, b, (((0,), (0,)), ((), ())), preferred_element_type=F32)


def _pack_bf16_pairs(x):
    w = x.shape[1] // 2
    bits = pltpu.bitcast(x.astype(BF16).astype(F32), U32)
    return bits[:, w:] | (bits[:, :w] >> 16)


def _unpack_bf16_pairs(p):
    lo = pltpu.bitcast(p << 16, F32)
    hi = pltpu.bitcast(p & jnp.uint32(0xFFFF0000), F32)
    return jnp.concatenate([lo, hi], axis=1)


def _x_specs(geo, seq_tiles):
    B, tpb, npt = geo["B"], geo["tpb"], geo["npt"]

    def xp_map(i):
        return (jnp.minimum(i // tpb, B - 1), jnp.where(i < npt, jnp.maximum(i % tpb - 1, 0), seq_tiles - 1), 0)

    return [pl.BlockSpec((1, ROW_TILE, D_MODEL), xp_map),
            pl.BlockSpec((ROW_TILE, D_MODEL), lambda i: (0, 0)),
            pl.BlockSpec((ROW_TILE, D_MODEL), lambda i: (jnp.maximum(i - npt, 0), 0))]


def _select_x(i, xp_ref, xh_ref, xs_ref, tiles_per_batch, n_prompt_tiles):
    is_head = (i < n_prompt_tiles) & (i % tiles_per_batch == 0)
    return jnp.where(i >= n_prompt_tiles, xs_ref[...], jnp.where(is_head, xh_ref[...], xp_ref[0]))


def _split_specs(width, npt):
    return [pl.BlockSpec((ROW_TILE, width), lambda i: (jnp.minimum(i, npt - 1), 0)),
            pl.BlockSpec((ROW_TILE, width), lambda i: (jnp.maximum(i - npt, 0), 0))]


def _wprep_kernel(wq_nope_ref, wuk_ref, wuv_ref, wbr_ref, wqlat_ref, wo_ref):
    hp = lax.Precision.HIGHEST
    for h in range(MLA_HEADS):
        a = wq_nope_ref[h]
        b = wuk_ref[h]
        ql = lax.dot_general(a, b, (((1,), (1,)), ((), ())), precision=hp, preferred_element_type=F32)
        wqlat_ref[:, h * MLA_KV_RANK:(h + 1) * MLA_KV_RANK] = (ql * Q_SCALE).astype(BF16)
        c = wuv_ref[h]
        d = wbr_ref[h]
        wo_ref[h * MLA_KV_RANK:(h + 1) * MLA_KV_RANK, :] = jnp.dot(
            c, d, precision=hp, preferred_element_type=F32).astype(BF16)


def _weight_prep(w_uq, w_uk, w_uv, w_br_mla):
    wq3 = w_uq.reshape(MLA_Q_RANK, MLA_HEADS, MLA_NOPE + MLA_ROPE)
    wq_nope = jnp.transpose(wq3[:, :, :MLA_NOPE], (1, 0, 2))
    wuk = jnp.transpose(w_uk, (1, 0, 2))
    wuv = jnp.transpose(w_uv, (1, 0, 2))
    wbr = w_br_mla.reshape(MLA_HEADS, MLA_V, D_MODEL)
    return pl.pallas_call(
        _wprep_kernel,
        out_shape=(jax.ShapeDtypeStruct((MLA_Q_RANK, MLA_HEADS * MLA_KV_RANK), BF16),
                   jax.ShapeDtypeStruct((MLA_HEADS * MLA_KV_RANK, D_MODEL), BF16)),
        compiler_params=pltpu.CompilerParams(vmem_limit_bytes=VMEM_LIMIT),
        name="weight_prep",
    )(wq_nope, wuk, wuv, wbr)


def _inproj_kernel(xp_ref, xh_ref, xs_ref, w_ref, qn_ref, kvn_ref, wqlat_ref, wqr_ref, perm_ref, cos_ref, sin_ref,
                   lb_ref, qcat_ref, kcat_ref, ckv_ref, kr_ref, hq_ref, hk_ref, hv_ref, lf_ref, sg_ref, sm_ref,
                   sh_ref, *, pad_front, tiles_per_batch, n_prompt_tiles):
    i = pl.program_id(0)
    tm = xh_ref.shape[0]
    xb = _select_x(i, xp_ref, xh_ref, xs_ref, tiles_per_batch, n_prompt_tiles).astype(BF16)

    def proj(c):
        return _dot(xb, w_ref[:, c[0]:c[1]])

    cos8 = cos_ref[...]
    sin8 = sin_ref[...]

    cq = proj(C_CQ)
    cqn = cq * lax.rsqrt(jnp.mean(cq * cq, axis=-1, keepdims=True) + NORM_EPS) * qn_ref[...]
    cqb = cqn.astype(BF16)
    qlat = _dot(cqb, wqlat_ref[...])
    qr = _dot(cqb, wqr_ref[...])
    x1, x2 = qr[:, :LANES], qr[:, LANES:]
    qrot = jnp.concatenate([x1 * cos8 - x2 * sin8, x2 * cos8 + x1 * sin8], axis=1).astype(BF16)
    qrh = _dot(qrot, perm_ref[...])
    for h in range(MLA_HEADS):
        qcat_ref[h, :, :LANES] = qlat[:, h * LANES:(h + 1) * LANES].astype(BF16)
        qcat_ref[h, :, LANES:] = qrh[:, h * LANES:(h + 1) * LANES].astype(BF16)

    kv = proj(C_CKV)
    ckv = kv * lax.rsqrt(jnp.mean(kv * kv, axis=-1, keepdims=True) + NORM_EPS) * kvn_ref[...]
    ckv_ref[...] = ckv
    kcat_ref[:, :LANES] = ckv.astype(BF16)
    krr = proj(C_KR)
    lane = lax.broadcasted_iota(I32, (tm, LANES), 1)
    half = MLA_ROPE // 2
    rot = jnp.where(lane < half, -pltpu.roll(krr, LANES - half, 1), pltpu.roll(krr, half, 1))
    kr = jnp.where(lane < MLA_ROPE, krr * cos8 + rot * sin8, 0.0)
    kr_ref[...] = kr
    kcat_ref[:, LANES:] = jnp.where(lane == ONES_COL - LANES, 1.0, kr).astype(BF16)

    row = lax.broadcasted_iota(I32, (tm, 1), 0)
    is_pad = (i < n_prompt_tiles) & (i % tiles_per_batch == 0) & (row < pad_front)
    keep = jnp.where(is_pad, 0.0, 1.0)
    hq_ref[...] = proj(C_HQ).astype(BF16)
    lb = lb_ref[...]
    f = lb + (1.0 - lb) * jax.nn.sigmoid(proj(C_HF))
    lf_ref[...] = jnp.log(f) * keep
    hk_ref[...] = ((1.0 - f) * keep).astype(BF16)
    hv_ref[...] = proj(C_HI).astype(BF16)

    g = proj(C_HG)
    sg_ref[...] = (g * jax.nn.sigmoid(g)).astype(BF16)
    sm_ref[...] = jax.nn.sigmoid(proj(C_GM)).astype(BF16)
    sh_ref[...] = jax.nn.sigmoid(proj(C_GH)).astype(BF16)


def _in_projection(x_prompt, x_head, x_smp, w_in_p, q_norm, kv_norm, wqlat, wqr, perm, cos8, sin8, lb, geo):
    R = cos8.shape[0]
    nt = R // ROW_TILE
    hgw = HG_HEADS * HG_DK
    x_specs = _x_specs(geo, x_prompt.shape[1] // ROW_TILE)

    def rows(w):
        return pl.BlockSpec((ROW_TILE, w), lambda i: (i, 0))

    def full(a):
        return pl.BlockSpec(a.shape, lambda i: (0,) * a.ndim)

    out_widths = [(2 * LANES, BF16), (LANES, F32), (LANES, F32),
                  (hgw, BF16), (hgw, BF16), (hgw, BF16), (hgw, F32), (hgw, BF16),
                  (D_MODEL, BF16), (D_MODEL, BF16)]
    qcat_spec = pl.BlockSpec((MLA_HEADS, ROW_TILE, QCAT), lambda i: (0, i, 0))
    return pl.pallas_call(
        functools.partial(_inproj_kernel, pad_front=geo["pad_front"], tiles_per_batch=geo["tpb"],
                          n_prompt_tiles=geo["npt"]),
        grid=(nt,),
        in_specs=x_specs + [full(w_in_p), full(q_norm), full(kv_norm), full(wqlat), full(wqr), full(perm),
                            rows(LANES), rows(LANES), full(lb)],
        out_specs=[qcat_spec] + [rows(w) for w, _ in out_widths],
        out_shape=[jax.ShapeDtypeStruct((MLA_HEADS, R, QCAT), BF16)]
        + [jax.ShapeDtypeStruct((R, w), dt) for w, dt in out_widths],
        compiler_params=_cparams("arbitrary"),
        name="in_projection",
    )(x_prompt, x_head, x_smp, w_in_p, q_norm, kv_norm, wqlat, wqr, perm, cos8, sin8, lb)


def _softmax_step(s, v_b, m_ref, l_ref, acc_ref):
    n = s.shape[1] // LANES
    m_prev = m_ref[...]
    m_next = jnp.maximum(m_prev, jnp.max(s, axis=1, keepdims=True))
    p = jnp.concatenate([jnp.exp2(s[:, j * LANES:(j + 1) * LANES] - m_next) for j in range(n)], axis=1)
    alpha = jnp.exp2(m_prev - m_next)
    pv = _dot(p.astype(BF16), v_b)
    if v_b.shape[1] == LANES:
        l_ref[...] = alpha * l_ref[...] + jnp.sum(p, axis=1, keepdims=True)
    else:
        l_ref[...] = alpha * l_ref[...] + pv[:, LANES:]
    acc_ref[...] = alpha * acc_ref[...] + pv[:, :LANES]
    m_ref[...] = m_next


def _pattn_kernel(q_ref, k_ref, o_ref, m_ref, l_ref, acc_ref, *, pad_front):
    qi = pl.program_id(1)
    nh, tq, _ = q_ref.shape
    tk = ATT_TK
    rows = nh * tq

    q = q_ref[...].reshape(rows, QCAT)
    m_ref[...] = jnp.full(m_ref.shape, -jnp.inf, F32)
    l_ref[...] = jnp.zeros(l_ref.shape, F32)
    acc_ref[...] = jnp.zeros(acc_ref.shape, F32)

    def step(kb, masked):
        kblk = k_ref[pl.ds(pl.multiple_of(kb * tk, tk), tk), :]
        s = _dot_nt(q, kblk)
        if masked:
            qrow = qi * tq + lax.broadcasted_iota(I32, (rows, tk), 0) % tq
            krow = kb * tk + lax.broadcasted_iota(I32, (rows, tk), 1)
            s = jnp.where((krow <= qrow) & (krow >= pad_front), s, NEG_BIG)
        _softmax_step(s, kblk, m_ref, l_ref, acc_ref)

    step(0, True)

    def body(kb, c):
        step(kb, False)
        return c

    lax.fori_loop(1, qi, body, 0)

    @pl.when(qi > 0)
    def _():
        step(qi, True)

    row_sum = l_ref[:, ONES_COL - LANES:ONES_COL - LANES + 1]
    o = (acc_ref[...] / row_sum).astype(BF16)
    for h in range(nh):
        o_ref[:, h * LANES:(h + 1) * LANES] = o[h * tq:(h + 1) * tq]


def _prompt_attention(qcat, kcat, geo):
    B, tpb, tp = geo["B"], geo["tpb"], geo["tp"]
    rows = MLA_HEADS * ROW_TILE
    return pl.pallas_call(
        functools.partial(_pattn_kernel, pad_front=geo["pad_front"]),
        grid=(B, tpb),
        in_specs=[pl.BlockSpec((MLA_HEADS, ROW_TILE, QCAT), lambda b, i: (0, b * tpb + i, 0)),
                  pl.BlockSpec((tp, 2 * LANES), lambda b, i: (b, 0))],
        out_specs=pl.BlockSpec((ROW_TILE, MLA_HEADS * LANES), lambda b, i: (b * tpb + i, 0)),
        out_shape=jax.ShapeDtypeStruct((B * tp, MLA_HEADS * LANES), BF16),
        scratch_shapes=[pltpu.VMEM((rows, LANES), F32)] * 3,
        compiler_params=_cparams("arbitrary", "arbitrary"),
        name="prompt_attention",
    )(qcat, kcat)


def _sattn_kernel(pt_ref, q_ref, knew_ref, ckv_hbm, kr_hbm, o_ref,
                  ckv_buf, kr_buf, sem, m_ref, l_ref, acc_ref, *, n_chunks, page):
    g = pl.program_id(0)
    ng = pl.num_programs(0)
    ch = PAGES_PER_CHUNK
    nh, nbt, S = q_ref.shape[0], q_ref.shape[1], q_ref.shape[2]
    rows = nh * S

    def chunk_copies(gg, c, slot):
        cps = []
        for bb in range(nbt):
            for j in range(ch):
                pg = pt_ref[gg * nbt + bb, c * ch + j]
                cps.append(pltpu.make_async_copy(ckv_hbm.at[pg], ckv_buf.at[slot, bb, pl.ds(j * page, page), :],
                                                 sem.at[0, slot]))
                cps.append(pltpu.make_async_copy(kr_hbm.at[pg], kr_buf.at[slot, bb, :, pl.ds(j * page, page)],
                                                 sem.at[1, slot]))
        return cps

    @pl.when(g == 0)
    def _():
        for cp in chunk_copies(0, 0, 0):
            cp.start()

    qlat, qrope = [], []
    for bb in range(nbt):
        q = q_ref[:, bb].reshape(rows, QCAT)
        qlat.append(q[:, :LANES])
        qrope.append(q[:, LANES:LANES + MLA_ROPE])
        knew = knew_ref[bb]
        s_new = _dot_nt(q, knew)
        qtok = lax.broadcasted_iota(I32, (rows, S), 0) % S
        ktok = lax.broadcasted_iota(I32, (rows, S), 1)
        s_new = jnp.where(ktok <= qtok, s_new, NEG_BIG)
        m0 = jnp.max(s_new, axis=1, keepdims=True)
        p0 = jnp.exp2(s_new - m0)
        m_ref[bb] = jnp.broadcast_to(m0, (rows, LANES))
        l_ref[bb] = jnp.broadcast_to(jnp.sum(p0, axis=1, keepdims=True), (rows, LANES))
        acc_ref[bb] = _dot(p0.astype(BF16), knew[:, :LANES])

    for c in range(n_chunks):
        slot = c % 2 if n_chunks % 2 == 0 else (g * n_chunks + c) % 2
        if c + 1 < n_chunks:
            for cp in chunk_copies(g, c + 1, 1 - slot):
                cp.start()
        else:
            @pl.when(g + 1 < ng)
            def _():
                for cp in chunk_copies(g + 1, 0, 1 - slot):
                    cp.start()
        for cp in chunk_copies(g, c, slot):
            cp.wait()
        for bb in range(nbt):
            ckv_b = ckv_buf[slot, bb].astype(BF16)
            krt_b = kr_buf[slot, bb].astype(BF16)
            s = _dot_nt(qlat[bb], ckv_b) + _dot(qrope[bb], krt_b)
            _softmax_step(s, ckv_b, m_ref.at[bb], l_ref.at[bb], acc_ref.at[bb])

    for bb in range(nbt):
        o = (acc_ref[bb] / l_ref[bb]).astype(BF16)
        for h in range(nh):
            o_ref[bb, :, h * LANES:(h + 1) * LANES] = o[h * S:(h + 1) * S]


def _sample_attention(page_table, q_s, knew_s, cache_ckv, cache_krt):
    nh, DB, S, _ = q_s.shape
    rows = nh * S
    n_pages = page_table.shape[1]
    page = cache_ckv.shape[1]
    assert n_pages % PAGES_PER_CHUNK == 0
    n_chunks = n_pages // PAGES_PER_CHUNK
    ck = PAGES_PER_CHUNK * page
    nbt = SATT_BATCH
    assert DB % nbt == 0
    grid_spec = pltpu.PrefetchScalarGridSpec(
        num_scalar_prefetch=1,
        grid=(DB // nbt,),
        in_specs=[pl.BlockSpec((nh, nbt, S, QCAT), lambda g, pt: (0, g, 0, 0)),
                  pl.BlockSpec((nbt, S, 2 * LANES), lambda g, pt: (g, 0, 0)),
                  pl.BlockSpec(memory_space=pl.ANY),
                  pl.BlockSpec(memory_space=pl.ANY)],
        out_specs=pl.BlockSpec((nbt, S, nh * LANES), lambda g, pt: (g, 0, 0)),
        scratch_shapes=[pltpu.VMEM((2, nbt, ck, MLA_KV_RANK), F32),
                        pltpu.VMEM((2, nbt, MLA_ROPE, ck), F32),
                        pltpu.SemaphoreType.DMA((2, 2)),
                        pltpu.VMEM((nbt, rows, LANES), F32),
                        pltpu.VMEM((nbt, rows, LANES), F32),
                        pltpu.VMEM((nbt, rows, LANES), F32)])
    return pl.pallas_call(
        functools.partial(_sattn_kernel, n_chunks=n_chunks, page=page),
        grid_spec=grid_spec,
        out_shape=jax.ShapeDtypeStruct((DB, S, nh * LANES), BF16),
        compiler_params=_cparams("arbitrary"),
        name="sample_attention",
    )(page_table, q_s, knew_s, cache_ckv, cache_krt)


def _split3(x):
    hi = x.astype(BF16)
    r1 = x - hi.astype(F32)
    mid = r1.astype(BF16)
    lo = (r1 - mid.astype(F32)).astype(BF16)
    return hi, mid, lo


def _hgrn_chunk(q, k, v, lf, S0, tri, group):
    C = q.shape[0]
    hi, mid, lo = _split3(lf)
    cum = _dot(tri, hi) + _dot(tri, mid) + _dot(tri, lo)
    vb = v.astype(BF16)
    rowi = lax.broadcasted_iota(I32, (C, C), 0)
    coli = lax.broadcasted_iota(I32, (C, C), 1)
    attn = jnp.zeros((C, C), F32)

    bd = min(16, group)
    hs = group // 2
    rid = lax.broadcasted_iota(I32, (C, 1), 0)
    while hs >= bd:
        npair = C // (2 * hs)
        ref = jnp.concatenate(
            [jnp.broadcast_to(cum[(2 * j + 1) * hs - 1:(2 * j + 1) * hs, :], (2 * hs, cum.shape[1]))
             for j in range(npair)], axis=0)
        odd = ((rid // hs) % 2) == 1
        e = jnp.exp(jnp.where(odd, cum - ref, ref - cum))
        qs = jnp.where(odd, q * e, 0.0).astype(BF16)
        ks = jnp.where(odd, 0.0, k * e).astype(BF16)
        a = _dot_nt(qs, ks)
        attn = attn + jnp.where((rowi // (2 * hs)) == (coli // (2 * hs)), a, 0.0)
        hs //= 2

    nblk = C // bd
    k3 = k.reshape(nblk, bd, k.shape[1])
    c3 = cum.reshape(nblk, bd, cum.shape[1])
    tl = rid % bd
    blk_base = (rowi // bd) * bd
    for sl in range(bd):
        ks_b = jnp.broadcast_to(k3[:, sl:sl + 1, :], k3.shape).reshape(C, k.shape[1])
        cs_b = jnp.broadcast_to(c3[:, sl:sl + 1, :], c3.shape).reshape(C, k.shape[1])
        e = jnp.exp(jnp.where(tl >= sl, cum - cs_b, NEG_BIG))
        col = jnp.sum(q * ks_b * e, axis=1, keepdims=True)
        attn = jnp.where(coli == blk_base + sl, col, attn)
    o_intra = _dot(attn.astype(BF16), vb)
    return o_intra, cum


def _tri(C, group):
    r = lax.broadcasted_iota(I32, (C, C), 0)
    c = lax.broadcasted_iota(I32, (C, C), 1)
    return jnp.where((c <= r) & (r // group == c // group), 1.0, 0.0).astype(BF16)


def _state_update(q, k, v, cum, S, o_intra):
    C = q.shape[0]
    last = cum[C - 1:C, :]
    o = o_intra + _dot((q * jnp.exp(cum)).astype(BF16), S.astype(BF16))
    kst = (k * jnp.exp(last - cum)).astype(BF16)
    dfull = jnp.transpose(jnp.broadcast_to(jnp.exp(last), (S.shape[1], S.shape[0])))
    S_new = dfull * S + _dot_tn(kst, v.astype(BF16))
    return o, S_new


def _hgrn_prompt_kernel(q_ref, k_ref, v_ref, lf_ref, o_ref, s_out_ref, s_ref):
    t = pl.program_id(1)

    @pl.when(t == 0)
    def _():
        s_ref[...] = jnp.zeros(s_ref.shape, F32)

    C = HG_CHUNK
    tri = _tri(C, C)
    states = [s_ref[h] for h in range(HG_HEADS)]
    for c in range(q_ref.shape[0] // C):
        sl = slice(c * C, (c + 1) * C)
        for h in range(HG_HEADS):
            hl = slice(h * HG_DK, (h + 1) * HG_DK)
            q = q_ref[sl, hl].astype(F32)
            k = k_ref[sl, hl].astype(F32)
            v = v_ref[sl, hl].astype(F32)
            lf = lf_ref[sl, hl]
            o_intra, cum = _hgrn_chunk(q, k, v, lf, None, tri, C)
            o, states[h] = _state_update(q, k, v, cum, states[h], o_intra)
            o_ref[sl, hl] = o.astype(BF16)
    for h in range(HG_HEADS):
        s_ref[h] = states[h]

    @pl.when(t == pl.num_programs(1) - 1)
    def _():
        s_out_ref[0] = s_ref[...]


def _hgrn_prompt(hq, hk, hv, lf, geo):
    B, tpb, tp = geo["B"], geo["tpb"], geo["tp"]
    hgw = HG_HEADS * HG_DK

    def blk():
        return pl.BlockSpec((ROW_TILE, hgw), lambda b, t: (b * tpb + t, 0))

    return pl.pallas_call(
        _hgrn_prompt_kernel,
        grid=(B, tpb),
        in_specs=[blk(), blk(), blk(), blk()],
        out_specs=[blk(), pl.BlockSpec((1, HG_HEADS, HG_DK, HG_DV), lambda b, t: (b, 0, 0, 0))],
        out_shape=[jax.ShapeDtypeStruct((B * tp, hgw), BF16),
                   jax.ShapeDtypeStruct((B, HG_HEADS, HG_DK, HG_DV), F32)],
        scratch_shapes=[pltpu.VMEM((HG_HEADS, HG_DK, HG_DV), F32)],
        compiler_params=_cparams("arbitrary", "arbitrary"),
        name="hgrn_prompt",
    )(hq, hk, hv, lf)


def _hgrn_sample_kernel(q_ref, k_ref, v_ref, lf_ref, s_in_ref, o_ref, s_out_ref, *, steps):
    nb = s_in_ref.shape[0]
    C = nb * steps
    tri = _tri(C, steps)
    for h in range(HG_HEADS):
        sl = slice(h * HG_DK, (h + 1) * HG_DK)
        q = q_ref[:, sl].astype(F32)
        k = k_ref[:, sl].astype(F32)
        v = v_ref[:, sl].astype(F32)
        lf = lf_ref[:, sl]
        o_intra, cum = _hgrn_chunk(q, k, v, lf, None, tri, steps)
        for b in range(nb):
            r = slice(b * steps, (b + 1) * steps)
            o, S_new = _state_update(q[r], k[r], v[r], cum[r], s_in_ref[b, h], o_intra[r])
            o_ref[r, sl] = o.astype(BF16)
            s_out_ref[b, h] = S_new


def _hgrn_sample(hq, hk, hv, lf, state, row0, steps):
    DB = state.shape[0]
    nb = HG_SAMPLE_BATCH
    rows = nb * steps
    hgw = HG_HEADS * HG_DK
    blk0 = row0 // rows

    def tok():
        return pl.BlockSpec((rows, hgw), lambda i: (blk0 + i, 0))

    st = pl.BlockSpec((nb, HG_HEADS, HG_DK, HG_DV), lambda i: (i, 0, 0, 0))
    return pl.pallas_call(
        functools.partial(_hgrn_sample_kernel, steps=steps),
        grid=(DB // nb,),
        in_specs=[tok(), tok(), tok(), tok(), st],
        out_specs=[pl.BlockSpec((rows, hgw), lambda i: (i, 0)), st],
        out_shape=[jax.ShapeDtypeStruct((DB * steps, hgw), BF16),
                   jax.ShapeDtypeStruct(state.shape, F32)],
        compiler_params=_cparams("arbitrary"),
        name="hgrn_sample",
    )(hq, hk, hv, lf, state)


def _layer_norm(x, g, b):
    mu = jnp.mean(x, axis=-1, keepdims=True)
    xc = x - mu
    var = jnp.mean(xc * xc, axis=-1, keepdims=True)
    return xc * lax.rsqrt(var + NORM_EPS) * g + b


def _tail_kernel(xp_ref, xh_ref, xs_ref, olp_ref, ols_ref, ohp_ref, ohs_ref, sg_ref, sm_ref, sh_ref, wo_ref, wbh_ref,
                 wout_ref, hgn_ref, g1_ref, b1_ref, rw_ref, rb_ref, x1_ref, x1p_ref, idx_ref, gate_ref, rank_ref,
                 count_ref, cnt_ref, *, tiles_per_batch, n_prompt_tiles):
    i = pl.program_id(0)
    tm = xh_ref.shape[0]
    x_in = _select_x(i, xp_ref, xh_ref, xs_ref, tiles_per_batch, n_prompt_tiles)
    is_sample = i >= n_prompt_tiles
    mla = _dot(jnp.where(is_sample, ols_ref[...], olp_ref[...]), wo_ref[...])
    oh = jnp.where(is_sample, ohs_ref[...], ohp_ref[...]).astype(F32)
    parts = []
    for h in range(HG_HEADS):
        y = oh[:, h * HG_DV:(h + 1) * HG_DV]
        parts.append(y * lax.rsqrt(jnp.mean(y * y, axis=-1, keepdims=True) + NORM_EPS) * hgn_ref[...])
    hg = (jnp.concatenate(parts, axis=1) * sg_ref[...].astype(F32)).astype(BF16)
    merged = sm_ref[...].astype(F32) * mla + sh_ref[...].astype(F32) * _dot(hg, wbh_ref[...])
    x1 = _layer_norm(ALPHA * x_in + _dot(merged.astype(BF16), wout_ref[...]), g1_ref[...], b1_ref[...])
    x1_ref[...] = x1
    x1p_ref[...] = pltpu.bitcast(_pack_bf16_pairs(x1), I32)

    x1_hi = x1.astype(BF16)
    x1_lo = (x1 - x1_hi.astype(F32)).astype(BF16)
    logits = _dot(x1_hi, rw_ref[0]) + _dot(x1_lo, rw_ref[0]) + _dot(x1_hi, rw_ref[1])
    scores = jax.nn.sigmoid(logits)
    lane = lax.broadcasted_iota(I32, (tm, LANES), 1).astype(F32)
    remaining = jnp.where(lane < N_EXPERTS, scores + rb_ref[...], -jnp.inf)
    idx_out = jnp.zeros((tm, LANES), F32)
    gate_out = jnp.zeros((tm, LANES), F32)
    hits = []
    for kk in range(TOP_K):
        mx = jnp.max(remaining, axis=1, keepdims=True)
        pick = jnp.min(jnp.where(remaining == mx, lane, float(LANES)), axis=1, keepdims=True)
        hit = lane == pick
        hits.append(hit)
        gval = jnp.sum(jnp.where(hit, scores, 0.0), axis=1, keepdims=True)
        idx_out = jnp.where(lane == kk, pick, idx_out)
        gate_out = jnp.where(lane == kk, gval, gate_out)
        remaining = jnp.where(hit, -jnp.inf, remaining)
    gate_out = gate_out / jnp.sum(gate_out, axis=1, keepdims=True) * ROUTED_SCALE
    idx_ref[...] = idx_out.astype(I32)
    gate_ref[...] = gate_out

    @pl.when(pl.program_id(0) == 0)
    def _():
        cnt_ref[...] = jnp.zeros(cnt_ref.shape, F32)

    sel = jnp.where(remaining == -jnp.inf, 1.0, 0.0) * jnp.where(lane < N_EXPERTS, 1.0, 0.0)
    r_i = lax.broadcasted_iota(I32, (tm, tm), 0)
    c_i = lax.broadcasted_iota(I32, (tm, tm), 1)
    before = _dot(jnp.where(c_i < r_i, 1.0, 0.0).astype(BF16), sel.astype(BF16)) + cnt_ref[0:1, :]
    rank_out = jnp.zeros((tm, LANES), F32)
    for kk in range(TOP_K):
        rk = jnp.sum(jnp.where(hits[kk], before, 0.0), axis=1, keepdims=True)
        rank_out = jnp.where(lane == kk, rk, rank_out)
    rank_ref[...] = rank_out.astype(I32)
    total = cnt_ref[0:1, :] + jnp.sum(sel, axis=0, keepdims=True)
    cnt_ref[...] = jnp.broadcast_to(total, cnt_ref.shape)
    count_ref[...] = jnp.broadcast_to(total, count_ref.shape).astype(I32)


def _layer_tail(x_prompt, x_head, x_smp, o_lat_p, o_lat_s, o_hg_p, o_hg_s, sg, sm, sh, wo, w_br_hg, w_out, hg_norm,
                ln1_g, ln1_b, rw_p, rb_p, geo):
    R = sg.shape[0]
    npt = geo["npt"]

    def rows(w):
        return pl.BlockSpec((ROW_TILE, w), lambda i: (i, 0))

    def full(a):
        return pl.BlockSpec(a.shape, lambda i: (0,) * a.ndim)

    hgw = HG_HEADS * HG_DV
    return pl.pallas_call(
        functools.partial(_tail_kernel, tiles_per_batch=geo["tpb"], n_prompt_tiles=npt),
        grid=(R // ROW_TILE,),
        in_specs=_x_specs(geo, x_prompt.shape[1] // ROW_TILE)
        + _split_specs(MLA_HEADS * LANES, npt) + _split_specs(hgw, npt)
        + [rows(hgw), rows(D_MODEL), rows(D_MODEL),
           full(wo), full(w_br_hg), full(w_out), full(hg_norm), full(ln1_g), full(ln1_b),
           full(rw_p), full(rb_p)],
        out_specs=[rows(D_MODEL), rows(D_MODEL // 2), rows(LANES), rows(LANES), rows(LANES),
                   pl.BlockSpec((8, LANES), lambda i: (0, 0))],
        out_shape=[jax.ShapeDtypeStruct((R, D_MODEL), F32), jax.ShapeDtypeStruct((R, D_MODEL // 2), I32),
                   jax.ShapeDtypeStruct((R, LANES), I32), jax.ShapeDtypeStruct((R, LANES), F32),
                   jax.ShapeDtypeStruct((R, LANES), I32), jax.ShapeDtypeStruct((8, LANES), I32)],
        scratch_shapes=[pltpu.VMEM((8, LANES), F32)],
        compiler_params=_cparams("arbitrary"),
        name="layer_tail",
    )(x_prompt, x_head, x_smp, o_lat_p, o_lat_s, o_hg_p, o_hg_s, sg, sm, sh, wo, w_br_hg, w_out, hg_norm,
      ln1_g, ln1_b, rw_p, rb_p)


def _sc_row_scatter(x, dest, n_slots):
    R, d = x.shape
    nw = SC_CORES * SC_SUBCORES
    T = SC_SCATTER_TOKENS
    rows = T * TOP_K
    per_w = R // nw
    assert R % (nw * T) == 0
    n_chunks = per_w // T
    order = jnp.transpose(dest.reshape(R // T, T, TOP_K), (0, 2, 1)).reshape(R * TOP_K)
    mesh = plsc.VectorSubcoreMesh(core_axis_name="c", subcore_axis_name="s",
                                  num_cores=SC_CORES, num_subcores=SC_SUBCORES)

    @functools.partial(
        pl.kernel, mesh=mesh, out_type=jax.ShapeDtypeStruct((n_slots, d), x.dtype),
        scratch_types=[pltpu.VMEM((rows,), I32), pltpu.VMEM((rows, d), x.dtype), pltpu.SemaphoreType.DMA],
        name="moe_dispatch_sc")
    def scatter(x_hbm, idx_hbm, out_hbm, idx_v, rows_v, sem):
        wid = lax.axis_index("s") * SC_CORES + lax.axis_index("c")
        tbase = wid * per_w

        def body(c, carry):
            t0 = pl.multiple_of(tbase + c * T, T)
            off = pl.multiple_of(t0 * TOP_K, rows)
            pltpu.sync_copy(idx_hbm.at[pl.ds(off, rows)], idx_v)
            for k in range(TOP_K):
                pltpu.sync_copy(x_hbm.at[pl.ds(t0, T)], rows_v.at[pl.ds(k * T, T)])
            pltpu.async_copy(rows_v, out_hbm.at[idx_v], sem).wait()
            return carry

        lax.fori_loop(0, n_chunks, body, 0)

    return scatter(x, order)


def _expert_kernel(be_ref, nused_ref, nvalid_ref, xs_ref, wg_ref, wu_ref, wd_ref, ys_ref, wgb_ref, wub_ref, wdb_ref):
    i = pl.program_id(0)

    @pl.when((i == 0) | (be_ref[i] != be_ref[jnp.maximum(i - 1, 0)]))
    def _():
        wgb_ref[...] = wg_ref[0].astype(BF16)
        wub_ref[...] = wu_ref[0].astype(BF16)
        wdb_ref[...] = wd_ref[0].astype(BF16)

    @pl.when(i < nused_ref[0])
    def _():
        row = lax.broadcasted_iota(I32, xs_ref.shape, 0)
        xw = jnp.where(row < nvalid_ref[i], xs_ref[...], 0)
        xb = _unpack_bf16_pairs(pltpu.bitcast(xw, U32)).astype(BF16)
        g = _dot(xb, wgb_ref[...])
        u = _dot(xb, wub_ref[...])
        hmid = (g * jax.nn.sigmoid(g) * u).astype(BF16)
        ys_ref[...] = pltpu.bitcast(_pack_bf16_pairs(_dot(hmid, wdb_ref[...])), I32)

    @pl.when(i >= nused_ref[0])
    def _():
        ys_ref[...] = jnp.zeros(ys_ref.shape, I32)


def _moe_experts(block_expert, n_used, n_valid, xs, wg, wu, wd):
    n_slots, W = xs.shape
    nblk = n_slots // MOE_BLOCK

    def xmap(i, be, nu, nv):
        return (jnp.minimum(i, nu[0] - 1), 0)

    def wmap(i, be, nu, nv):
        return (be[i], 0, 0)

    grid_spec = pltpu.PrefetchScalarGridSpec(
        num_scalar_prefetch=3,
        grid=(nblk,),
        in_specs=[pl.BlockSpec((MOE_BLOCK, W), xmap),
                  pl.BlockSpec((1, D_MODEL, D_EXPERT), wmap),
                  pl.BlockSpec((1, D_MODEL, D_EXPERT), wmap),
                  pl.BlockSpec((1, D_EXPERT, D_MODEL), wmap)],
        out_specs=pl.BlockSpec((MOE_BLOCK, W), lambda i, be, nu, nv: (i, 0)),
        scratch_shapes=[pltpu.VMEM((D_MODEL, D_EXPERT), BF16), pltpu.VMEM((D_MODEL, D_EXPERT), BF16),
                        pltpu.VMEM((D_EXPERT, D_MODEL), BF16)])
    return pl.pallas_call(
        _expert_kernel,
        grid_spec=grid_spec,
        out_shape=jax.ShapeDtypeStruct((n_slots, W), I32),
        compiler_params=_cparams("arbitrary"),
        name="moe_experts",
    )(block_expert, n_used, n_valid, xs, wg, wu, wd)


def _sc_row_gather(table, idx):
    n, d = idx.shape[0], table.shape[1]
    nw = SC_CORES * SC_SUBCORES
    per_w = n // nw
    rows = SC_GATHER_ROWS
    assert n % (nw * 2 * rows) == 0
    n_pairs = per_w // (2 * rows)
    mesh = plsc.VectorSubcoreMesh(core_axis_name="c", subcore_axis_name="s",
                                  num_cores=SC_CORES, num_subcores=SC_SUBCORES)

    @functools.partial(
        pl.kernel, mesh=mesh, out_type=jax.ShapeDtypeStruct((n, d), table.dtype),
        scratch_types=[pltpu.VMEM((rows,), I32), pltpu.VMEM((rows,), I32),
                       pltpu.VMEM((rows, d), table.dtype), pltpu.VMEM((rows, d), table.dtype),
                       pltpu.SemaphoreType.DMA, pltpu.SemaphoreType.DMA],
        name="moe_gather_sc")
    def gather(table_hbm, idx_hbm, out_hbm, idx_a, idx_b, rows_a, rows_b, sem_a, sem_b):
        wid = lax.axis_index("s") * SC_CORES + lax.axis_index("c")
        base = wid * per_w
        bufs = ((idx_a, rows_a, sem_a), (idx_b, rows_b, sem_b))

        def start(c, buf):
            idx_v, rows_v, sem = buf
            off = pl.multiple_of(base + c * rows, rows)
            pltpu.sync_copy(idx_hbm.at[pl.ds(off, rows)], idx_v)
            pltpu.async_copy(table_hbm.at[idx_v], rows_v, sem)

        def finish(c, buf):
            idx_v, rows_v, sem = buf
            off = pl.multiple_of(base + c * rows, rows)
            pltpu.make_async_copy(table_hbm.at[idx_v], rows_v, sem).wait()
            pltpu.sync_copy(rows_v, out_hbm.at[pl.ds(off, rows)])

        start(0, bufs[0])

        def body(p, carry):
            c = 2 * p
            start(c + 1, bufs[1])
            finish(c, bufs[0])

            @pl.when(p + 1 < n_pairs)
            def _():
                start(c + 2, bufs[0])

            finish(c + 1, bufs[1])
            return carry

        lax.fori_loop(0, n_pairs, body, 0)

    return gather(table, idx)


def _combine_kernel(x1_ref, gate_ref, yk_ref, wsg_ref, wsu_ref, wsd_ref, g2_ref, b2_ref, yp_ref, ysmp_ref,
                    *, n_prompt_tiles):
    i = pl.program_id(0)
    tm = x1_ref.shape[0]
    x1 = x1_ref[...]
    xb = x1.astype(BF16)
    g = _dot(xb, wsg_ref[...])
    u = _dot(xb, wsu_ref[...])
    ff = _dot((g * jax.nn.sigmoid(g) * u).astype(BF16), wsd_ref[...])
    gate = gate_ref[...]
    for kk in range(TOP_K):
        ff = ff + gate[:, kk:kk + 1] * _unpack_bf16_pairs(pltpu.bitcast(yk_ref[kk * tm:(kk + 1) * tm, :], U32))
    y = _layer_norm(ALPHA * x1 + ff, g2_ref[...], b2_ref[...])

    @pl.when(i < n_prompt_tiles)
    def _():
        yp_ref[0] = y

    @pl.when(i >= n_prompt_tiles)
    def _():
        ysmp_ref[...] = y


def _moe_combine(dest, x1, gate, ys, wsg, wsu, wsd, ln2_g, ln2_b, geo, seq):
    R = x1.shape[0]
    nt = R // ROW_TILE
    W = ys.shape[1]
    B, tpb, npt = geo["B"], geo["tpb"], geo["npt"]
    assert seq == (tpb - 1) * ROW_TILE
    order = jnp.transpose(dest.reshape(nt, ROW_TILE, TOP_K), (0, 2, 1)).reshape(R * TOP_K)
    yk = _sc_row_gather(ys, order)

    def rows(w):
        return pl.BlockSpec((ROW_TILE, w), lambda i: (i, 0))

    def full(a):
        return pl.BlockSpec(a.shape, lambda i: (0,) * a.ndim)

    def rows_k(w):
        return pl.BlockSpec((TOP_K * ROW_TILE, w), lambda i: (i, 0))

    def yp_map(i):
        in_prompt = i < npt
        return (jnp.minimum(i // tpb, B - 1), jnp.where(in_prompt, jnp.maximum(i % tpb - 1, 0), tpb - 2), 0)

    return pl.pallas_call(
        functools.partial(_combine_kernel, n_prompt_tiles=npt),
        grid=(nt,),
        in_specs=[rows(D_MODEL), rows(LANES), rows_k(W),
                  full(wsg), full(wsu), full(wsd), full(ln2_g), full(ln2_b)],
        out_specs=[pl.BlockSpec((1, ROW_TILE, D_MODEL), yp_map),
                   pl.BlockSpec((ROW_TILE, D_MODEL), lambda i: (jnp.maximum(i - npt, 0), 0))],
        out_shape=[jax.ShapeDtypeStruct((B, seq, D_MODEL), F32),
                   jax.ShapeDtypeStruct((R - npt * ROW_TILE, D_MODEL), F32)],
        compiler_params=_cparams("arbitrary"),
        name="moe_combine",
    )(x1, gate, yk, wsg, wsu, wsd, ln2_g, ln2_b)


def _routing_tables(idx8, rank8, counts, n_slots_blocks):
    R = idx8.shape[0]
    padded = (counts + MOE_BLOCK - 1) // MOE_BLOCK * MOE_BLOCK
    pends = jnp.cumsum(padded)
    pstarts = pends - padded
    experts = jnp.arange(N_EXPERTS, dtype=I32)
    start8 = jnp.sum(jnp.where(idx8[:, :, None] == experts[None, None, :], pstarts[None, None, :], 0), axis=-1)
    dest = start8 + rank8
    blk_start = jnp.arange(n_slots_blocks, dtype=I32) * MOE_BLOCK
    block_expert = jnp.minimum(jnp.sum((blk_start[:, None] >= pends[None, :]).astype(I32), axis=1), N_EXPERTS - 1)
    n_used = (pends[-1] // MOE_BLOCK).astype(I32).reshape(1)
    hit = block_expert[:, None] == experts[None, :]
    left = jnp.sum(jnp.where(hit, (counts + pstarts)[None, :], 0), axis=1) - blk_start
    n_valid = jnp.clip(left, 0, MOE_BLOCK).astype(I32)
    return dest.astype(I32), block_expert.astype(I32), n_used, n_valid


def kernel(x_prompt, x_sample, cache_mla_ckv, cache_mla_krope, state_hgrn, page_table, meta_tokens, hg_lb_logits,
           w_in, q_norm, kv_norm, w_uq, w_uk, w_uv, hg_norm, w_br_mla, w_br_hg, w_out, ln1_g, ln1_b, router_w,
           router_bias, w_exp_gate, w_exp_up, w_exp_down, w_sh_gate, w_sh_up, w_sh_down, ln2_g, ln2_b):
    assert w_in.shape[0] == DEPTH
    B, seq, _ = x_prompt.shape
    DB, steps, _ = x_sample.shape
    n_meta = meta_tokens.shape[0]
    n_pages, page = page_table.shape[1], cache_mla_ckv.shape[2]
    past = n_pages * page
    T = n_meta + seq
    tp = -(-T // ROW_TILE) * ROW_TILE
    pad_front = tp - T
    tpb = tp // ROW_TILE
    n_s = DB * steps
    assert n_s % ROW_TILE == 0 and DB % HG_SAMPLE_BATCH == 0
    geo = dict(B=B, tp=tp, tpb=tpb, pad_front=pad_front, npt=B * tpb)
    Rp = B * tp
    R = Rp + n_s

    assert pad_front + n_meta == ROW_TILE and seq % ROW_TILE == 0
    x_head = jnp.concatenate([jnp.zeros((pad_front, D_MODEL), F32), meta_tokens.astype(F32)], axis=0)
    x_smp = x_sample.reshape(n_s, D_MODEL)

    pos_p = jnp.maximum(jnp.arange(tp) - pad_front, 0)
    pos = jnp.concatenate([jnp.tile(pos_p, B), jnp.tile(past + jnp.arange(steps), DB)]).astype(F32)
    half = MLA_ROPE // 2
    inv = ROPE_THETA ** (-jnp.arange(half, dtype=F32) / half)
    ang = pos[:, None] * inv[None, :]
    cos8 = jnp.tile(jnp.cos(ang), (1, LANES // half))
    sin8 = jnp.tile(jnp.sin(ang), (1, LANES // half))

    l = 0
    win = w_in[l]
    kr_end = MLA_Q_RANK + MLA_KV_RANK + MLA_ROPE
    w_in_p = jnp.concatenate([win[:, :kr_end], jnp.zeros((D_MODEL, LANES - MLA_ROPE), F32), win[:, kr_end:]],
                             axis=1).astype(BF16)
    assert w_in_p.shape[1] == D_IN_PACKED
    wq3 = w_uq[l].reshape(MLA_Q_RANK, MLA_HEADS, MLA_NOPE + MLA_ROPE)
    wqr = (jnp.concatenate([wq3[:, :, MLA_NOPE:MLA_NOPE + half].reshape(MLA_Q_RANK, MLA_HEADS * half),
                            wq3[:, :, MLA_NOPE + half:].reshape(MLA_Q_RANK, MLA_HEADS * half)], axis=1)
           * Q_SCALE).astype(BF16)
    src = jnp.arange(2 * LANES)
    hh, ii = (src % LANES) // half, src % half
    dst = hh * LANES + ii + jnp.where(src >= LANES, half, 0)
    perm = (dst[:, None] == jnp.arange(MLA_HEADS * LANES)[None, :]).astype(BF16)
    lb = jnp.cumsum(jax.nn.softmax(hg_lb_logits.astype(F32), axis=0), axis=0)[l].reshape(1, -1)

    wqlat, wo = _weight_prep(w_uq[l], w_uk[l], w_uv[l], w_br_mla[l])

    (qcat, kcat, ckv, kr, hq, hk, hv, lf, sg, sm, sh) = _in_projection(
        x_prompt, x_head, x_smp, w_in_p, q_norm[l].reshape(1, -1), kv_norm[l].reshape(1, -1), wqlat, wqr, perm,
        cos8, sin8, lb, geo)

    o_lat_p = _prompt_attention(qcat, kcat, geo)
    q_s = qcat[:, Rp:].reshape(MLA_HEADS, DB, steps, QCAT)
    knew_s = kcat[Rp:].reshape(DB, steps, 2 * LANES)
    cache_krt = jnp.swapaxes(cache_mla_krope[l], 1, 2)
    o_lat_s = _sample_attention(page_table, q_s, knew_s, cache_mla_ckv[l], cache_krt)
    o_lat_s = o_lat_s.reshape(n_s, MLA_HEADS * LANES)
    o_hg_p, s_prompt = _hgrn_prompt(hq, hk, hv, lf, geo)
    o_hg_s, s_sample = _hgrn_sample(hq, hk, hv, lf, state_hgrn[l], Rp, steps)

    rw_f = jnp.concatenate([router_w[l], jnp.zeros((D_MODEL, LANES - N_EXPERTS), F32)], axis=1)
    rw_hi = rw_f.astype(BF16)
    rw_p = jnp.stack([rw_hi, (rw_f - rw_hi.astype(F32)).astype(BF16)])
    rb_p = jnp.concatenate([router_bias[l].astype(F32), jnp.zeros((LANES - N_EXPERTS,), F32)]).reshape(1, -1)
    x1, x1p, idx, gate, rank, count = _layer_tail(
        x_prompt, x_head, x_smp, o_lat_p, o_lat_s, o_hg_p, o_hg_s, sg, sm, sh, wo, w_br_hg[l].astype(BF16),
        w_out[l].astype(BF16), hg_norm[l].reshape(1, -1), ln1_g[l].reshape(1, -1), ln1_b[l].reshape(1, -1),
        rw_p, rb_p, geo)

    nblk = R * TOP_K // MOE_BLOCK + N_EXPERTS
    dest, block_expert, n_used, n_valid = _routing_tables(
        idx[:, :TOP_K], rank[:, :TOP_K], count[0, :N_EXPERTS], nblk)
    xs = _sc_row_scatter(x1p, dest, nblk * MOE_BLOCK)
    ys = _moe_experts(block_expert, n_used, n_valid, xs, w_exp_gate[l], w_exp_up[l], w_exp_down[l])
    yp, y_smp = _moe_combine(dest, x1, gate, ys, w_sh_gate[l].astype(BF16), w_sh_up[l].astype(BF16),
                             w_sh_down[l].astype(BF16), ln2_g[l].reshape(1, -1), ln2_b[l].reshape(1, -1), geo, seq)

    ys_out = y_smp.reshape(DB, steps, D_MODEL)
    ckv_p = ckv[:Rp].reshape(B, tp, MLA_KV_RANK)[:, pad_front:][None]
    kr_p = kr[:Rp].reshape(B, tp, LANES)[:, pad_front:, :MLA_ROPE][None]
    ckv_s = ckv[Rp:].reshape(DB, steps, MLA_KV_RANK)[None]
    kr_s = kr[Rp:, :MLA_ROPE].reshape(DB, steps, MLA_ROPE)[None]
    return (yp, ys_out, ckv_p, kr_p, s_prompt[None], ckv_s, kr_s, s_sample[None])
```

```python
import functools

import jax
import jax.numpy as jnp
from jax import lax
from jax.experimental import pallas as pl
from jax.experimental.pallas import tpu as pltpu
from jax.experimental.pallas import tpu_sc as plsc

F32 = jnp.float32
BF16 = jnp.bfloat16
U32 = jnp.uint32
I32 = jnp.int32

D_MODEL = 1024
MLA_HEADS = 8
MLA_Q_RANK = 256
MLA_KV_RANK = 128
MLA_NOPE = 64
MLA_ROPE = 32
MLA_V = 64
MLA_SCALE = (MLA_NOPE + MLA_ROPE) ** -0.5
LOG2E = 1.4426950408889634
Q_SCALE = MLA_SCALE * LOG2E
ROPE_THETA = 10000.0
HG_HEADS = 4
HG_DK = 128
HG_DV = 128
HG_CHUNK = 64
N_EXPERTS = 64
TOP_K = 8
D_EXPERT = 256
ROUTED_SCALE = 2.5
NORM_EPS = 1e-6
DEPTH = 1
ALPHA = (2.0 * DEPTH) ** 0.25

LANES = 128
ROW_TILE = 256
QCAT = 256
ATT_TK = 256
MOE_BLOCK = 512
PAGES_PER_CHUNK = 64
ONES_COL = LANES + MLA_ROPE
HG_SAMPLE_BATCH = 8
SATT_BATCH = 1
SC_CORES = 2
SC_SUBCORES = 16
SC_GATHER_ROWS = 64
SC_SCATTER_TOKENS = 8
NEG_BIG = -1e30
VMEM_LIMIT = 52 * 1024 * 1024

C_CQ = (0, 256)
C_CKV = (256, 384)
C_KR = (384, 512)
C_HQ = (512, 1024)
C_HF = (1024, 1536)
C_HI = (1536, 2048)
C_HG = (2048, 2560)
C_GM = (2560, 3584)
C_GH = (3584, 4608)
D_IN_PACKED = 4608


def _cparams(*sem):
    return pltpu.CompilerParams(dimension_semantics=sem, vmem_limit_bytes=VMEM_LIMIT)


def _dot(a, b):
    return jnp.dot(a, b, preferred_element_type=F32)


def _dot_nt(a, b):
    return lax.dot_general(a, b, (((1,), (1,)), ((), ())), preferred_element_type=F32)


def _dot_tn(a, b):
    return lax.dot_general(a, b, (((0,), (0,)), ((), ())), preferred_element_type=F32)


def _pack_bf16_pairs(x):
    w = x.shape[1] // 2
    bits = pltpu.bitcast(x.astype(BF16).astype(F32), U32)
    return bits[:, w:] | (bits[:, :w] >> 16)


def _unpack_bf16_pairs(p):
    lo = pltpu.bitcast(p << 16, F32)
    hi = pltpu.bitcast(p & jnp.uint32(0xFFFF0000), F32)
    return jnp.concatenate([lo, hi], axis=1)


def _x_specs(geo, seq_tiles):
    B, tpb, npt = geo["B"], geo["tpb"], geo["npt"]

    def xp_map(i):
        return (jnp.minimum(i // tpb, B - 1), jnp.where(i < npt, jnp.maximum(i % tpb - 1, 0), seq_tiles - 1), 0)

    return [pl.BlockSpec((1, ROW_TILE, D_MODEL), xp_map),
            pl.BlockSpec((ROW_TILE, D_MODEL), lambda i: (0, 0)),
            pl.BlockSpec((ROW_TILE, D_MODEL), lambda i: (jnp.maximum(i - npt, 0), 0))]


def _select_x(i, xp_ref, xh_ref, xs_ref, tiles_per_batch, n_prompt_tiles):
    is_head = (i < n_prompt_tiles) & (i % tiles_per_batch == 0)
    return jnp.where(i >= n_prompt_tiles, xs_ref[...], jnp.where(is_head, xh_ref[...], xp_ref[0]))


def _split_specs(width, npt):
    return [pl.BlockSpec((ROW_TILE, width), lambda i: (jnp.minimum(i, npt - 1), 0)),
            pl.BlockSpec((ROW_TILE, width), lambda i: (jnp.maximum(i - npt, 0), 0))]


def _wprep_kernel(wq_nope_ref, wuk_ref, wuv_ref, wbr_ref, wqlat_ref, wo_ref):
    hp = lax.Precision.HIGHEST
    for h in range(MLA_HEADS):
        a = wq_nope_ref[h]
        b = wuk_ref[h]
        ql = lax.dot_general(a, b, (((1,), (1,)), ((), ())), precision=hp, preferred_element_type=F32)
        wqlat_ref[:, h * MLA_KV_RANK:(h + 1) * MLA_KV_RANK] = (ql * Q_SCALE).astype(BF16)
        c = wuv_ref[h]
        d = wbr_ref[h]
        wo_ref[h * MLA_KV_RANK:(h + 1) * MLA_KV_RANK, :] = jnp.dot(
            c, d, precision=hp, preferred_element_type=F32).astype(BF16)


def _weight_prep(w_uq, w_uk, w_uv, w_br_mla):
    wq3 = w_uq.reshape(MLA_Q_RANK, MLA_HEADS, MLA_NOPE + MLA_ROPE)
    wq_nope = jnp.transpose(wq3[:, :, :MLA_NOPE], (1, 0, 2))
    wuk = jnp.transpose(w_uk, (1, 0, 2))
    wuv = jnp.transpose(w_uv, (1, 0, 2))
    wbr = w_br_mla.reshape(MLA_HEADS, MLA_V, D_MODEL)
    return pl.pallas_call(
        _wprep_kernel,
        out_shape=(jax.ShapeDtypeStruct((MLA_Q_RANK, MLA_HEADS * MLA_KV_RANK), BF16),
                   jax.ShapeDtypeStruct((MLA_HEADS * MLA_KV_RANK, D_MODEL), BF16)),
        compiler_params=pltpu.CompilerParams(vmem_limit_bytes=VMEM_LIMIT),
        name="weight_prep",
    )(wq_nope, wuk, wuv, wbr)


def _inproj_kernel(xp_ref, xh_ref, xs_ref, w_ref, qn_ref, kvn_ref, wqlat_ref, wqr_ref, perm_ref, cos_ref, sin_ref,
                   lb_ref, qcat_ref, kcat_ref, ckv_ref, kr_ref, hq_ref, hk_ref, hv_ref, lf_ref, sg_ref, sm_ref,
                   sh_ref, *, pad_front, tiles_per_batch, n_prompt_tiles):
    i = pl.program_id(0)
    tm = xh_ref.shape[0]
    xb = _select_x(i, xp_ref, xh_ref, xs_ref, tiles_per_batch, n_prompt_tiles).astype(BF16)

    def proj(c):
        return _dot(xb, w_ref[:, c[0]:c[1]])

    cos8 = cos_ref[...]
    sin8 = sin_ref[...]

    cq = proj(C_CQ)
    cqn = cq * lax.rsqrt(jnp.mean(cq * cq, axis=-1, keepdims=True) + NORM_EPS) * qn_ref[...]
    cqb = cqn.astype(BF16)
    qlat = _dot(cqb, wqlat_ref[...])
    qr = _dot(cqb, wqr_ref[...])
    x1, x2 = qr[:, :LANES], qr[:, LANES:]
    qrot = jnp.concatenate([x1 * cos8 - x2 * sin8, x2 * cos8 + x1 * sin8], axis=1).astype(BF16)
    qrh = _dot(qrot, perm_ref[...])
    for h in range(MLA_HEADS):
        qcat_ref[h, :, :LANES] = qlat[:, h * LANES:(h + 1) * LANES].astype(BF16)
        qcat_ref[h, :, LANES:] = qrh[:, h * LANES:(h + 1) * LANES].astype(BF16)

    kv = proj(C_CKV)
    ckv = kv * lax.rsqrt(jnp.mean(kv * kv, axis=-1, keepdims=True) + NORM_EPS) * kvn_ref[...]
    ckv_ref[...] = ckv
    kcat_ref[:, :LANES] = ckv.astype(BF16)
    krr = proj(C_KR)
    lane = lax.broadcasted_iota(I32, (tm, LANES), 1)
    half = MLA_ROPE // 2
    rot = jnp.where(lane < half, -pltpu.roll(krr, LANES - half, 1), pltpu.roll(krr, half, 1))
    kr = jnp.where(lane < MLA_ROPE, krr * cos8 + rot * sin8, 0.0)
    kr_ref[...] = kr
    kcat_ref[:, LANES:] = jnp.where(lane == ONES_COL - LANES, 1.0, kr).astype(BF16)

    row = lax.broadcasted_iota(I32, (tm, 1), 0)
    is_pad = (i < n_prompt_tiles) & (i % tiles_per_batch == 0) & (row < pad_front)
    keep = jnp.where(is_pad, 0.0, 1.0)
    hq_ref[...] = proj(C_HQ).astype(BF16)
    lb = lb_ref[...]
    f = lb + (1.0 - lb) * jax.nn.sigmoid(proj(C_HF))
    lf_ref[...] = jnp.log(f) * keep
    hk_ref[...] = ((1.0 - f) * keep).astype(BF16)
    hv_ref[...] = proj(C_HI).astype(BF16)

    g = proj(C_HG)
    sg_ref[...] = (g * jax.nn.sigmoid(g)).astype(BF16)
    sm_ref[...] = jax.nn.sigmoid(proj(C_GM)).astype(BF16)
    sh_ref[...] = jax.nn.sigmoid(proj(C_GH)).astype(BF16)


def _in_projection(x_prompt, x_head, x_smp, w_in_p, q_norm, kv_norm, wqlat, wqr, perm, cos8, sin8, lb, geo):
    R = cos8.shape[0]
    nt = R // ROW_TILE
    hgw = HG_HEADS * HG_DK
    x_specs = _x_specs(geo, x_prompt.shape[1] // ROW_TILE)

    def rows(w):
        return pl.BlockSpec((ROW_TILE, w), lambda i: (i, 0))

    def full(a):
        return pl.BlockSpec(a.shape, lambda i: (0,) * a.ndim)

    out_widths = [(2 * LANES, BF16), (LANES, F32), (LANES, F32),
                  (hgw, BF16), (hgw, BF16), (hgw, BF16), (hgw, F32), (hgw, BF16),
                  (D_MODEL, BF16), (D_MODEL, BF16)]
    qcat_spec = pl.BlockSpec((MLA_HEADS, ROW_TILE, QCAT), lambda i: (0, i, 0))
    return pl.pallas_call(
        functools.partial(_inproj_kernel, pad_front=geo["pad_front"], tiles_per_batch=geo["tpb"],
                          n_prompt_tiles=geo["npt"]),
        grid=(nt,),
        in_specs=x_specs + [full(w_in_p), full(q_norm), full(kv_norm), full(wqlat), full(wqr), full(perm),
                            rows(LANES), rows(LANES), full(lb)],
        out_specs=[qcat_spec] + [rows(w) for w, _ in out_widths],
        out_shape=[jax.ShapeDtypeStruct((MLA_HEADS, R, QCAT), BF16)]
        + [jax.ShapeDtypeStruct((R, w), dt) for w, dt in out_widths],
        compiler_params=_cparams("arbitrary"),
        name="in_projection",
    )(x_prompt, x_head, x_smp, w_in_p, q_norm, kv_norm, wqlat, wqr, perm, cos8, sin8, lb)


def _softmax_step(s, v_b, m_ref, l_ref, acc_ref):
    n = s.shape[1] // LANES
    m_prev = m_ref[...]
    m_next = jnp.maximum(m_prev, jnp.max(s, axis=1, keepdims=True))
    p = jnp.concatenate([jnp.exp2(s[:, j * LANES:(j + 1) * LANES] - m_next) for j in range(n)], axis=1)
    alpha = jnp.exp2(m_prev - m_next)
    pv = _dot(p.astype(BF16), v_b)
    if v_b.shape[1] == LANES:
        l_ref[...] = alpha * l_ref[...] + jnp.sum(p, axis=1, keepdims=True)
    else:
        l_ref[...] = alpha * l_ref[...] + pv[:, LANES:]
    acc_ref[...] = alpha * acc_ref[...] + pv[:, :LANES]
    m_ref[...] = m_next


def _pattn_kernel(q_ref, k_ref, o_ref, m_ref, l_ref, acc_ref, *, pad_front):
    qi = pl.program_id(1)
    nh, tq, _ = q_ref.shape
    tk = ATT_TK
    rows = nh * tq

    q = q_ref[...].reshape(rows, QCAT)
    m_ref[...] = jnp.full(m_ref.shape, -jnp.inf, F32)
    l_ref[...] = jnp.zeros(l_ref.shape, F32)
    acc_ref[...] = jnp.zeros(acc_ref.shape, F32)

    def step(kb, masked):
        kblk = k_ref[pl.ds(pl.multiple_of(kb * tk, tk), tk), :]
        s = _dot_nt(q, kblk)
        if masked:
            qrow = qi * tq + lax.broadcasted_iota(I32, (rows, tk), 0) % tq
            krow = kb * tk + lax.broadcasted_iota(I32, (rows, tk), 1)
            s = jnp.where((krow <= qrow) & (krow >= pad_front), s, NEG_BIG)
        _softmax_step(s, kblk, m_ref, l_ref, acc_ref)

    step(0, True)

    def body(kb, c):
        step(kb, False)
        return c

    lax.fori_loop(1, qi, body, 0)

    @pl.when(qi > 0)
    def _():
        step(qi, True)

    row_sum = l_ref[:, ONES_COL - LANES:ONES_COL - LANES + 1]
    o = (acc_ref[...] / row_sum).astype(BF16)
    for h in range(nh):
        o_ref[:, h * LANES:(h + 1) * LANES] = o[h * tq:(h + 1) * tq]


def _prompt_attention(qcat, kcat, geo):
    B, tpb, tp = geo["B"], geo["tpb"], geo["tp"]
    rows = MLA_HEADS * ROW_TILE
    return pl.pallas_call(
        functools.partial(_pattn_kernel, pad_front=geo["pad_front"]),
        grid=(B, tpb),
        in_specs=[pl.BlockSpec((MLA_HEADS, ROW_TILE, QCAT), lambda b, i: (0, b * tpb + i, 0)),
                  pl.BlockSpec((tp, 2 * LANES), lambda b, i: (b, 0))],
        out_specs=pl.BlockSpec((ROW_TILE, MLA_HEADS * LANES), lambda b, i: (b * tpb + i, 0)),
        out_shape=jax.ShapeDtypeStruct((B * tp, MLA_HEADS * LANES), BF16),
        scratch_shapes=[pltpu.VMEM((rows, LANES), F32)] * 3,
        compiler_params=_cparams("arbitrary", "arbitrary"),
        name="prompt_attention",
    )(qcat, kcat)


def _sattn_kernel(pt_ref, q_ref, knew_ref, ckv_hbm, kr_hbm, o_ref,
                  ckv_buf, kr_buf, sem, m_ref, l_ref, acc_ref, *, n_chunks, page):
    g = pl.program_id(0)
    ng = pl.num_programs(0)
    ch = PAGES_PER_CHUNK
    nh, nbt, S = q_ref.shape[0], q_ref.shape[1], q_ref.shape[2]
    rows = nh * S

    def chunk_copies(gg, c, slot):
        cps = []
        for bb in range(nbt):
            for j in range(ch):
                pg = pt_ref[gg * nbt + bb, c * ch + j]
                cps.append(pltpu.make_async_copy(ckv_hbm.at[pg], ckv_buf.at[slot, bb, pl.ds(j * page, page), :],
                                                 sem.at[0, slot]))
                cps.append(pltpu.make_async_copy(kr_hbm.at[pg], kr_buf.at[slot, bb, :, pl.ds(j * page, page)],
                                                 sem.at[1, slot]))
        return cps

    @pl.when(g == 0)
    def _():
        for cp in chunk_copies(0, 0, 0):
            cp.start()

    qlat, qrope = [], []
    for bb in range(nbt):
        q = q_ref[:, bb].reshape(rows, QCAT)
        qlat.append(q[:, :LANES])
        qrope.append(q[:, LANES:LANES + MLA_ROPE])
        knew = knew_ref[bb]
        s_new = _dot_nt(q, knew)
        qtok = lax.broadcasted_iota(I32, (rows, S), 0) % S
        ktok = lax.broadcasted_iota(I32, (rows, S), 1)
        s_new = jnp.where(ktok <= qtok, s_new, NEG_BIG)
        m0 = jnp.max(s_new, axis=1, keepdims=True)
        p0 = jnp.exp2(s_new - m0)
        m_ref[bb] = jnp.broadcast_to(m0, (rows, LANES))
        l_ref[bb] = jnp.broadcast_to(jnp.sum(p0, axis=1, keepdims=True), (rows, LANES))
        acc_ref[bb] = _dot(p0.astype(BF16), knew[:, :LANES])

    for c in range(n_chunks):
        slot = c % 2 if n_chunks % 2 == 0 else (g * n_chunks + c) % 2
        if c + 1 < n_chunks:
            for cp in chunk_copies(g, c + 1, 1 - slot):
                cp.start()
        else:
            @pl.when(g + 1 < ng)
            def _():
                for cp in chunk_copies(g + 1, 0, 1 - slot):
                    cp.start()
        for cp in chunk_copies(g, c, slot):
            cp.wait()
        for bb in range(nbt):
            ckv_b = ckv_buf[slot, bb].astype(BF16)
            krt_b = kr_buf[slot, bb].astype(BF16)
            s = _dot_nt(qlat[bb], ckv_b) + _dot(qrope[bb], krt_b)
            _softmax_step(s, ckv_b, m_ref.at[bb], l_ref.at[bb], acc_ref.at[bb])

    for bb in range(nbt):
        o = (acc_ref[bb] / l_ref[bb]).astype(BF16)
        for h in range(nh):
            o_ref[bb, :, h * LANES:(h + 1) * LANES] = o[h * S:(h + 1) * S]


def _sample_attention(page_table, q_s, knew_s, cache_ckv, cache_krt):
    nh, DB, S, _ = q_s.shape
    rows = nh * S
    n_pages = page_table.shape[1]
    page = cache_ckv.shape[1]
    assert n_pages % PAGES_PER_CHUNK == 0
    n_chunks = n_pages // PAGES_PER_CHUNK
    ck = PAGES_PER_CHUNK * page
    nbt = SATT_BATCH
    assert DB % nbt == 0
    grid_spec = pltpu.PrefetchScalarGridSpec(
        num_scalar_prefetch=1,
        grid=(DB // nbt,),
        in_specs=[pl.BlockSpec((nh, nbt, S, QCAT), lambda g, pt: (0, g, 0, 0)),
                  pl.BlockSpec((nbt, S, 2 * LANES), lambda g, pt: (g, 0, 0)),
                  pl.BlockSpec(memory_space=pl.ANY),
                  pl.BlockSpec(memory_space=pl.ANY)],
        out_specs=pl.BlockSpec((nbt, S, nh * LANES), lambda g, pt: (g, 0, 0)),
        scratch_shapes=[pltpu.VMEM((2, nbt, ck, MLA_KV_RANK), F32),
                        pltpu.VMEM((2, nbt, MLA_ROPE, ck), F32),
                        pltpu.SemaphoreType.DMA((2, 2)),
                        pltpu.VMEM((nbt, rows, LANES), F32),
                        pltpu.VMEM((nbt, rows, LANES), F32),
                        pltpu.VMEM((nbt, rows, LANES), F32)])
    return pl.pallas_call(
        functools.partial(_sattn_kernel, n_chunks=n_chunks, page=page),
        grid_spec=grid_spec,
        out_shape=jax.ShapeDtypeStruct((DB, S, nh * LANES), BF16),
        compiler_params=_cparams("arbitrary"),
        name="sample_attention",
    )(page_table, q_s, knew_s, cache_ckv, cache_krt)


def _split3(x):
    hi = x.astype(BF16)
    r1 = x - hi.astype(F32)
    mid = r1.astype(BF16)
    lo = (r1 - mid.astype(F32)).astype(BF16)
    return hi, mid, lo


def _hgrn_chunk(q, k, v, lf, S0, tri, group):
    C = q.shape[0]
    hi, mid, lo = _split3(lf)
    cum = _dot(tri, hi) + _dot(tri, mid) + _dot(tri, lo)
    vb = v.astype(BF16)
    rowi = lax.broadcasted_iota(I32, (C, C), 0)
    coli = lax.broadcasted_iota(I32, (C, C), 1)
    attn = jnp.zeros((C, C), F32)

    bd = min(16, group)
    hs = group // 2
    rid = lax.broadcasted_iota(I32, (C, 1), 0)
    while hs >= bd:
        npair = C // (2 * hs)
        ref = jnp.concatenate(
            [jnp.broadcast_to(cum[(2 * j + 1) * hs - 1:(2 * j + 1) * hs, :], (2 * hs, cum.shape[1]))
             for j in range(npair)], axis=0)
        odd = ((rid // hs) % 2) == 1
        e = jnp.exp(jnp.where(odd, cum - ref, ref - cum))
        qs = jnp.where(odd, q * e, 0.0).astype(BF16)
        ks = jnp.where(odd, 0.0, k * e).astype(BF16)
        a = _dot_nt(qs, ks)
        attn = attn + jnp.where((rowi // (2 * hs)) == (coli // (2 * hs)), a, 0.0)
        hs //= 2

    nblk = C // bd
    k3 = k.reshape(nblk, bd, k.shape[1])
    c3 = cum.reshape(nblk, bd, cum.shape[1])
    tl = rid % bd
    blk_base = (rowi // bd) * bd
    for sl in range(bd):
        ks_b = jnp.broadcast_to(k3[:, sl:sl + 1, :], k3.shape).reshape(C, k.shape[1])
        cs_b = jnp.broadcast_to(c3[:, sl:sl + 1, :], c3.shape).reshape(C, k.shape[1])
        e = jnp.exp(jnp.where(tl >= sl, cum - cs_b, NEG_BIG))
        col = jnp.sum(q * ks_b * e, axis=1, keepdims=True)
        attn = jnp.where(coli == blk_base + sl, col, attn)
    o_intra = _dot(attn.astype(BF16), vb)
    return o_intra, cum


def _tri(C, group):
    r = lax.broadcasted_iota(I32, (C, C), 0)
    c = lax.broadcasted_iota(I32, (C, C), 1)
    return jnp.where((c <= r) & (r // group == c // group), 1.0, 0.0).astype(BF16)


def _state_update(q, k, v, cum, S, o_intra):
    C = q.shape[0]
    last = cum[C - 1:C, :]
    o = o_intra + _dot((q * jnp.exp(cum)).astype(BF16), S.astype(BF16))
    kst = (k * jnp.exp(last - cum)).astype(BF16)
    dfull = jnp.transpose(jnp.broadcast_to(jnp.exp(last), (S.shape[1], S.shape[0])))
    S_new = dfull * S + _dot_tn(kst, v.astype(BF16))
    return o, S_new


def _hgrn_prompt_kernel(q_ref, k_ref, v_ref, lf_ref, o_ref, s_out_ref, s_ref):
    t = pl.program_id(1)

    @pl.when(t == 0)
    def _():
        s_ref[...] = jnp.zeros(s_ref.shape, F32)

    C = HG_CHUNK
    tri = _tri(C, C)
    states = [s_ref[h] for h in range(HG_HEADS)]
    for c in range(q_ref.shape[0] // C):
        sl = slice(c * C, (c + 1) * C)
        for h in range(HG_HEADS):
            hl = slice(h * HG_DK, (h + 1) * HG_DK)
            q = q_ref[sl, hl].astype(F32)
            k = k_ref[sl, hl].astype(F32)
            v = v_ref[sl, hl].astype(F32)
            lf = lf_ref[sl, hl]
            o_intra, cum = _hgrn_chunk(q, k, v, lf, None, tri, C)
            o, states[h] = _state_update(q, k, v, cum, states[h], o_intra)
            o_ref[sl, hl] = o.astype(BF16)
    for h in range(HG_HEADS):
        s_ref[h] = states[h]

    @pl.when(t == pl.num_programs(1) - 1)
    def _():
        s_out_ref[0] = s_ref[...]


def _hgrn_prompt(hq, hk, hv, lf, geo):
    B, tpb, tp = geo["B"], geo["tpb"], geo["tp"]
    hgw = HG_HEADS * HG_DK

    def blk():
        return pl.BlockSpec((ROW_TILE, hgw), lambda b, t: (b * tpb + t, 0))

    return pl.pallas_call(
        _hgrn_prompt_kernel,
        grid=(B, tpb),
        in_specs=[blk(), blk(), blk(), blk()],
        out_specs=[blk(), pl.BlockSpec((1, HG_HEADS, HG_DK, HG_DV), lambda b, t: (b, 0, 0, 0))],
        out_shape=[jax.ShapeDtypeStruct((B * tp, hgw), BF16),
                   jax.ShapeDtypeStruct((B, HG_HEADS, HG_DK, HG_DV), F32)],
        scratch_shapes=[pltpu.VMEM((HG_HEADS, HG_DK, HG_DV), F32)],
        compiler_params=_cparams("arbitrary", "arbitrary"),
        name="hgrn_prompt",
    )(hq, hk, hv, lf)


def _hgrn_sample_kernel(q_ref, k_ref, v_ref, lf_ref, s_in_ref, o_ref, s_out_ref, *, steps):
    nb = s_in_ref.shape[0]
    C = nb * steps
    tri = _tri(C, steps)
    for h in range(HG_HEADS):
        sl = slice(h * HG_DK, (h + 1) * HG_DK)
        q = q_ref[:, sl].astype(F32)
        k = k_ref[:, sl].astype(F32)
        v = v_ref[:, sl].astype(F32)
        lf = lf_ref[:, sl]
        o_intra, cum = _hgrn_chunk(q, k, v, lf, None, tri, steps)
        for b in range(nb):
            r = slice(b * steps, (b + 1) * steps)
            o, S_new = _state_update(q[r], k[r], v[r], cum[r], s_in_ref[b, h], o_intra[r])
            o_ref[r, sl] = o.astype(BF16)
            s_out_ref[b, h] = S_new


def _hgrn_sample(hq, hk, hv, lf, state, row0, steps):
    DB = state.shape[0]
    nb = HG_SAMPLE_BATCH
    rows = nb * steps
    hgw = HG_HEADS * HG_DK
    blk0 = row0 // rows

    def tok():
        return pl.BlockSpec((rows, hgw), lambda i: (blk0 + i, 0))

    st = pl.BlockSpec((nb, HG_HEADS, HG_DK, HG_DV), lambda i: (i, 0, 0, 0))
    return pl.pallas_call(
        functools.partial(_hgrn_sample_kernel, steps=steps),
        grid=(DB // nb,),
        in_specs=[tok(), tok(), tok(), tok(), st],
        out_specs=[pl.BlockSpec((rows, hgw), lambda i: (i, 0)), st],
        out_shape=[jax.ShapeDtypeStruct((DB * steps, hgw), BF16),
                   jax.ShapeDtypeStruct(state.shape, F32)],
        compiler_params=_cparams("arbitrary"),
        name="hgrn_sample",
    )(hq, hk, hv, lf, state)


def _layer_norm(x, g, b):
    mu = jnp.mean(x, axis=-1, keepdims=True)
    xc = x - mu
    var = jnp.mean(xc * xc, axis=-1, keepdims=True)
    return xc * lax.rsqrt(var + NORM_EPS) * g + b


def _tail_kernel(xp_ref, xh_ref, xs_ref, olp_ref, ols_ref, ohp_ref, ohs_ref, sg_ref, sm_ref, sh_ref, wo_ref, wbh_ref,
                 wout_ref, hgn_ref, g1_ref, b1_ref, rw_ref, rb_ref, x1_ref, x1p_ref, idx_ref, gate_ref, rank_ref,
                 count_ref, cnt_ref, *, tiles_per_batch, n_prompt_tiles):
    i = pl.program_id(0)
    tm = xh_ref.shape[0]
    x_in = _select_x(i, xp_ref, xh_ref, xs_ref, tiles_per_batch, n_prompt_tiles)
    is_sample = i >= n_prompt_tiles
    mla = _dot(jnp.where(is_sample, ols_ref[...], olp_ref[...]), wo_ref[...])
    oh = jnp.where(is_sample, ohs_ref[...], ohp_ref[...]).astype(F32)
    parts = []
    for h in range(HG_HEADS):
        y = oh[:, h * HG_DV:(h + 1) * HG_DV]
        parts.append(y * lax.rsqrt(jnp.mean(y * y, axis=-1, keepdims=True) + NORM_EPS) * hgn_ref[...])
    hg = (jnp.concatenate(parts, axis=1) * sg_ref[...].astype(F32)).astype(BF16)
    merged = sm_ref[...].astype(F32) * mla + sh_ref[...].astype(F32) * _dot(hg, wbh_ref[...])
    x1 = _layer_norm(ALPHA * x_in + _dot(merged.astype(BF16), wout_ref[...]), g1_ref[...], b1_ref[...])
    x1_ref[...] = x1
    x1p_ref[...] = pltpu.bitcast(_pack_bf16_pairs(x1), I32)

    x1_hi = x1.astype(BF16)
    x1_lo = (x1 - x1_hi.astype(F32)).astype(BF16)
    logits = _dot(x1_hi, rw_ref[0]) + _dot(x1_lo, rw_ref[0]) + _dot(x1_hi, rw_ref[1])
    scores = jax.nn.sigmoid(logits)
    lane = lax.broadcasted_iota(I32, (tm, LANES), 1).astype(F32)
    remaining = jnp.where(lane < N_EXPERTS, scores + rb_ref[...], -jnp.inf)
    idx_out = jnp.zeros((tm, LANES), F32)
    gate_out = jnp.zeros((tm, LANES), F32)
    hits = []
    for kk in range(TOP_K):
        mx = jnp.max(remaining, axis=1, keepdims=True)
        pick = jnp.min(jnp.where(remaining == mx, lane, float(LANES)), axis=1, keepdims=True)
        hit = lane == pick
        hits.append(hit)
        gval = jnp.sum(jnp.where(hit, scores, 0.0), axis=1, keepdims=True)
        idx_out = jnp.where(lane == kk, pick, idx_out)
        gate_out = jnp.where(lane == kk, gval, gate_out)
        remaining = jnp.where(hit, -jnp.inf, remaining)
    gate_out = gate_out / jnp.sum(gate_out, axis=1, keepdims=True) * ROUTED_SCALE
    idx_ref[...] = idx_out.astype(I32)
    gate_ref[...] = gate_out

    @pl.when(pl.program_id(0) == 0)
    def _():
        cnt_ref[...] = jnp.zeros(cnt_ref.shape, F32)

    sel = jnp.where(remaining == -jnp.inf, 1.0, 0.0) * jnp.where(lane < N_EXPERTS, 1.0, 0.0)
    r_i = lax.broadcasted_iota(I32, (tm, tm), 0)
    c_i = lax.broadcasted_iota(I32, (tm, tm), 1)
    before = _dot(jnp.where(c_i < r_i, 1.0, 0.0).astype(BF16), sel.astype(BF16)) + cnt_ref[0:1, :]
    rank_out = jnp.zeros((tm, LANES), F32)
    for kk in range(TOP_K):
        rk = jnp.sum(jnp.where(hits[kk], before, 0.0), axis=1, keepdims=True)
        rank_out = jnp.where(lane == kk, rk, rank_out)
    rank_ref[...] = rank_out.astype(I32)
    total = cnt_ref[0:1, :] + jnp.sum(sel, axis=0, keepdims=True)
    cnt_ref[...] = jnp.broadcast_to(total, cnt_ref.shape)
    count_ref[...] = jnp.broadcast_to(total, count_ref.shape).astype(I32)


def _layer_tail(x_prompt, x_head, x_smp, o_lat_p, o_lat_s, o_hg_p, o_hg_s, sg, sm, sh, wo, w_br_hg, w_out, hg_norm,
                ln1_g, ln1_b, rw_p, rb_p, geo):
    R = sg.shape[0]
    npt = geo["npt"]

    def rows(w):
        return pl.BlockSpec((ROW_TILE, w), lambda i: (i, 0))

    def full(a):
        return pl.BlockSpec(a.shape, lambda i: (0,) * a.ndim)

    hgw = HG_HEADS * HG_DV
    return pl.pallas_call(
        functools.partial(_tail_kernel, tiles_per_batch=geo["tpb"], n_prompt_tiles=npt),
        grid=(R // ROW_TILE,),
        in_specs=_x_specs(geo, x_prompt.shape[1] // ROW_TILE)
        + _split_specs(MLA_HEADS * LANES, npt) + _split_specs(hgw, npt)
        + [rows(hgw), rows(D_MODEL), rows(D_MODEL),
           full(wo), full(w_br_hg), full(w_out), full(hg_norm), full(ln1_g), full(ln1_b),
           full(rw_p), full(rb_p)],
        out_specs=[rows(D_MODEL), rows(D_MODEL // 2), rows(LANES), rows(LANES), rows(LANES),
                   pl.BlockSpec((8, LANES), lambda i: (0, 0))],
        out_shape=[jax.ShapeDtypeStruct((R, D_MODEL), F32), jax.ShapeDtypeStruct((R, D_MODEL // 2), I32),
                   jax.ShapeDtypeStruct((R, LANES), I32), jax.ShapeDtypeStruct((R, LANES), F32),
                   jax.ShapeDtypeStruct((R, LANES), I32), jax.ShapeDtypeStruct((8, LANES), I32)],
        scratch_shapes=[pltpu.VMEM((8, LANES), F32)],
        compiler_params=_cparams("arbitrary"),
        name="layer_tail",
    )(x_prompt, x_head, x_smp, o_lat_p, o_lat_s, o_hg_p, o_hg_s, sg, sm, sh, wo, w_br_hg, w_out, hg_norm,
      ln1_g, ln1_b, rw_p, rb_p)


def _sc_row_scatter(x, dest, n_slots):
    R, d = x.shape
    nw = SC_CORES * SC_SUBCORES
    T = SC_SCATTER_TOKENS
    rows = T * TOP_K
    per_w = R // nw
    assert R % (nw * 2 * T) == 0
    n_pairs = per_w // (2 * T)
    order = jnp.transpose(dest.reshape(R // T, T, TOP_K), (0, 2, 1)).reshape(R * TOP_K)
    mesh = plsc.VectorSubcoreMesh(core_axis_name="c", subcore_axis_name="s",
                                  num_cores=SC_CORES, num_subcores=SC_SUBCORES)

    @functools.partial(
        pl.kernel, mesh=mesh, out_type=jax.ShapeDtypeStruct((n_slots, d), x.dtype),
        scratch_types=[pltpu.VMEM((rows,), I32), pltpu.VMEM((rows,), I32),
                       pltpu.VMEM((rows, d), x.dtype), pltpu.VMEM((rows, d), x.dtype),
                       pltpu.SemaphoreType.DMA, pltpu.SemaphoreType.DMA, pltpu.SemaphoreType.DMA],
        name="moe_dispatch_sc")
    def scatter(x_hbm, idx_hbm, out_hbm, idx_a, idx_b, rows_a, rows_b, sem_a, sem_b, sem_out):
        wid = lax.axis_index("s") * SC_CORES + lax.axis_index("c")
        tbase = wid * per_w
        bufs = ((idx_a, rows_a, sem_a), (idx_b, rows_b, sem_b))

        def loads(c, buf):
            idx_v, rows_v, sem = buf
            t0 = pl.multiple_of(tbase + c * T, T)
            off = pl.multiple_of(t0 * TOP_K, rows)
            cps = [pltpu.make_async_copy(idx_hbm.at[pl.ds(off, rows)], idx_v, sem)]
            for k in range(TOP_K):
                cps.append(pltpu.make_async_copy(x_hbm.at[pl.ds(t0, T)], rows_v.at[pl.ds(k * T, T)], sem))
            return cps

        def load(c, buf):
            for cp in loads(c, buf):
                cp.start()

        def send(c, buf):
            idx_v, rows_v, _ = buf
            for cp in loads(c, buf):
                cp.wait()
            pltpu.async_copy(rows_v, out_hbm.at[idx_v], sem_out).wait()

        load(0, bufs[0])

        def body(p, carry):
            c = 2 * p
            load(c + 1, bufs[1])
            send(c, bufs[0])

            @pl.when(p + 1 < n_pairs)
            def _():
                load(c + 2, bufs[0])

            send(c + 1, bufs[1])
            return carry

        lax.fori_loop(0, n_pairs, body, 0)

    return scatter(x, order)


def _expert_kernel(be_ref, nused_ref, nvalid_ref, xs_ref, wg_ref, wu_ref, wd_ref, ys_ref, wgb_ref, wub_ref, wdb_ref):
    i = pl.program_id(0)

    @pl.when((i == 0) | (be_ref[i] != be_ref[jnp.maximum(i - 1, 0)]))
    def _():
        wgb_ref[...] = wg_ref[0].astype(BF16)
        wub_ref[...] = wu_ref[0].astype(BF16)
        wdb_ref[...] = wd_ref[0].astype(BF16)

    @pl.when(i < nused_ref[0])
    def _():
        row = lax.broadcasted_iota(I32, xs_ref.shape, 0)
        xw = jnp.where(row < nvalid_ref[i], xs_ref[...], 0)
        xb = _unpack_bf16_pairs(pltpu.bitcast(xw, U32)).astype(BF16)
        g = _dot(xb, wgb_ref[...])
        u = _dot(xb, wub_ref[...])
        hmid = (g * jax.nn.sigmoid(g) * u).astype(BF16)
        ys_ref[...] = pltpu.bitcast(_pack_bf16_pairs(_dot(hmid, wdb_ref[...])), I32)

    @pl.when(i >= nused_ref[0])
    def _():
        ys_ref[...] = jnp.zeros(ys_ref.shape, I32)


def _moe_experts(block_expert, n_used, n_valid, xs, wg, wu, wd):
    n_slots, W = xs.shape
    nblk = n_slots // MOE_BLOCK

    def xmap(i, be, nu, nv):
        return (jnp.minimum(i, nu[0] - 1), 0)

    def wmap(i, be, nu, nv):
        return (be[i], 0, 0)

    grid_spec = pltpu.PrefetchScalarGridSpec(
        num_scalar_prefetch=3,
        grid=(nblk,),
        in_specs=[pl.BlockSpec((MOE_BLOCK, W), xmap),
                  pl.BlockSpec((1, D_MODEL, D_EXPERT), wmap),
                  pl.BlockSpec((1, D_MODEL, D_EXPERT), wmap),
                  pl.BlockSpec((1, D_EXPERT, D_MODEL), wmap)],
        out_specs=pl.BlockSpec((MOE_BLOCK, W), lambda i, be, nu, nv: (i, 0)),
        scratch_shapes=[pltpu.VMEM((D_MODEL, D_EXPERT), BF16), pltpu.VMEM((D_MODEL, D_EXPERT), BF16),
                        pltpu.VMEM((D_EXPERT, D_MODEL), BF16)])
    return pl.pallas_call(
        _expert_kernel,
        grid_spec=grid_spec,
        out_shape=jax.ShapeDtypeStruct((n_slots, W), I32),
        compiler_params=_cparams("arbitrary"),
        name="moe_experts",
    )(block_expert, n_used, n_valid, xs, wg, wu, wd)


def _sc_row_gather(table, idx):
    n, d = idx.shape[0], table.shape[1]
    nw = SC_CORES * SC_SUBCORES
    per_w = n // nw
    rows = SC_GATHER_ROWS
    assert n % (nw * 2 * rows) == 0
    n_pairs = per_w // (2 * rows)
    mesh = plsc.VectorSubcoreMesh(core_axis_name="c", subcore_axis_name="s",
                                  num_cores=SC_CORES, num_subcores=SC_SUBCORES)

    @functools.partial(
        pl.kernel, mesh=mesh, out_type=jax.ShapeDtypeStruct((n, d), table.dtype),
        scratch_types=[pltpu.VMEM((rows,), I32), pltpu.VMEM((rows,), I32),
                       pltpu.VMEM((rows, d), table.dtype), pltpu.VMEM((rows, d), table.dtype),
                       pltpu.SemaphoreType.DMA, pltpu.SemaphoreType.DMA],
        name="moe_gather_sc")
    def gather(table_hbm, idx_hbm, out_hbm, idx_a, idx_b, rows_a, rows_b, sem_a, sem_b):
        wid = lax.axis_index("s") * SC_CORES + lax.axis_index("c")
        base = wid * per_w
        bufs = ((idx_a, rows_a, sem_a), (idx_b, rows_b, sem_b))

        def start(c, buf):
            idx_v, rows_v, sem = buf
            off = pl.multiple_of(base + c * rows, rows)
            pltpu.sync_copy(idx_hbm.at[pl.ds(off, rows)], idx_v)
            pltpu.async_copy(table_hbm.at[idx_v], rows_v, sem)

        def finish(c, buf):
            idx_v, rows_v, sem = buf
            off = pl.multiple_of(base + c * rows, rows)
            pltpu.make_async_copy(table_hbm.at[idx_v], rows_v, sem).wait()
            pltpu.sync_copy(rows_v, out_hbm.at[pl.ds(off, rows)])

        start(0, bufs[0])

        def body(p, carry):
            c = 2 * p
            start(c + 1, bufs[1])
            finish(c, bufs[0])

            @pl.when(p + 1 < n_pairs)
            def _():
                start(c + 2, bufs[0])

            finish(c + 1, bufs[1])
            return carry

        lax.fori_loop(0, n_pairs, body, 0)

    return gather(table, idx)


def _combine_kernel(x1_ref, gate_ref, yk_ref, wsg_ref, wsu_ref, wsd_ref, g2_ref, b2_ref, yp_ref, ysmp_ref,
                    *, n_prompt_tiles):
    i = pl.program_id(0)
    tm = x1_ref.shape[0]
    x1 = x1_ref[...]
    xb = x1.astype(BF16)
    g = _dot(xb, wsg_ref[...])
    u = _dot(xb, wsu_ref[...])
    ff = _dot((g * jax.nn.sigmoid(g) * u).astype(BF16), wsd_ref[...])
    gate = gate_ref[...]
    for kk in range(TOP_K):
        ff = ff + gate[:, kk:kk + 1] * _unpack_bf16_pairs(pltpu.bitcast(yk_ref[kk * tm:(kk + 1) * tm, :], U32))
    y = _layer_norm(ALPHA * x1 + ff, g2_ref[...], b2_ref[...])

    @pl.when(i < n_prompt_tiles)
    def _():
        yp_ref[0] = y

    @pl.when(i >= n_prompt_tiles)
    def _():
        ysmp_ref[...] = y


def _moe_combine(dest, x1, gate, ys, wsg, wsu, wsd, ln2_g, ln2_b, geo, seq):
    R = x1.shape[0]
    nt = R // ROW_TILE
    W = ys.shape[1]
    B, tpb, npt = geo["B"], geo["tpb"], geo["npt"]
    assert seq == (tpb - 1) * ROW_TILE
    order = jnp.transpose(dest.reshape(nt, ROW_TILE, TOP_K), (0, 2, 1)).reshape(R * TOP_K)
    yk = _sc_row_gather(ys, order)

    def rows(w):
        return pl.BlockSpec((ROW_TILE, w), lambda i: (i, 0))

    def full(a):
        return pl.BlockSpec(a.shape, lambda i: (0,) * a.ndim)

    def rows_k(w):
        return pl.BlockSpec((TOP_K * ROW_TILE, w), lambda i: (i, 0))

    def yp_map(i):
        in_prompt = i < npt
        return (jnp.minimum(i // tpb, B - 1), jnp.where(in_prompt, jnp.maximum(i % tpb - 1, 0), tpb - 2), 0)

    return pl.pallas_call(
        functools.partial(_combine_kernel, n_prompt_tiles=npt),
        grid=(nt,),
        in_specs=[rows(D_MODEL), rows(LANES), rows_k(W),
                  full(wsg), full(wsu), full(wsd), full(ln2_g), full(ln2_b)],
        out_specs=[pl.BlockSpec((1, ROW_TILE, D_MODEL), yp_map),
                   pl.BlockSpec((ROW_TILE, D_MODEL), lambda i: (jnp.maximum(i - npt, 0), 0))],
        out_shape=[jax.ShapeDtypeStruct((B, seq, D_MODEL), F32),
                   jax.ShapeDtypeStruct((R - npt * ROW_TILE, D_MODEL), F32)],
        compiler_params=_cparams("arbitrary"),
        name="moe_combine",
    )(x1, gate, yk, wsg, wsu, wsd, ln2_g, ln2_b)


def _routing_tables(idx8, rank8, counts, n_slots_blocks):
    R = idx8.shape[0]
    padded = (counts + MOE_BLOCK - 1) // MOE_BLOCK * MOE_BLOCK
    pends = jnp.cumsum(padded)
    pstarts = pends - padded
    experts = jnp.arange(N_EXPERTS, dtype=I32)
    start8 = jnp.sum(jnp.where(idx8[:, :, None] == experts[None, None, :], pstarts[None, None, :], 0), axis=-1)
    dest = start8 + rank8
    blk_start = jnp.arange(n_slots_blocks, dtype=I32) * MOE_BLOCK
    block_expert = jnp.minimum(jnp.sum((blk_start[:, None] >= pends[None, :]).astype(I32), axis=1), N_EXPERTS - 1)
    n_used = (pends[-1] // MOE_BLOCK).astype(I32).reshape(1)
    hit = block_expert[:, None] == experts[None, :]
    left = jnp.sum(jnp.where(hit, (counts + pstarts)[None, :], 0), axis=1) - blk_start
    n_valid = jnp.clip(left, 0, MOE_BLOCK).astype(I32)
    return dest.astype(I32), block_expert.astype(I32), n_used, n_valid


def kernel(x_prompt, x_sample, cache_mla_ckv, cache_mla_krope, state_hgrn, page_table, meta_tokens, hg_lb_logits,
           w_in, q_norm, kv_norm, w_uq, w_uk, w_uv, hg_norm, w_br_mla, w_br_hg, w_out, ln1_g, ln1_b, router_w,
           router_bias, w_exp_gate, w_exp_up, w_exp_down, w_sh_gate, w_sh_up, w_sh_down, ln2_g, ln2_b):
    assert w_in.shape[0] == DEPTH
    B, seq, _ = x_prompt.shape
    DB, steps, _ = x_sample.shape
    n_meta = meta_tokens.shape[0]
    n_pages, page = page_table.shape[1], cache_mla_ckv.shape[2]
    past = n_pages * page
    T = n_meta + seq
    tp = -(-T // ROW_TILE) * ROW_TILE
    pad_front = tp - T
    tpb = tp // ROW_TILE
    n_s = DB * steps
    assert n_s % ROW_TILE == 0 and DB % HG_SAMPLE_BATCH == 0
    geo = dict(B=B, tp=tp, tpb=tpb, pad_front=pad_front, npt=B * tpb)
    Rp = B * tp
    R = Rp + n_s

    assert pad_front + n_meta == ROW_TILE and seq % ROW_TILE == 0
    x_head = jnp.concatenate([jnp.zeros((pad_front, D_MODEL), F32), meta_tokens.astype(F32)], axis=0)
    x_smp = x_sample.reshape(n_s, D_MODEL)

    pos_p = jnp.maximum(jnp.arange(tp) - pad_front, 0)
    pos = jnp.concatenate([jnp.tile(pos_p, B), jnp.tile(past + jnp.arange(steps), DB)]).astype(F32)
    half = MLA_ROPE // 2
    inv = ROPE_THETA ** (-jnp.arange(half, dtype=F32) / half)
    ang = pos[:, None] * inv[None, :]
    cos8 = jnp.tile(jnp.cos(ang), (1, LANES // half))
    sin8 = jnp.tile(jnp.sin(ang), (1, LANES // half))

    l = 0
    win = w_in[l]
    kr_end = MLA_Q_RANK + MLA_KV_RANK + MLA_ROPE
    w_in_p = jnp.concatenate([win[:, :kr_end], jnp.zeros((D_MODEL, LANES - MLA_ROPE), F32), win[:, kr_end:]],
                             axis=1).astype(BF16)
    assert w_in_p.shape[1] == D_IN_PACKED
    wq3 = w_uq[l].reshape(MLA_Q_RANK, MLA_HEADS, MLA_NOPE + MLA_ROPE)
    wqr = (jnp.concatenate([wq3[:, :, MLA_NOPE:MLA_NOPE + half].reshape(MLA_Q_RANK, MLA_HEADS * half),
                            wq3[:, :, MLA_NOPE + half:].reshape(MLA_Q_RANK, MLA_HEADS * half)], axis=1)
           * Q_SCALE).astype(BF16)
    src = jnp.arange(2 * LANES)
    hh, ii = (src % LANES) // half, src % half
    dst = hh * LANES + ii + jnp.where(src >= LANES, half, 0)
    perm = (dst[:, None] == jnp.arange(MLA_HEADS * LANES)[None, :]).astype(BF16)
    lb = jnp.cumsum(jax.nn.softmax(hg_lb_logits.astype(F32), axis=0), axis=0)[l].reshape(1, -1)

    wqlat, wo = _weight_prep(w_uq[l], w_uk[l], w_uv[l], w_br_mla[l])

    (qcat, kcat, ckv, kr, hq, hk, hv, lf, sg, sm, sh) = _in_projection(
        x_prompt, x_head, x_smp, w_in_p, q_norm[l].reshape(1, -1), kv_norm[l].reshape(1, -1), wqlat, wqr, perm,
        cos8, sin8, lb, geo)

    o_lat_p = _prompt_attention(qcat, kcat, geo)
    q_s = qcat[:, Rp:].reshape(MLA_HEADS, DB, steps, QCAT)
    knew_s = kcat[Rp:].reshape(DB, steps, 2 * LANES)
    cache_krt = jnp.swapaxes(cache_mla_krope[l], 1, 2)
    o_lat_s = _sample_attention(page_table, q_s, knew_s, cache_mla_ckv[l], cache_krt)
    o_lat_s = o_lat_s.reshape(n_s, MLA_HEADS * LANES)
    o_hg_p, s_prompt = _hgrn_prompt(hq, hk, hv, lf, geo)
    o_hg_s, s_sample = _hgrn_sample(hq, hk, hv, lf, state_hgrn[l], Rp, steps)

    rw_f = jnp.concatenate([router_w[l], jnp.zeros((D_MODEL, LANES - N_EXPERTS), F32)], axis=1)
    rw_hi = rw_f.astype(BF16)
    rw_p = jnp.stack([rw_hi, (rw_f - rw_hi.astype(F32)).astype(BF16)])
    rb_p = jnp.concatenate([router_bias[l].astype(F32), jnp.zeros((LANES - N_EXPERTS,), F32)]).reshape(1, -1)
    x1, x1p, idx, gate, rank, count = _layer_tail(
        x_prompt, x_head, x_smp, o_lat_p, o_lat_s, o_hg_p, o_hg_s, sg, sm, sh, wo, w_br_hg[l].astype(BF16),
        w_out[l].astype(BF16), hg_norm[l].reshape(1, -1), ln1_g[l].reshape(1, -1), ln1_b[l].reshape(1, -1),
        rw_p, rb_p, geo)

    nblk = R * TOP_K // MOE_BLOCK + N_EXPERTS
    dest, block_expert, n_used, n_valid = _routing_tables(
        idx[:, :TOP_K], rank[:, :TOP_K], count[0, :N_EXPERTS], nblk)
    xs = _sc_row_scatter(x1p, dest, nblk * MOE_BLOCK)
    ys = _moe_experts(block_expert, n_used, n_valid, xs, w_exp_gate[l], w_exp_up[l], w_exp_down[l])
    yp, y_smp = _moe_combine(dest, x1, gate, ys, w_sh_gate[l].astype(BF16), w_sh_up[l].astype(BF16),
                             w_sh_down[l].astype(BF16), ln2_g[l].reshape(1, -1), ln2_b[l].reshape(1, -1), geo, seq)

    ys_out = y_smp.reshape(DB, steps, D_MODEL)
    ckv_p = ckv[:Rp].reshape(B, tp, MLA_KV_RANK)[:, pad_front:][None]
    kr_p = kr[:Rp].reshape(B, tp, LANES)[:, pad_front:, :MLA_ROPE][None]
    ckv_s = ckv[Rp:].reshape(DB, steps, MLA_KV_RANK)[None]
    kr_s = kr[Rp:, :MLA_ROPE].reshape(DB, steps, MLA_ROPE)[None]
    return (yp, ys_out, ckv_p, kr_p, s_prompt[None], ckv_s, kr_s, s_sample[None])
```

```python
import functools

import jax
import jax.numpy as jnp
from jax import lax
from jax.experimental import pallas as pl
from jax.experimental.pallas import tpu as pltpu
from jax.experimental.pallas import tpu_sc as plsc

F32 = jnp.float32
BF16 = jnp.bfloat16
U32 = jnp.uint32
I32 = jnp.int32

D_MODEL = 1024
MLA_HEADS = 8
MLA_Q_RANK = 256
MLA_KV_RANK = 128
MLA_NOPE = 64
MLA_ROPE = 32
MLA_V = 64
MLA_SCALE = (MLA_NOPE + MLA_ROPE) ** -0.5
LOG2E = 1.4426950408889634
Q_SCALE = MLA_SCALE * LOG2E
ROPE_THETA = 10000.0
HG_HEADS = 4
HG_DK = 128
HG_DV = 128
HG_CHUNK = 64
N_EXPERTS = 64
TOP_K = 8
D_EXPERT = 256
ROUTED_SCALE = 2.5
NORM_EPS = 1e-6
DEPTH = 1
ALPHA = (2.0 * DEPTH) ** 0.25

LANES = 128
ROW_TILE = 256
QCAT = 256
ATT_TK = 256
MOE_BLOCK = 512
PAGES_PER_CHUNK = 64
ONES_COL = LANES + MLA_ROPE
HG_SAMPLE_BATCH = 8
SATT_BATCH = 1
SC_CORES = 2
SC_SUBCORES = 16
SC_GATHER_ROWS = 64
SC_SCATTER_TOKENS = 8
NEG_BIG = -1e30
VMEM_LIMIT = 52 * 1024 * 1024

C_CQ = (0, 256)
C_CKV = (256, 384)
C_KR = (384, 512)
C_HQ = (512, 1024)
C_HF = (1024, 1536)
C_HI = (1536, 2048)
C_HG = (2048, 2560)
C_GM = (2560, 3584)
C_GH = (3584, 4608)
D_IN_PACKED = 4608


def _cparams(*sem):
    return pltpu.CompilerParams(dimension_semantics=sem, vmem_limit_bytes=VMEM_LIMIT)


def _dot(a, b):
    return jnp.dot(a, b, preferred_element_type=F32)


def _dot_nt(a, b):
    return lax.dot_general(a, b, (((1,), (1,)), ((), ())), preferred_element_type=F32)


def _dot_tn(a, b):
    return lax.dot_general(a, b, (((0,), (0,)), ((), ())), preferred_element_type=F32)


def _pack_bf16_pairs(x):
    w = x.shape[1] // 2
    bits = pltpu.bitcast(x.astype(BF16).astype(F32), U32)
    return bits[:, w:] | (bits[:, :w] >> 16)


def _unpack_bf16_pairs(p):
    lo = pltpu.bitcast(p << 16, F32)
    hi = pltpu.bitcast(p & jnp.uint32(0xFFFF0000), F32)
    return jnp.concatenate([lo, hi], axis=1)


def _x_specs(geo, seq_tiles):
    B, tpb, npt = geo["B"], geo["tpb"], geo["npt"]

    def xp_map(i):
        return (jnp.minimum(i // tpb, B - 1), jnp.where(i < npt, jnp.maximum(i % tpb - 1, 0), seq_tiles - 1), 0)

    return [pl.BlockSpec((1, ROW_TILE, D_MODEL), xp_map),
            pl.BlockSpec((ROW_TILE, D_MODEL), lambda i: (0, 0)),
            pl.BlockSpec((ROW_TILE, D_MODEL), lambda i: (jnp.maximum(i - npt, 0), 0))]


def _select_x(i, xp_ref, xh_ref, xs_ref, tiles_per_batch, n_prompt_tiles):
    is_head = (i < n_prompt_tiles) & (i % tiles_per_batch == 0)
    return jnp.where(i >= n_prompt_tiles, xs_ref[...], jnp.where(is_head, xh_ref[...], xp_ref[0]))


def _split_specs(width, npt):
    return [pl.BlockSpec((ROW_TILE, width), lambda i: (jnp.minimum(i, npt - 1), 0)),
            pl.BlockSpec((ROW_TILE, width), lambda i: (jnp.maximum(i - npt, 0), 0))]


def _wprep_kernel(wq_nope_ref, wuk_ref, wuv_ref, wbr_ref, wqlat_ref, wo_ref):
    hp = lax.Precision.HIGHEST
    for h in range(MLA_HEADS):
        a = wq_nope_ref[h]
        b = wuk_ref[h]
        ql = lax.dot_general(a, b, (((1,), (1,)), ((), ())), precision=hp, preferred_element_type=F32)
        wqlat_ref[:, h * MLA_KV_RANK:(h + 1) * MLA_KV_RANK] = (ql * Q_SCALE).astype(BF16)
        c = wuv_ref[h]
        d = wbr_ref[h]
        wo_ref[h * MLA_KV_RANK:(h + 1) * MLA_KV_RANK, :] = jnp.dot(
            c, d, precision=hp, preferred_element_type=F32).astype(BF16)


def _weight_prep(w_uq, w_uk, w_uv, w_br_mla):
    wq3 = w_uq.reshape(MLA_Q_RANK, MLA_HEADS, MLA_NOPE + MLA_ROPE)
    wq_nope = jnp.transpose(wq3[:, :, :MLA_NOPE], (1, 0, 2))
    wuk = jnp.transpose(w_uk, (1, 0, 2))
    wuv = jnp.transpose(w_uv, (1, 0, 2))
    wbr = w_br_mla.reshape(MLA_HEADS, MLA_V, D_MODEL)
    return pl.pallas_call(
        _wprep_kernel,
        out_shape=(jax.ShapeDtypeStruct((MLA_Q_RANK, MLA_HEADS * MLA_KV_RANK), BF16),
                   jax.ShapeDtypeStruct((MLA_HEADS * MLA_KV_RANK, D_MODEL), BF16)),
        compiler_params=pltpu.CompilerParams(vmem_limit_bytes=VMEM_LIMIT),
        name="weight_prep",
    )(wq_nope, wuk, wuv, wbr)


def _inproj_kernel(xp_ref, xh_ref, xs_ref, w_ref, qn_ref, kvn_ref, wqlat_ref, wqr_ref, perm_ref, cos_ref, sin_ref,
                   lb_ref, qcat_ref, kcat_ref, ckv_ref, kr_ref, hq_ref, hk_ref, hv_ref, lf_ref, sg_ref, sm_ref,
                   sh_ref, *, pad_front, tiles_per_batch, n_prompt_tiles):
    i = pl.program_id(0)
    tm = xh_ref.shape[0]
    xb = _select_x(i, xp_ref, xh_ref, xs_ref, tiles_per_batch, n_prompt_tiles).astype(BF16)

    def proj(c):
        return _dot(xb, w_ref[:, c[0]:c[1]])

    cos8 = cos_ref[...]
    sin8 = sin_ref[...]

    cq = proj(C_CQ)
    cqn = cq * lax.rsqrt(jnp.mean(cq * cq, axis=-1, keepdims=True) + NORM_EPS) * qn_ref[...]
    cqb = cqn.astype(BF16)
    qlat = _dot(cqb, wqlat_ref[...])
    qr = _dot(cqb, wqr_ref[...])
    x1, x2 = qr[:, :LANES], qr[:, LANES:]
    qrot = jnp.concatenate([x1 * cos8 - x2 * sin8, x2 * cos8 + x1 * sin8], axis=1).astype(BF16)
    qrh = _dot(qrot, perm_ref[...])
    for h in range(MLA_HEADS):
        qcat_ref[h, :, :LANES] = qlat[:, h * LANES:(h + 1) * LANES].astype(BF16)
        qcat_ref[h, :, LANES:] = qrh[:, h * LANES:(h + 1) * LANES].astype(BF16)

    kv = proj(C_CKV)
    ckv = kv * lax.rsqrt(jnp.mean(kv * kv, axis=-1, keepdims=True) + NORM_EPS) * kvn_ref[...]
    ckv_ref[...] = ckv
    kcat_ref[:, :LANES] = ckv.astype(BF16)
    krr = proj(C_KR)
    lane = lax.broadcasted_iota(I32, (tm, LANES), 1)
    half = MLA_ROPE // 2
    rot = jnp.where(lane < half, -pltpu.roll(krr, LANES - half, 1), pltpu.roll(krr, half, 1))
    kr = jnp.where(lane < MLA_ROPE, krr * cos8 + rot * sin8, 0.0)
    kr_ref[...] = kr
    kcat_ref[:, LANES:] = jnp.where(lane == ONES_COL - LANES, 1.0, kr).astype(BF16)

    row = lax.broadcasted_iota(I32, (tm, 1), 0)
    is_pad = (i < n_prompt_tiles) & (i % tiles_per_batch == 0) & (row < pad_front)
    keep = jnp.where(is_pad, 0.0, 1.0)
    hq_ref[...] = proj(C_HQ).astype(BF16)
    lb = lb_ref[...]
    f = lb + (1.0 - lb) * jax.nn.sigmoid(proj(C_HF))
    lf_ref[...] = jnp.log(f) * keep
    hk_ref[...] = ((1.0 - f) * keep).astype(BF16)
    hv_ref[...] = proj(C_HI).astype(BF16)

    g = proj(C_HG)
    sg_ref[...] = (g * jax.nn.sigmoid(g)).astype(BF16)
    sm_ref[...] = jax.nn.sigmoid(proj(C_GM)).astype(BF16)
    sh_ref[...] = jax.nn.sigmoid(proj(C_GH)).astype(BF16)


def _in_projection(x_prompt, x_head, x_smp, w_in_p, q_norm, kv_norm, wqlat, wqr, perm, cos8, sin8, lb, geo):
    R = cos8.shape[0]
    nt = R // ROW_TILE
    hgw = HG_HEADS * HG_DK
    x_specs = _x_specs(geo, x_prompt.shape[1] // ROW_TILE)

    def rows(w):
        return pl.BlockSpec((ROW_TILE, w), lambda i: (i, 0))

    def full(a):
        return pl.BlockSpec(a.shape, lambda i: (0,) * a.ndim)

    out_widths = [(2 * LANES, BF16), (LANES, F32), (LANES, F32),
                  (hgw, BF16), (hgw, BF16), (hgw, BF16), (hgw, F32), (hgw, BF16),
                  (D_MODEL, BF16), (D_MODEL, BF16)]
    qcat_spec = pl.BlockSpec((MLA_HEADS, ROW_TILE, QCAT), lambda i: (0, i, 0))
    return pl.pallas_call(
        functools.partial(_inproj_kernel, pad_front=geo["pad_front"], tiles_per_batch=geo["tpb"],
                          n_prompt_tiles=geo["npt"]),
        grid=(nt,),
        in_specs=x_specs + [full(w_in_p), full(q_norm), full(kv_norm), full(wqlat), full(wqr), full(perm),
                            rows(LANES), rows(LANES), full(lb)],
        out_specs=[qcat_spec] + [rows(w) for w, _ in out_widths],
        out_shape=[jax.ShapeDtypeStruct((MLA_HEADS, R, QCAT), BF16)]
        + [jax.ShapeDtypeStruct((R, w), dt) for w, dt in out_widths],
        compiler_params=_cparams("arbitrary"),
        name="in_projection",
    )(x_prompt, x_head, x_smp, w_in_p, q_norm, kv_norm, wqlat, wqr, perm, cos8, sin8, lb)


def _softmax_step(s, v_b, m_ref, l_ref, acc_ref):
    n = s.shape[1] // LANES
    m_prev = m_ref[...]
    m_next = jnp.maximum(m_prev, jnp.max(s, axis=1, keepdims=True))
    p = jnp.concatenate([jnp.exp2(s[:, j * LANES:(j + 1) * LANES] - m_next) for j in range(n)], axis=1)
    alpha = jnp.exp2(m_prev - m_next)
    pv = _dot(p.astype(BF16), v_b)
    if v_b.shape[1] == LANES:
        l_ref[...] = alpha * l_ref[...] + jnp.sum(p, axis=1, keepdims=True)
    else:
        l_ref[...] = alpha * l_ref[...] + pv[:, LANES:]
    acc_ref[...] = alpha * acc_ref[...] + pv[:, :LANES]
    m_ref[...] = m_next


def _pattn_kernel(q_ref, k_ref, o_ref, m_ref, l_ref, acc_ref, *, pad_front):
    qi = pl.program_id(1)
    nh, tq, _ = q_ref.shape
    tk = ATT_TK
    rows = nh * tq

    q = q_ref[...].reshape(rows, QCAT)
    m_ref[...] = jnp.full(m_ref.shape, -jnp.inf, F32)
    l_ref[...] = jnp.zeros(l_ref.shape, F32)
    acc_ref[...] = jnp.zeros(acc_ref.shape, F32)

    def step(kb, masked):
        kblk = k_ref[pl.ds(pl.multiple_of(kb * tk, tk), tk), :]
        s = _dot_nt(q, kblk)
        if masked:
            qrow = qi * tq + lax.broadcasted_iota(I32, (rows, tk), 0) % tq
            krow = kb * tk + lax.broadcasted_iota(I32, (rows, tk), 1)
            s = jnp.where((krow <= qrow) & (krow >= pad_front), s, NEG_BIG)
        _softmax_step(s, kblk, m_ref, l_ref, acc_ref)

    step(0, True)

    def body(kb, c):
        step(kb, False)
        return c

    lax.fori_loop(1, qi, body, 0)

    @pl.when(qi > 0)
    def _():
        step(qi, True)

    row_sum = l_ref[:, ONES_COL - LANES:ONES_COL - LANES + 1]
    o = (acc_ref[...] / row_sum).astype(BF16)
    for h in range(nh):
        o_ref[:, h * LANES:(h + 1) * LANES] = o[h * tq:(h + 1) * tq]


def _prompt_attention(qcat, kcat, geo):
    B, tpb, tp = geo["B"], geo["tpb"], geo["tp"]
    rows = MLA_HEADS * ROW_TILE
    return pl.pallas_call(
        functools.partial(_pattn_kernel, pad_front=geo["pad_front"]),
        grid=(B, tpb),
        in_specs=[pl.BlockSpec((MLA_HEADS, ROW_TILE, QCAT), lambda b, i: (0, b * tpb + i, 0)),
                  pl.BlockSpec((tp, 2 * LANES), lambda b, i: (b, 0))],
        out_specs=pl.BlockSpec((ROW_TILE, MLA_HEADS * LANES), lambda b, i: (b * tpb + i, 0)),
        out_shape=jax.ShapeDtypeStruct((B * tp, MLA_HEADS * LANES), BF16),
        scratch_shapes=[pltpu.VMEM((rows, LANES), F32)] * 3,
        compiler_params=_cparams("arbitrary", "arbitrary"),
        name="prompt_attention",
    )(qcat, kcat)


def _sattn_kernel(pt_ref, q_ref, knew_ref, ckv_hbm, kr_hbm, o_ref,
                  ckv_buf, kr_buf, sem, m_ref, l_ref, acc_ref, *, n_chunks, page):
    g = pl.program_id(0)
    ng = pl.num_programs(0)
    ch = PAGES_PER_CHUNK
    nh, nbt, S = q_ref.shape[0], q_ref.shape[1], q_ref.shape[2]
    rows = nh * S

    def chunk_copies(gg, c, slot):
        cps = []
        for bb in range(nbt):
            for j in range(ch):
                pg = pt_ref[gg * nbt + bb, c * ch + j]
                cps.append(pltpu.make_async_copy(ckv_hbm.at[pg], ckv_buf.at[slot, bb, pl.ds(j * page, page), :],
                                                 sem.at[0, slot]))
                cps.append(pltpu.make_async_copy(kr_hbm.at[pg], kr_buf.at[slot, bb, :, pl.ds(j * page, page)],
                                                 sem.at[1, slot]))
        return cps

    @pl.when(g == 0)
    def _():
        for cp in chunk_copies(0, 0, 0):
            cp.start()

    qlat, qrope = [], []
    for bb in range(nbt):
        q = q_ref[:, bb].reshape(rows, QCAT)
        qlat.append(q[:, :LANES])
        qrope.append(q[:, LANES:LANES + MLA_ROPE])
        knew = knew_ref[bb]
        s_new = _dot_nt(q, knew)
        qtok = lax.broadcasted_iota(I32, (rows, S), 0) % S
        ktok = lax.broadcasted_iota(I32, (rows, S), 1)
        s_new = jnp.where(ktok <= qtok, s_new, NEG_BIG)
        m0 = jnp.max(s_new, axis=1, keepdims=True)
        p0 = jnp.exp2(s_new - m0)
        m_ref[bb] = jnp.broadcast_to(m0, (rows, LANES))
        l_ref[bb] = jnp.broadcast_to(jnp.sum(p0, axis=1, keepdims=True), (rows, LANES))
        acc_ref[bb] = _dot(p0.astype(BF16), knew[:, :LANES])

    for c in range(n_chunks):
        slot = c % 2 if n_chunks % 2 == 0 else (g * n_chunks + c) % 2
        if c + 1 < n_chunks:
            for cp in chunk_copies(g, c + 1, 1 - slot):
                cp.start()
        else:
            @pl.when(g + 1 < ng)
            def _():
                for cp in chunk_copies(g + 1, 0, 1 - slot):
                    cp.start()
        for cp in chunk_copies(g, c, slot):
            cp.wait()
        for bb in range(nbt):
            ckv_b = ckv_buf[slot, bb].astype(BF16)
            krt_b = kr_buf[slot, bb].astype(BF16)
            s = _dot_nt(qlat[bb], ckv_b) + _dot(qrope[bb], krt_b)
            _softmax_step(s, ckv_b, m_ref.at[bb], l_ref.at[bb], acc_ref.at[bb])

    for bb in range(nbt):
        o = (acc_ref[bb] / l_ref[bb]).astype(BF16)
        for h in range(nh):
            o_ref[bb, :, h * LANES:(h + 1) * LANES] = o[h * S:(h + 1) * S]


def _sample_attention(page_table, q_s, knew_s, cache_ckv, cache_krt):
    nh, DB, S, _ = q_s.shape
    rows = nh * S
    n_pages = page_table.shape[1]
    page = cache_ckv.shape[1]
    assert n_pages % PAGES_PER_CHUNK == 0
    n_chunks = n_pages // PAGES_PER_CHUNK
    ck = PAGES_PER_CHUNK * page
    nbt = SATT_BATCH
    assert DB % nbt == 0
    grid_spec = pltpu.PrefetchScalarGridSpec(
        num_scalar_prefetch=1,
        grid=(DB // nbt,),
        in_specs=[pl.BlockSpec((nh, nbt, S, QCAT), lambda g, pt: (0, g, 0, 0)),
                  pl.BlockSpec((nbt, S, 2 * LANES), lambda g, pt: (g, 0, 0)),
                  pl.BlockSpec(memory_space=pl.ANY),
                  pl.BlockSpec(memory_space=pl.ANY)],
        out_specs=pl.BlockSpec((nbt, S, nh * LANES), lambda g, pt: (g, 0, 0)),
        scratch_shapes=[pltpu.VMEM((2, nbt, ck, MLA_KV_RANK), F32),
                        pltpu.VMEM((2, nbt, MLA_ROPE, ck), F32),
                        pltpu.SemaphoreType.DMA((2, 2)),
                        pltpu.VMEM((nbt, rows, LANES), F32),
                        pltpu.VMEM((nbt, rows, LANES), F32),
                        pltpu.VMEM((nbt, rows, LANES), F32)])
    return pl.pallas_call(
        functools.partial(_sattn_kernel, n_chunks=n_chunks, page=page),
        grid_spec=grid_spec,
        out_shape=jax.ShapeDtypeStruct((DB, S, nh * LANES), BF16),
        compiler_params=_cparams("arbitrary"),
        name="sample_attention",
    )(page_table, q_s, knew_s, cache_ckv, cache_krt)


def _split3(x):
    hi = x.astype(BF16)
    r1 = x - hi.astype(F32)
    mid = r1.astype(BF16)
    lo = (r1 - mid.astype(F32)).astype(BF16)
    return hi, mid, lo


def _hgrn_chunk(q, k, v, lf, S0, tri, group):
    C = q.shape[0]
    hi, mid, lo = _split3(lf)
    cum = _dot(tri, hi) + _dot(tri, mid) + _dot(tri, lo)
    vb = v.astype(BF16)
    rowi = lax.broadcasted_iota(I32, (C, C), 0)
    coli = lax.broadcasted_iota(I32, (C, C), 1)
    attn = jnp.zeros((C, C), F32)

    bd = min(16, group)
    hs = group // 2
    rid = lax.broadcasted_iota(I32, (C, 1), 0)
    while hs >= bd:
        npair = C // (2 * hs)
        ref = jnp.concatenate(
            [jnp.broadcast_to(cum[(2 * j + 1) * hs - 1:(2 * j + 1) * hs, :], (2 * hs, cum.shape[1]))
             for j in range(npair)], axis=0)
        odd = ((rid // hs) % 2) == 1
        e = jnp.exp(jnp.where(odd, cum - ref, ref - cum))
        qs = jnp.where(odd, q * e, 0.0).astype(BF16)
        ks = jnp.where(odd, 0.0, k * e).astype(BF16)
        a = _dot_nt(qs, ks)
        attn = attn + jnp.where((rowi // (2 * hs)) == (coli // (2 * hs)), a, 0.0)
        hs //= 2

    nblk = C // bd
    k3 = k.reshape(nblk, bd, k.shape[1])
    c3 = cum.reshape(nblk, bd, cum.shape[1])
    tl = rid % bd
    blk_base = (rowi // bd) * bd
    for sl in range(bd):
        ks_b = jnp.broadcast_to(k3[:, sl:sl + 1, :], k3.shape).reshape(C, k.shape[1])
        cs_b = jnp.broadcast_to(c3[:, sl:sl + 1, :], c3.shape).reshape(C, k.shape[1])
        e = jnp.exp(jnp.where(tl >= sl, cum - cs_b, NEG_BIG))
        col = jnp.sum(q * ks_b * e, axis=1, keepdims=True)
        attn = jnp.where(coli == blk_base + sl, col, attn)
    o_intra = _dot(attn.astype(BF16), vb)
    return o_intra, cum


def _tri(C, group):
    r = lax.broadcasted_iota(I32, (C, C), 0)
    c = lax.broadcasted_iota(I32, (C, C), 1)
    return jnp.where((c <= r) & (r // group == c // group), 1.0, 0.0).astype(BF16)


def _state_update(q, k, v, cum, S, o_intra):
    C = q.shape[0]
    last = cum[C - 1:C, :]
    o = o_intra + _dot((q * jnp.exp(cum)).astype(BF16), S.astype(BF16))
    kst = (k * jnp.exp(last - cum)).astype(BF16)
    dfull = jnp.transpose(jnp.broadcast_to(jnp.exp(last), (S.shape[1], S.shape[0])))
    S_new = dfull * S + _dot_tn(kst, v.astype(BF16))
    return o, S_new


def _hgrn_prompt_kernel(q_ref, k_ref, v_ref, lf_ref, o_ref, s_out_ref, s_ref):
    t = pl.program_id(1)

    @pl.when(t == 0)
    def _():
        s_ref[...] = jnp.zeros(s_ref.shape, F32)

    C = HG_CHUNK
    tri = _tri(C, C)
    states = [s_ref[h] for h in range(HG_HEADS)]
    for c in range(q_ref.shape[0] // C):
        sl = slice(c * C, (c + 1) * C)
        for h in range(HG_HEADS):
            hl = slice(h * HG_DK, (h + 1) * HG_DK)
            q = q_ref[sl, hl].astype(F32)
            k = k_ref[sl, hl].astype(F32)
            v = v_ref[sl, hl].astype(F32)
            lf = lf_ref[sl, hl]
            o_intra, cum = _hgrn_chunk(q, k, v, lf, None, tri, C)
            o, states[h] = _state_update(q, k, v, cum, states[h], o_intra)
            o_ref[sl, hl] = o.astype(BF16)
    for h in range(HG_HEADS):
        s_ref[h] = states[h]

    @pl.when(t == pl.num_programs(1) - 1)
    def _():
        s_out_ref[0] = s_ref[...]


def _hgrn_prompt(hq, hk, hv, lf, geo):
    B, tpb, tp = geo["B"], geo["tpb"], geo["tp"]
    hgw = HG_HEADS * HG_DK

    def blk():
        return pl.BlockSpec((ROW_TILE, hgw), lambda b, t: (b * tpb + t, 0))

    return pl.pallas_call(
        _hgrn_prompt_kernel,
        grid=(B, tpb),
        in_specs=[blk(), blk(), blk(), blk()],
        out_specs=[blk(), pl.BlockSpec((1, HG_HEADS, HG_DK, HG_DV), lambda b, t: (b, 0, 0, 0))],
        out_shape=[jax.ShapeDtypeStruct((B * tp, hgw), BF16),
                   jax.ShapeDtypeStruct((B, HG_HEADS, HG_DK, HG_DV), F32)],
        scratch_shapes=[pltpu.VMEM((HG_HEADS, HG_DK, HG_DV), F32)],
        compiler_params=_cparams("arbitrary", "arbitrary"),
        name="hgrn_prompt",
    )(hq, hk, hv, lf)


def _hgrn_sample_kernel(q_ref, k_ref, v_ref, lf_ref, s_in_ref, o_ref, s_out_ref, *, steps):
    nb = s_in_ref.shape[0]
    C = nb * steps
    tri = _tri(C, steps)
    for h in range(HG_HEADS):
        sl = slice(h * HG_DK, (h + 1) * HG_DK)
        q = q_ref[:, sl].astype(F32)
        k = k_ref[:, sl].astype(F32)
        v = v_ref[:, sl].astype(F32)
        lf = lf_ref[:, sl]
        o_intra, cum = _hgrn_chunk(q, k, v, lf, None, tri, steps)
        for b in range(nb):
            r = slice(b * steps, (b + 1) * steps)
            o, S_new = _state_update(q[r], k[r], v[r], cum[r], s_in_ref[b, h], o_intra[r])
            o_ref[r, sl] = o.astype(BF16)
            s_out_ref[b, h] = S_new


def _hgrn_sample(hq, hk, hv, lf, state, row0, steps):
    DB = state.shape[0]
    nb = HG_SAMPLE_BATCH
    rows = nb * steps
    hgw = HG_HEADS * HG_DK
    blk0 = row0 // rows

    def tok():
        return pl.BlockSpec((rows, hgw), lambda i: (blk0 + i, 0))

    st = pl.BlockSpec((nb, HG_HEADS, HG_DK, HG_DV), lambda i: (i, 0, 0, 0))
    return pl.pallas_call(
        functools.partial(_hgrn_sample_kernel, steps=steps),
        grid=(DB // nb,),
        in_specs=[tok(), tok(), tok(), tok(), st],
        out_specs=[pl.BlockSpec((rows, hgw), lambda i: (i, 0)), st],
        out_shape=[jax.ShapeDtypeStruct((DB * steps, hgw), BF16),
                   jax.ShapeDtypeStruct(state.shape, F32)],
        compiler_params=_cparams("arbitrary"),
        name="hgrn_sample",
    )(hq, hk, hv, lf, state)


def _layer_norm(x, g, b):
    mu = jnp.mean(x, axis=-1, keepdims=True)
    xc = x - mu
    var = jnp.mean(xc * xc, axis=-1, keepdims=True)
    return xc * lax.rsqrt(var + NORM_EPS) * g + b


def _tail_kernel(xp_ref, xh_ref, xs_ref, olp_ref, ols_ref, ohp_ref, ohs_ref, sg_ref, sm_ref, sh_ref, wo_ref, wbh_ref,
                 wout_ref, hgn_ref, g1_ref, b1_ref, rw_ref, rb_ref, x1_ref, x1p_ref, idx_ref, gate_ref, rank_ref,
                 count_ref, cnt_ref, *, tiles_per_batch, n_prompt_tiles):
    i = pl.program_id(0)
    tm = xh_ref.shape[0]
    x_in = _select_x(i, xp_ref, xh_ref, xs_ref, tiles_per_batch, n_prompt_tiles)
    is_sample = i >= n_prompt_tiles
    mla = _dot(jnp.where(is_sample, ols_ref[...], olp_ref[...]), wo_ref[...])
    oh = jnp.where(is_sample, ohs_ref[...], ohp_ref[...]).astype(F32)
    parts = []
    for h in range(HG_HEADS):
        y = oh[:, h * HG_DV:(h + 1) * HG_DV]
        parts.append(y * lax.rsqrt(jnp.mean(y * y, axis=-1, keepdims=True) + NORM_EPS) * hgn_ref[...])
    hg = (jnp.concatenate(parts, axis=1) * sg_ref[...].astype(F32)).astype(BF16)
    merged = sm_ref[...].astype(F32) * mla + sh_ref[...].astype(F32) * _dot(hg, wbh_ref[...])
    x1 = _layer_norm(ALPHA * x_in + _dot(merged.astype(BF16), wout_ref[...]), g1_ref[...], b1_ref[...])
    x1_ref[...] = x1
    x1p_ref[...] = pltpu.bitcast(_pack_bf16_pairs(x1), I32)

    x1_hi = x1.astype(BF16)
    x1_lo = (x1 - x1_hi.astype(F32)).astype(BF16)
    logits = _dot(x1_hi, rw_ref[0]) + _dot(x1_lo, rw_ref[0]) + _dot(x1_hi, rw_ref[1])
    scores = jax.nn.sigmoid(logits)
    lane = lax.broadcasted_iota(I32, (tm, LANES), 1).astype(F32)
    remaining = jnp.where(lane < N_EXPERTS, scores + rb_ref[...], -jnp.inf)
    idx_out = jnp.zeros((tm, LANES), F32)
    gate_out = jnp.zeros((tm, LANES), F32)
    hits = []
    for kk in range(TOP_K):
        mx = jnp.max(remaining, axis=1, keepdims=True)
        pick = jnp.min(jnp.where(remaining == mx, lane, float(LANES)), axis=1, keepdims=True)
        hit = lane == pick
        hits.append(hit)
        gval = jnp.sum(jnp.where(hit, scores, 0.0), axis=1, keepdims=True)
        idx_out = jnp.where(lane == kk, pick, idx_out)
        gate_out = jnp.where(lane == kk, gval, gate_out)
        remaining = jnp.where(hit, -jnp.inf, remaining)
    gate_out = gate_out / jnp.sum(gate_out, axis=1, keepdims=True) * ROUTED_SCALE
    idx_ref[...] = idx_out.astype(I32)
    gate_ref[...] = gate_out

    @pl.when(pl.program_id(0) == 0)
    def _():
        cnt_ref[...] = jnp.zeros(cnt_ref.shape, F32)

    sel = jnp.where(remaining == -jnp.inf, 1.0, 0.0) * jnp.where(lane < N_EXPERTS, 1.0, 0.0)
    r_i = lax.broadcasted_iota(I32, (tm, tm), 0)
    c_i = lax.broadcasted_iota(I32, (tm, tm), 1)
    before = _dot(jnp.where(c_i < r_i, 1.0, 0.0).astype(BF16), sel.astype(BF16)) + cnt_ref[0:1, :]
    rank_out = jnp.zeros((tm, LANES), F32)
    for kk in range(TOP_K):
        rk = jnp.sum(jnp.where(hits[kk], before, 0.0), axis=1, keepdims=True)
        rank_out = jnp.where(lane == kk, rk, rank_out)
    rank_ref[...] = rank_out.astype(I32)
    total = cnt_ref[0:1, :] + jnp.sum(sel, axis=0, keepdims=True)
    cnt_ref[...] = jnp.broadcast_to(total, cnt_ref.shape)
    count_ref[...] = jnp.broadcast_to(total, count_ref.shape).astype(I32)


def _layer_tail(x_prompt, x_head, x_smp, o_lat_p, o_lat_s, o_hg_p, o_hg_s, sg, sm, sh, wo, w_br_hg, w_out, hg_norm,
                ln1_g, ln1_b, rw_p, rb_p, geo):
    R = sg.shape[0]
    npt = geo["npt"]

    def rows(w):
        return pl.BlockSpec((ROW_TILE, w), lambda i: (i, 0))

    def full(a):
        return pl.BlockSpec(a.shape, lambda i: (0,) * a.ndim)

    hgw = HG_HEADS * HG_DV
    return pl.pallas_call(
        functools.partial(_tail_kernel, tiles_per_batch=geo["tpb"], n_prompt_tiles=npt),
        grid=(R // ROW_TILE,),
        in_specs=_x_specs(geo, x_prompt.shape[1] // ROW_TILE)
        + _split_specs(MLA_HEADS * LANES, npt) + _split_specs(hgw, npt)
        + [rows(hgw), rows(D_MODEL), rows(D_MODEL),
           full(wo), full(w_br_hg), full(w_out), full(hg_norm), full(ln1_g), full(ln1_b),
           full(rw_p), full(rb_p)],
        out_specs=[rows(D_MODEL), rows(D_MODEL // 2), rows(LANES), rows(LANES), rows(LANES),
                   pl.BlockSpec((8, LANES), lambda i: (0, 0))],
        out_shape=[jax.ShapeDtypeStruct((R, D_MODEL), F32), jax.ShapeDtypeStruct((R, D_MODEL // 2), I32),
                   jax.ShapeDtypeStruct((R, LANES), I32), jax.ShapeDtypeStruct((R, LANES), F32),
                   jax.ShapeDtypeStruct((R, LANES), I32), jax.ShapeDtypeStruct((8, LANES), I32)],
        scratch_shapes=[pltpu.VMEM((8, LANES), F32)],
        compiler_params=_cparams("arbitrary"),
        name="layer_tail",
    )(x_prompt, x_head, x_smp, o_lat_p, o_lat_s, o_hg_p, o_hg_s, sg, sm, sh, wo, w_br_hg, w_out, hg_norm,
      ln1_g, ln1_b, rw_p, rb_p)


def _sc_row_scatter(x, dest, n_slots):
    R, d = x.shape
    nw = SC_CORES * SC_SUBCORES
    T = SC_SCATTER_TOKENS
    rows = T * TOP_K
    per_w = R // nw
    assert R % (nw * 2 * T) == 0
    n_pairs = per_w // (2 * T)
    order = jnp.transpose(dest.reshape(R // T, T, TOP_K), (0, 2, 1)).reshape(R * TOP_K)
    mesh = plsc.VectorSubcoreMesh(core_axis_name="c", subcore_axis_name="s",
                                  num_cores=SC_CORES, num_subcores=SC_SUBCORES)

    @functools.partial(
        pl.kernel, mesh=mesh, out_type=jax.ShapeDtypeStruct((n_slots, d), x.dtype),
        scratch_types=[pltpu.VMEM((rows,), I32), pltpu.VMEM((rows,), I32),
                       pltpu.VMEM((rows, d), x.dtype), pltpu.VMEM((rows, d), x.dtype),
                       pltpu.SemaphoreType.DMA, pltpu.SemaphoreType.DMA, pltpu.SemaphoreType.DMA],
        name="moe_dispatch_sc")
    def scatter(x_hbm, idx_hbm, out_hbm, idx_a, idx_b, rows_a, rows_b, sem_a, sem_b, sem_out):
        wid = lax.axis_index("s") * SC_CORES + lax.axis_index("c")
        tbase = wid * per_w
        bufs = ((idx_a, rows_a, sem_a), (idx_b, rows_b, sem_b))

        def loads(c, buf):
            idx_v, rows_v, sem = buf
            t0 = pl.multiple_of(tbase + c * T, T)
            off = pl.multiple_of(t0 * TOP_K, rows)
            cps = [pltpu.make_async_copy(idx_hbm.at[pl.ds(off, rows)], idx_v, sem)]
            for k in range(TOP_K):
                cps.append(pltpu.make_async_copy(x_hbm.at[pl.ds(t0, T)], rows_v.at[pl.ds(k * T, T)], sem))
            return cps

        def load(c, buf):
            for cp in loads(c, buf):
                cp.start()

        def send(c, buf):
            idx_v, rows_v, _ = buf
            for cp in loads(c, buf):
                cp.wait()
            pltpu.async_copy(rows_v, out_hbm.at[idx_v], sem_out).wait()

        load(0, bufs[0])

        def body(p, carry):
            c = 2 * p
            load(c + 1, bufs[1])
            send(c, bufs[0])

            @pl.when(p + 1 < n_pairs)
            def _():
                load(c + 2, bufs[0])

            send(c + 1, bufs[1])
            return carry

        lax.fori_loop(0, n_pairs, body, 0)

    return scatter(x, order)


def _expert_kernel(be_ref, nused_ref, nvalid_ref, xs_ref, wg_ref, wu_ref, wd_ref, ys_ref, wgb_ref, wub_ref, wdb_ref):
    i = pl.program_id(0)

    @pl.when((i == 0) | (be_ref[i] != be_ref[jnp.maximum(i - 1, 0)]))
    def _():
        wgb_ref[...] = wg_ref[0].astype(BF16)
        wub_ref[...] = wu_ref[0].astype(BF16)
        wdb_ref[...] = wd_ref[0].astype(BF16)

    @pl.when(i < nused_ref[0])
    def _():
        row = lax.broadcasted_iota(I32, xs_ref.shape, 0)
        xw = jnp.where(row < nvalid_ref[i], xs_ref[...], 0)
        xb = _unpack_bf16_pairs(pltpu.bitcast(xw, U32)).astype(BF16)
        g = _dot(xb, wgb_ref[...])
        u = _dot(xb, wub_ref[...])
        hmid = (g * jax.nn.sigmoid(g) * u).astype(BF16)
        ys_ref[...] = pltpu.bitcast(_pack_bf16_pairs(_dot(hmid, wdb_ref[...])), I32)

    @pl.when(i >= nused_ref[0])
    def _():
        ys_ref[...] = jnp.zeros(ys_ref.shape, I32)


def _moe_experts(block_expert, n_used, n_valid, xs, wg, wu, wd):
    n_slots, W = xs.shape
    nblk = n_slots // MOE_BLOCK

    def xmap(i, be, nu, nv):
        return (jnp.minimum(i, nu[0] - 1), 0)

    def wmap(i, be, nu, nv):
        return (be[i], 0, 0)

    grid_spec = pltpu.PrefetchScalarGridSpec(
        num_scalar_prefetch=3,
        grid=(nblk,),
        in_specs=[pl.BlockSpec((MOE_BLOCK, W), xmap),
                  pl.BlockSpec((1, D_MODEL, D_EXPERT), wmap),
                  pl.BlockSpec((1, D_MODEL, D_EXPERT), wmap),
                  pl.BlockSpec((1, D_EXPERT, D_MODEL), wmap)],
        out_specs=pl.BlockSpec((MOE_BLOCK, W), lambda i, be, nu, nv: (i, 0)),
        scratch_shapes=[pltpu.VMEM((D_MODEL, D_EXPERT), BF16), pltpu.VMEM((D_MODEL, D_EXPERT), BF16),
                        pltpu.VMEM((D_EXPERT, D_MODEL), BF16)])
    return pl.pallas_call(
        _expert_kernel,
        grid_spec=grid_spec,
        out_shape=jax.ShapeDtypeStruct((n_slots, W), I32),
        compiler_params=_cparams("arbitrary"),
        name="moe_experts",
    )(block_expert, n_used, n_valid, xs, wg, wu, wd)


def _sc_row_gather(table, idx):
    n, d = idx.shape[0], table.shape[1]
    nw = SC_CORES * SC_SUBCORES
    per_w = n // nw
    rows = SC_GATHER_ROWS
    assert n % (nw * 2 * rows) == 0
    n_pairs = per_w // (2 * rows)
    mesh = plsc.VectorSubcoreMesh(core_axis_name="c", subcore_axis_name="s",
                                  num_cores=SC_CORES, num_subcores=SC_SUBCORES)

    @functools.partial(
        pl.kernel, mesh=mesh, out_type=jax.ShapeDtypeStruct((n, d), table.dtype),
        scratch_types=[pltpu.VMEM((per_w,), I32),
                       pltpu.VMEM((rows, d), table.dtype), pltpu.VMEM((rows, d), table.dtype),
                       pltpu.SemaphoreType.DMA, pltpu.SemaphoreType.DMA],
        name="moe_gather_sc")
    def gather(table_hbm, idx_hbm, out_hbm, idx_all, rows_a, rows_b, sem_a, sem_b):
        wid = lax.axis_index("s") * SC_CORES + lax.axis_index("c")
        base = wid * per_w
        bufs = ((rows_a, sem_a), (rows_b, sem_b))
        pltpu.sync_copy(idx_hbm.at[pl.ds(pl.multiple_of(base, rows), per_w)], idx_all)

        def stream(c, buf):
            rows_v, sem = buf
            idx_v = idx_all.at[pl.ds(pl.multiple_of(c * rows, rows), rows)]
            return pltpu.make_async_copy(table_hbm.at[idx_v], rows_v, sem)

        def start(c, buf):
            stream(c, buf).start()

        def finish(c, buf):
            off = pl.multiple_of(base + c * rows, rows)
            stream(c, buf).wait()
            pltpu.sync_copy(buf[0], out_hbm.at[pl.ds(off, rows)])

        start(0, bufs[0])

        def body(p, carry):
            c = 2 * p
            start(c + 1, bufs[1])
            finish(c, bufs[0])

            @pl.when(p + 1 < n_pairs)
            def _():
                start(c + 2, bufs[0])

            finish(c + 1, bufs[1])
            return carry

        lax.fori_loop(0, n_pairs, body, 0)

    return gather(table, idx)


def _combine_kernel(x1_ref, gate_ref, yk_ref, wsg_ref, wsu_ref, wsd_ref, g2_ref, b2_ref, yp_ref, ysmp_ref,
                    *, n_prompt_tiles):
    i = pl.program_id(0)
    tm = x1_ref.shape[0]
    x1 = x1_ref[...]
    xb = x1.astype(BF16)
    g = _dot(xb, wsg_ref[...])
    u = _dot(xb, wsu_ref[...])
    ff = _dot((g * jax.nn.sigmoid(g) * u).astype(BF16), wsd_ref[...])
    gate = gate_ref[...]
    for kk in range(TOP_K):
        ff = ff + gate[:, kk:kk + 1] * _unpack_bf16_pairs(pltpu.bitcast(yk_ref[kk * tm:(kk + 1) * tm, :], U32))
    y = _layer_norm(ALPHA * x1 + ff, g2_ref[...], b2_ref[...])

    @pl.when(i < n_prompt_tiles)
    def _():
        yp_ref[0] = y

    @pl.when(i >= n_prompt_tiles)
    def _():
        ysmp_ref[...] = y


def _moe_combine(dest, x1, gate, ys, wsg, wsu, wsd, ln2_g, ln2_b, geo, seq):
    R = x1.shape[0]
    nt = R // ROW_TILE
    W = ys.shape[1]
    B, tpb, npt = geo["B"], geo["tpb"], geo["npt"]
    assert seq == (tpb - 1) * ROW_TILE
    order = jnp.transpose(dest.reshape(nt, ROW_TILE, TOP_K), (0, 2, 1)).reshape(R * TOP_K)
    yk = _sc_row_gather(ys, order)

    def rows(w):
        return pl.BlockSpec((ROW_TILE, w), lambda i: (i, 0))

    def full(a):
        return pl.BlockSpec(a.shape, lambda i: (0,) * a.ndim)

    def rows_k(w):
        return pl.BlockSpec((TOP_K * ROW_TILE, w), lambda i: (i, 0))

    def yp_map(i):
        in_prompt = i < npt
        return (jnp.minimum(i // tpb, B - 1), jnp.where(in_prompt, jnp.maximum(i % tpb - 1, 0), tpb - 2), 0)

    return pl.pallas_call(
        functools.partial(_combine_kernel, n_prompt_tiles=npt),
        grid=(nt,),
        in_specs=[rows(D_MODEL), rows(LANES), rows_k(W),
                  full(wsg), full(wsu), full(wsd), full(ln2_g), full(ln2_b)],
        out_specs=[pl.BlockSpec((1, ROW_TILE, D_MODEL), yp_map),
                   pl.BlockSpec((ROW_TILE, D_MODEL), lambda i: (jnp.maximum(i - npt, 0), 0))],
        out_shape=[jax.ShapeDtypeStruct((B, seq, D_MODEL), F32),
                   jax.ShapeDtypeStruct((R - npt * ROW_TILE, D_MODEL), F32)],
        compiler_params=_cparams("arbitrary"),
        name="moe_combine",
    )(x1, gate, yk, wsg, wsu, wsd, ln2_g, ln2_b)


def _routing_tables(idx8, rank8, counts, n_slots_blocks):
    R = idx8.shape[0]
    padded = (counts + MOE_BLOCK - 1) // MOE_BLOCK * MOE_BLOCK
    pends = jnp.cumsum(padded)
    pstarts = pends - padded
    experts = jnp.arange(N_EXPERTS, dtype=I32)
    start8 = jnp.sum(jnp.where(idx8[:, :, None] == experts[None, None, :], pstarts[None, None, :], 0), axis=-1)
    dest = start8 + rank8
    blk_start = jnp.arange(n_slots_blocks, dtype=I32) * MOE_BLOCK
    block_expert = jnp.minimum(jnp.sum((blk_start[:, None] >= pends[None, :]).astype(I32), axis=1), N_EXPERTS - 1)
    n_used = (pends[-1] // MOE_BLOCK).astype(I32).reshape(1)
    hit = block_expert[:, None] == experts[None, :]
    left = jnp.sum(jnp.where(hit, (counts + pstarts)[None, :], 0), axis=1) - blk_start
    n_valid = jnp.clip(left, 0, MOE_BLOCK).astype(I32)
    return dest.astype(I32), block_expert.astype(I32), n_used, n_valid


def kernel(x_prompt, x_sample, cache_mla_ckv, cache_mla_krope, state_hgrn, page_table, meta_tokens, hg_lb_logits,
           w_in, q_norm, kv_norm, w_uq, w_uk, w_uv, hg_norm, w_br_mla, w_br_hg, w_out, ln1_g, ln1_b, router_w,
           router_bias, w_exp_gate, w_exp_up, w_exp_down, w_sh_gate, w_sh_up, w_sh_down, ln2_g, ln2_b):
    assert w_in.shape[0] == DEPTH
    B, seq, _ = x_prompt.shape
    DB, steps, _ = x_sample.shape
    n_meta = meta_tokens.shape[0]
    n_pages, page = page_table.shape[1], cache_mla_ckv.shape[2]
    past = n_pages * page
    T = n_meta + seq
    tp = -(-T // ROW_TILE) * ROW_TILE
    pad_front = tp - T
    tpb = tp // ROW_TILE
    n_s = DB * steps
    assert n_s % ROW_TILE == 0 and DB % HG_SAMPLE_BATCH == 0
    geo = dict(B=B, tp=tp, tpb=tpb, pad_front=pad_front, npt=B * tpb)
    Rp = B * tp
    R = Rp + n_s

    assert pad_front + n_meta == ROW_TILE and seq % ROW_TILE == 0
    x_head = jnp.concatenate([jnp.zeros((pad_front, D_MODEL), F32), meta_tokens.astype(F32)], axis=0)
    x_smp = x_sample.reshape(n_s, D_MODEL)

    pos_p = jnp.maximum(jnp.arange(tp) - pad_front, 0)
    pos = jnp.concatenate([jnp.tile(pos_p, B), jnp.tile(past + jnp.arange(steps), DB)]).astype(F32)
    half = MLA_ROPE // 2
    inv = ROPE_THETA ** (-jnp.arange(half, dtype=F32) / half)
    ang = pos[:, None] * inv[None, :]
    cos8 = jnp.tile(jnp.cos(ang), (1, LANES // half))
    sin8 = jnp.tile(jnp.sin(ang), (1, LANES // half))

    l = 0
    win = w_in[l]
    kr_end = MLA_Q_RANK + MLA_KV_RANK + MLA_ROPE
    w_in_p = jnp.concatenate([win[:, :kr_end], jnp.zeros((D_MODEL, LANES - MLA_ROPE), F32), win[:, kr_end:]],
                             axis=1).astype(BF16)
    assert w_in_p.shape[1] == D_IN_PACKED
    wq3 = w_uq[l].reshape(MLA_Q_RANK, MLA_HEADS, MLA_NOPE + MLA_ROPE)
    wqr = (jnp.concatenate([wq3[:, :, MLA_NOPE:MLA_NOPE + half].reshape(MLA_Q_RANK, MLA_HEADS * half),
                            wq3[:, :, MLA_NOPE + half:].reshape(MLA_Q_RANK, MLA_HEADS * half)], axis=1)
           * Q_SCALE).astype(BF16)
    src = jnp.arange(2 * LANES)
    hh, ii = (src % LANES) // half, src % half
    dst = hh * LANES + ii + jnp.where(src >= LANES, half, 0)
    perm = (dst[:, None] == jnp.arange(MLA_HEADS * LANES)[None, :]).astype(BF16)
    lb = jnp.cumsum(jax.nn.softmax(hg_lb_logits.astype(F32), axis=0), axis=0)[l].reshape(1, -1)

    wqlat, wo = _weight_prep(w_uq[l], w_uk[l], w_uv[l], w_br_mla[l])

    (qcat, kcat, ckv, kr, hq, hk, hv, lf, sg, sm, sh) = _in_projection(
        x_prompt, x_head, x_smp, w_in_p, q_norm[l].reshape(1, -1), kv_norm[l].reshape(1, -1), wqlat, wqr, perm,
        cos8, sin8, lb, geo)

    o_lat_p = _prompt_attention(qcat, kcat, geo)
    q_s = qcat[:, Rp:].reshape(MLA_HEADS, DB, steps, QCAT)
    knew_s = kcat[Rp:].reshape(DB, steps, 2 * LANES)
    cache_krt = jnp.swapaxes(cache_mla_krope[l], 1, 2)
    o_lat_s = _sample_attention(page_table, q_s, knew_s, cache_mla_ckv[l], cache_krt)
    o_lat_s = o_lat_s.reshape(n_s, MLA_HEADS * LANES)
    o_hg_p, s_prompt = _hgrn_prompt(hq, hk, hv, lf, geo)
    o_hg_s, s_sample = _hgrn_sample(hq, hk, hv, lf, state_hgrn[l], Rp, steps)

    rw_f = jnp.concatenate([router_w[l], jnp.zeros((D_MODEL, LANES - N_EXPERTS), F32)], axis=1)
    rw_hi = rw_f.astype(BF16)
    rw_p = jnp.stack([rw_hi, (rw_f - rw_hi.astype(F32)).astype(BF16)])
    rb_p = jnp.concatenate([router_bias[l].astype(F32), jnp.zeros((LANES - N_EXPERTS,), F32)]).reshape(1, -1)
    x1, x1p, idx, gate, rank, count = _layer_tail(
        x_prompt, x_head, x_smp, o_lat_p, o_lat_s, o_hg_p, o_hg_s, sg, sm, sh, wo, w_br_hg[l].astype(BF16),
        w_out[l].astype(BF16), hg_norm[l].reshape(1, -1), ln1_g[l].reshape(1, -1), ln1_b[l].reshape(1, -1),
        rw_p, rb_p, geo)

    nblk = R * TOP_K // MOE_BLOCK + N_EXPERTS
    dest, block_expert, n_used, n_valid = _routing_tables(
        idx[:, :TOP_K], rank[:, :TOP_K], count[0, :N_EXPERTS], nblk)
    xs = _sc_row_scatter(x1p, dest, nblk * MOE_BLOCK)
    ys = _moe_experts(block_expert, n_used, n_valid, xs, w_exp_gate[l], w_exp_up[l], w_exp_down[l])
    yp, y_smp = _moe_combine(dest, x1, gate, ys, w_sh_gate[l].astype(BF16), w_sh_up[l].astype(BF16),
                             w_sh_down[l].astype(BF16), ln2_g[l].reshape(1, -1), ln2_b[l].reshape(1, -1), geo, seq)

    ys_out = y_smp.reshape(DB, steps, D_MODEL)
    ckv_p = ckv[:Rp].reshape(B, tp, MLA_KV_RANK)[:, pad_front:][None]
    kr_p = kr[:Rp].reshape(B, tp, LANES)[:, pad_front:, :MLA_ROPE][None]
    ckv_s = ckv[Rp:].reshape(DB, steps, MLA_KV_RANK)[None]
    kr_s = kr[Rp:, :MLA_ROPE].reshape(DB, steps, MLA_ROPE)[None]
    return (yp, ys_out, ckv_p, kr_p, s_prompt[None], ckv_s, kr_s, s_sample[None])
```

```python
import functools

import jax
import jax.numpy as jnp
from jax import lax
from jax.experimental import pallas as pl
from jax.experimental.pallas import tpu as pltpu
from jax.experimental.pallas import tpu_sc as plsc

F32 = jnp.float32
BF16 = jnp.bfloat16
U32 = jnp.uint32
I32 = jnp.int32

D_MODEL = 1024
MLA_HEADS = 8
MLA_Q_RANK = 256
MLA_KV_RANK = 128
MLA_NOPE = 64
MLA_ROPE = 32
MLA_V = 64
MLA_SCALE = (MLA_NOPE + MLA_ROPE) ** -0.5
LOG2E = 1.4426950408889634
Q_SCALE = MLA_SCALE * LOG2E
ROPE_THETA = 10000.0
HG_HEADS = 4
HG_DK = 128
HG_DV = 128
HG_CHUNK = 64
N_EXPERTS = 64
TOP_K = 8
D_EXPERT = 256
ROUTED_SCALE = 2.5
NORM_EPS = 1e-6
DEPTH = 1
ALPHA = (2.0 * DEPTH) ** 0.25

LANES = 128
ROW_TILE = 256
QCAT = 256
ATT_TK = 256
MOE_BLOCK = 512
PAGES_PER_CHUNK = 64
ONES_COL = LANES + MLA_ROPE
HG_SAMPLE_BATCH = 8
SATT_BATCH = 1
SC_CORES = 2
SC_SUBCORES = 16
SC_GATHER_ROWS = 64
SC_SCATTER_TOKENS = 8
NEG_BIG = -1e30
VMEM_LIMIT = 52 * 1024 * 1024

C_CQ = (0, 256)
C_CKV = (256, 384)
C_KR = (384, 512)
C_HQ = (512, 1024)
C_HF = (1024, 1536)
C_HI = (1536, 2048)
C_HG = (2048, 2560)
C_GM = (2560, 3584)
C_GH = (3584, 4608)
D_IN_PACKED = 4608


def _cparams(*sem):
    return pltpu.CompilerParams(dimension_semantics=sem, vmem_limit_bytes=VMEM_LIMIT)


def _dot(a, b):
    return jnp.dot(a, b, preferred_element_type=F32)


def _dot_nt(a, b):
    return lax.dot_general(a, b, (((1,), (1,)), ((), ())), preferred_element_type=F32)


def _dot_tn(a, b):
    return lax.dot_general(a, b, (((0,), (0,)), ((), ())), preferred_element_type=F32)


def _pack_bf16_pairs(x):
    w = x.shape[1] // 2
    bits = pltpu.bitcast(x.astype(BF16).astype(F32), U32)
    return bits[:, w:] | (bits[:, :w] >> 16)


def _unpack_bf16_pairs(p):
    lo = pltpu.bitcast(p << 16, F32)
    hi = pltpu.bitcast(p & jnp.uint32(0xFFFF0000), F32)
    return jnp.concatenate([lo, hi], axis=1)


def _x_specs(geo, seq_tiles):
    B, tpb, npt = geo["B"], geo["tpb"], geo["npt"]

    def xp_map(i):
        return (jnp.minimum(i // tpb, B - 1), jnp.where(i < npt, jnp.maximum(i % tpb - 1, 0), seq_tiles - 1), 0)

    return [pl.BlockSpec((1, ROW_TILE, D_MODEL), xp_map),
            pl.BlockSpec((ROW_TILE, D_MODEL), lambda i: (0, 0)),
            pl.BlockSpec((ROW_TILE, D_MODEL), lambda i: (jnp.maximum(i - npt, 0), 0))]


def _select_x(i, xp_ref, xh_ref, xs_ref, tiles_per_batch, n_prompt_tiles):
    is_head = (i < n_prompt_tiles) & (i % tiles_per_batch == 0)
    return jnp.where(i >= n_prompt_tiles, xs_ref[...], jnp.where(is_head, xh_ref[...], xp_ref[0]))


def _split_specs(width, npt):
    return [pl.BlockSpec((ROW_TILE, width), lambda i: (jnp.minimum(i, npt - 1), 0)),
            pl.BlockSpec((ROW_TILE, width), lambda i: (jnp.maximum(i - npt, 0), 0))]


def _wprep_kernel(wq_nope_ref, wuk_ref, wuv_ref, wbr_ref, wqlat_ref, wo_ref):
    hp = lax.Precision.HIGHEST
    for h in range(MLA_HEADS):
        a = wq_nope_ref[h]
        b = wuk_ref[h]
        ql = lax.dot_general(a, b, (((1,), (1,)), ((), ())), precision=hp, preferred_element_type=F32)
        wqlat_ref[:, h * MLA_KV_RANK:(h + 1) * MLA_KV_RANK] = (ql * Q_SCALE).astype(BF16)
        c = wuv_ref[h]
        d = wbr_ref[h]
        wo_ref[h * MLA_KV_RANK:(h + 1) * MLA_KV_RANK, :] = jnp.dot(
            c, d, precision=hp, preferred_element_type=F32).astype(BF16)


def _weight_prep(w_uq, w_uk, w_uv, w_br_mla):
    wq3 = w_uq.reshape(MLA_Q_RANK, MLA_HEADS, MLA_NOPE + MLA_ROPE)
    wq_nope = jnp.transpose(wq3[:, :, :MLA_NOPE], (1, 0, 2))
    wuk = jnp.transpose(w_uk, (1, 0, 2))
    wuv = jnp.transpose(w_uv, (1, 0, 2))
    wbr = w_br_mla.reshape(MLA_HEADS, MLA_V, D_MODEL)
    return pl.pallas_call(
        _wprep_kernel,
        out_shape=(jax.ShapeDtypeStruct((MLA_Q_RANK, MLA_HEADS * MLA_KV_RANK), BF16),
                   jax.ShapeDtypeStruct((MLA_HEADS * MLA_KV_RANK, D_MODEL), BF16)),
        compiler_params=pltpu.CompilerParams(vmem_limit_bytes=VMEM_LIMIT),
        name="weight_prep",
    )(wq_nope, wuk, wuv, wbr)


def _inproj_kernel(xp_ref, xh_ref, xs_ref, w_ref, qn_ref, kvn_ref, wqlat_ref, wqr_ref, perm_ref, cos_ref, sin_ref,
                   lb_ref, qcat_ref, kcat_ref, ckv_ref, kr_ref, hq_ref, hk_ref, hv_ref, lf_ref, sg_ref, sm_ref,
                   sh_ref, *, pad_front, tiles_per_batch, n_prompt_tiles):
    i = pl.program_id(0)
    tm = xh_ref.shape[0]
    xb = _select_x(i, xp_ref, xh_ref, xs_ref, tiles_per_batch, n_prompt_tiles).astype(BF16)

    def proj(c):
        return _dot(xb, w_ref[:, c[0]:c[1]])

    cos8 = cos_ref[...]
    sin8 = sin_ref[...]

    cq = proj(C_CQ)
    cqn = cq * lax.rsqrt(jnp.mean(cq * cq, axis=-1, keepdims=True) + NORM_EPS) * qn_ref[...]
    cqb = cqn.astype(BF16)
    qlat = _dot(cqb, wqlat_ref[...])
    qr = _dot(cqb, wqr_ref[...])
    x1, x2 = qr[:, :LANES], qr[:, LANES:]
    qrot = jnp.concatenate([x1 * cos8 - x2 * sin8, x2 * cos8 + x1 * sin8], axis=1).astype(BF16)
    qrh = _dot(qrot, perm_ref[...])
    for h in range(MLA_HEADS):
        qcat_ref[h, :, :LANES] = qlat[:, h * LANES:(h + 1) * LANES].astype(BF16)
        qcat_ref[h, :, LANES:] = qrh[:, h * LANES:(h + 1) * LANES].astype(BF16)

    kv = proj(C_CKV)
    ckv = kv * lax.rsqrt(jnp.mean(kv * kv, axis=-1, keepdims=True) + NORM_EPS) * kvn_ref[...]
    ckv_ref[...] = ckv
    kcat_ref[:, :LANES] = ckv.astype(BF16)
    krr = proj(C_KR)
    lane = lax.broadcasted_iota(I32, (tm, LANES), 1)
    half = MLA_ROPE // 2
    rot = jnp.where(lane < half, -pltpu.roll(krr, LANES - half, 1), pltpu.roll(krr, half, 1))
    kr = jnp.where(lane < MLA_ROPE, krr * cos8 + rot * sin8, 0.0)
    kr_ref[...] = kr
    kcat_ref[:, LANES:] = jnp.where(lane == ONES_COL - LANES, 1.0, kr).astype(BF16)

    row = lax.broadcasted_iota(I32, (tm, 1), 0)
    is_pad = (i < n_prompt_tiles) & (i % tiles_per_batch == 0) & (row < pad_front)
    keep = jnp.where(is_pad, 0.0, 1.0)
    hq_ref[...] = proj(C_HQ).astype(BF16)
    lb = lb_ref[...]
    f = lb + (1.0 - lb) * jax.nn.sigmoid(proj(C_HF))
    lf_ref[...] = jnp.log(f) * keep
    hk_ref[...] = ((1.0 - f) * keep).astype(BF16)
    hv_ref[...] = proj(C_HI).astype(BF16)

    g = proj(C_HG)
    sg_ref[...] = (g * jax.nn.sigmoid(g)).astype(BF16)
    sm_ref[...] = jax.nn.sigmoid(proj(C_GM)).astype(BF16)
    sh_ref[...] = jax.nn.sigmoid(proj(C_GH)).astype(BF16)


def _in_projection(x_prompt, x_head, x_smp, w_in_p, q_norm, kv_norm, wqlat, wqr, perm, cos8, sin8, lb, geo):
    R = cos8.shape[0]
    nt = R // ROW_TILE
    hgw = HG_HEADS * HG_DK
    x_specs = _x_specs(geo, x_prompt.shape[1] // ROW_TILE)

    def rows(w):
        return pl.BlockSpec((ROW_TILE, w), lambda i: (i, 0))

    def full(a):
        return pl.BlockSpec(a.shape, lambda i: (0,) * a.ndim)

    out_widths = [(2 * LANES, BF16), (LANES, F32), (LANES, F32),
                  (hgw, BF16), (hgw, BF16), (hgw, BF16), (hgw, F32), (hgw, BF16),
                  (D_MODEL, BF16), (D_MODEL, BF16)]
    qcat_spec = pl.BlockSpec((MLA_HEADS, ROW_TILE, QCAT), lambda i: (0, i, 0))
    return pl.pallas_call(
        functools.partial(_inproj_kernel, pad_front=geo["pad_front"], tiles_per_batch=geo["tpb"],
                          n_prompt_tiles=geo["npt"]),
        grid=(nt,),
        in_specs=x_specs + [full(w_in_p), full(q_norm), full(kv_norm), full(wqlat), full(wqr), full(perm),
                            rows(LANES), rows(LANES), full(lb)],
        out_specs=[qcat_spec] + [rows(w) for w, _ in out_widths],
        out_shape=[jax.ShapeDtypeStruct((MLA_HEADS, R, QCAT), BF16)]
        + [jax.ShapeDtypeStruct((R, w), dt) for w, dt in out_widths],
        compiler_params=_cparams("arbitrary"),
        name="in_projection",
    )(x_prompt, x_head, x_smp, w_in_p, q_norm, kv_norm, wqlat, wqr, perm, cos8, sin8, lb)


def _softmax_step(s, v_b, m_ref, l_ref, acc_ref):
    n = s.shape[1] // LANES
    m_prev = m_ref[...]
    m_next = jnp.maximum(m_prev, jnp.max(s, axis=1, keepdims=True))
    p = jnp.concatenate([jnp.exp2(s[:, j * LANES:(j + 1) * LANES] - m_next) for j in range(n)], axis=1)
    alpha = jnp.exp2(m_prev - m_next)
    pv = _dot(p.astype(BF16), v_b)
    if v_b.shape[1] == LANES:
        l_ref[...] = alpha * l_ref[...] + jnp.sum(p, axis=1, keepdims=True)
    else:
        l_ref[...] = alpha * l_ref[...] + pv[:, LANES:]
    acc_ref[...] = alpha * acc_ref[...] + pv[:, :LANES]
    m_ref[...] = m_next


def _pattn_kernel(q_ref, k_ref, o_ref, m_ref, l_ref, acc_ref, *, pad_front):
    qi = pl.program_id(1)
    nh, tq, _ = q_ref.shape
    tk = ATT_TK
    rows = nh * tq

    q = q_ref[...].reshape(rows, QCAT)
    m_ref[...] = jnp.full(m_ref.shape, -jnp.inf, F32)
    l_ref[...] = jnp.zeros(l_ref.shape, F32)
    acc_ref[...] = jnp.zeros(acc_ref.shape, F32)

    def step(kb, masked):
        kblk = k_ref[pl.ds(pl.multiple_of(kb * tk, tk), tk), :]
        s = _dot_nt(q, kblk)
        if masked:
            qrow = qi * tq + lax.broadcasted_iota(I32, (rows, tk), 0) % tq
            krow = kb * tk + lax.broadcasted_iota(I32, (rows, tk), 1)
            s = jnp.where((krow <= qrow) & (krow >= pad_front), s, NEG_BIG)
        _softmax_step(s, kblk, m_ref, l_ref, acc_ref)

    step(0, True)

    def body(kb, c):
        step(kb, False)
        return c

    lax.fori_loop(1, qi, body, 0)

    @pl.when(qi > 0)
    def _():
        step(qi, True)

    row_sum = l_ref[:, ONES_COL - LANES:ONES_COL - LANES + 1]
    o = (acc_ref[...] / row_sum).astype(BF16)
    for h in range(nh):
        o_ref[:, h * LANES:(h + 1) * LANES] = o[h * tq:(h + 1) * tq]


def _prompt_attention(qcat, kcat, geo):
    B, tpb, tp = geo["B"], geo["tpb"], geo["tp"]
    rows = MLA_HEADS * ROW_TILE
    return pl.pallas_call(
        functools.partial(_pattn_kernel, pad_front=geo["pad_front"]),
        grid=(B, tpb),
        in_specs=[pl.BlockSpec((MLA_HEADS, ROW_TILE, QCAT), lambda b, i: (0, b * tpb + i, 0)),
                  pl.BlockSpec((tp, 2 * LANES), lambda b, i: (b, 0))],
        out_specs=pl.BlockSpec((ROW_TILE, MLA_HEADS * LANES), lambda b, i: (b * tpb + i, 0)),
        out_shape=jax.ShapeDtypeStruct((B * tp, MLA_HEADS * LANES), BF16),
        scratch_shapes=[pltpu.VMEM((rows, LANES), F32)] * 3,
        compiler_params=_cparams("arbitrary", "arbitrary"),
        name="prompt_attention",
    )(qcat, kcat)


def _sattn_kernel(pt_ref, q_ref, knew_ref, ckv_hbm, kr_hbm, o_ref,
                  ckv_buf, kr_buf, sem, m_ref, l_ref, acc_ref, *, n_chunks, page):
    g = pl.program_id(0)
    ng = pl.num_programs(0)
    ch = PAGES_PER_CHUNK
    nh, nbt, S = q_ref.shape[0], q_ref.shape[1], q_ref.shape[2]
    rows = nh * S

    def chunk_copies(gg, c, slot):
        cps = []
        for bb in range(nbt):
            for j in range(ch):
                pg = pt_ref[gg * nbt + bb, c * ch + j]
                cps.append(pltpu.make_async_copy(ckv_hbm.at[pg], ckv_buf.at[slot, bb, pl.ds(j * page, page), :],
                                                 sem.at[0, slot]))
                cps.append(pltpu.make_async_copy(kr_hbm.at[pg], kr_buf.at[slot, bb, :, pl.ds(j * page, page)],
                                                 sem.at[1, slot]))
        return cps

    @pl.when(g == 0)
    def _():
        for cp in chunk_copies(0, 0, 0):
            cp.start()

    qlat, qrope = [], []
    for bb in range(nbt):
        q = q_ref[:, bb].reshape(rows, QCAT)
        qlat.append(q[:, :LANES])
        qrope.append(q[:, LANES:LANES + MLA_ROPE])
        knew = knew_ref[bb]
        s_new = _dot_nt(q, knew)
        qtok = lax.broadcasted_iota(I32, (rows, S), 0) % S
        ktok = lax.broadcasted_iota(I32, (rows, S), 1)
        s_new = jnp.where(ktok <= qtok, s_new, NEG_BIG)
        m0 = jnp.max(s_new, axis=1, keepdims=True)
        p0 = jnp.exp2(s_new - m0)
        m_ref[bb] = jnp.broadcast_to(m0, (rows, LANES))
        l_ref[bb] = jnp.broadcast_to(jnp.sum(p0, axis=1, keepdims=True), (rows, LANES))
        acc_ref[bb] = _dot(p0.astype(BF16), knew[:, :LANES])

    for c in range(n_chunks):
        slot = c % 2 if n_chunks % 2 == 0 else (g * n_chunks + c) % 2
        if c + 1 < n_chunks:
            for cp in chunk_copies(g, c + 1, 1 - slot):
                cp.start()
        else:
            @pl.when(g + 1 < ng)
            def _():
                for cp in chunk_copies(g + 1, 0, 1 - slot):
                    cp.start()
        for cp in chunk_copies(g, c, slot):
            cp.wait()
        for bb in range(nbt):
            ckv_b = ckv_buf[slot, bb].astype(BF16)
            krt_b = kr_buf[slot, bb].astype(BF16)
            s = _dot_nt(qlat[bb], ckv_b) + _dot(qrope[bb], krt_b)
            _softmax_step(s, ckv_b, m_ref.at[bb], l_ref.at[bb], acc_ref.at[bb])

    for bb in range(nbt):
        o = (acc_ref[bb] / l_ref[bb]).astype(BF16)
        for h in range(nh):
            o_ref[bb, :, h * LANES:(h + 1) * LANES] = o[h * S:(h + 1) * S]


def _sample_attention(page_table, q_s, knew_s, cache_ckv, cache_krt):
    nh, DB, S, _ = q_s.shape
    rows = nh * S
    n_pages = page_table.shape[1]
    page = cache_ckv.shape[1]
    assert n_pages % PAGES_PER_CHUNK == 0
    n_chunks = n_pages // PAGES_PER_CHUNK
    ck = PAGES_PER_CHUNK * page
    nbt = SATT_BATCH
    assert DB % nbt == 0
    grid_spec = pltpu.PrefetchScalarGridSpec(
        num_scalar_prefetch=1,
        grid=(DB // nbt,),
        in_specs=[pl.BlockSpec((nh, nbt, S, QCAT), lambda g, pt: (0, g, 0, 0)),
                  pl.BlockSpec((nbt, S, 2 * LANES), lambda g, pt: (g, 0, 0)),
                  pl.BlockSpec(memory_space=pl.ANY),
                  pl.BlockSpec(memory_space=pl.ANY)],
        out_specs=pl.BlockSpec((nbt, S, nh * LANES), lambda g, pt: (g, 0, 0)),
        scratch_shapes=[pltpu.VMEM((2, nbt, ck, MLA_KV_RANK), F32),
                        pltpu.VMEM((2, nbt, MLA_ROPE, ck), F32),
                        pltpu.SemaphoreType.DMA((2, 2)),
                        pltpu.VMEM((nbt, rows, LANES), F32),
                        pltpu.VMEM((nbt, rows, LANES), F32),
                        pltpu.VMEM((nbt, rows, LANES), F32)])
    return pl.pallas_call(
        functools.partial(_sattn_kernel, n_chunks=n_chunks, page=page),
        grid_spec=grid_spec,
        out_shape=jax.ShapeDtypeStruct((DB, S, nh * LANES), BF16),
        compiler_params=_cparams("arbitrary"),
        name="sample_attention",
    )(page_table, q_s, knew_s, cache_ckv, cache_krt)


def _split3(x):
    hi = x.astype(BF16)
    r1 = x - hi.astype(F32)
    mid = r1.astype(BF16)
    lo = (r1 - mid.astype(F32)).astype(BF16)
    return hi, mid, lo


def _hgrn_chunk(q, k, v, lf, S0, tri, group):
    C = q.shape[0]
    hi, mid, lo = _split3(lf)
    cum = _dot(tri, hi) + _dot(tri, mid) + _dot(tri, lo)
    vb = v.astype(BF16)
    rowi = lax.broadcasted_iota(I32, (C, C), 0)
    coli = lax.broadcasted_iota(I32, (C, C), 1)
    attn = jnp.zeros((C, C), F32)

    bd = min(16, group)
    hs = group // 2
    rid = lax.broadcasted_iota(I32, (C, 1), 0)
    while hs >= bd:
        npair = C // (2 * hs)
        ref = jnp.concatenate(
            [jnp.broadcast_to(cum[(2 * j + 1) * hs - 1:(2 * j + 1) * hs, :], (2 * hs, cum.shape[1]))
             for j in range(npair)], axis=0)
        odd = ((rid // hs) % 2) == 1
        e = jnp.exp(jnp.where(odd, cum - ref, ref - cum))
        qs = jnp.where(odd, q * e, 0.0).astype(BF16)
        ks = jnp.where(odd, 0.0, k * e).astype(BF16)
        a = _dot_nt(qs, ks)
        attn = attn + jnp.where((rowi // (2 * hs)) == (coli // (2 * hs)), a, 0.0)
        hs //= 2

    nblk = C // bd
    k3 = k.reshape(nblk, bd, k.shape[1])
    c3 = cum.reshape(nblk, bd, cum.shape[1])
    tl = rid % bd
    blk_base = (rowi // bd) * bd
    for sl in range(bd):
        ks_b = jnp.broadcast_to(k3[:, sl:sl + 1, :], k3.shape).reshape(C, k.shape[1])
        cs_b = jnp.broadcast_to(c3[:, sl:sl + 1, :], c3.shape).reshape(C, k.shape[1])
        e = jnp.exp(jnp.where(tl >= sl, cum - cs_b, NEG_BIG))
        col = jnp.sum(q * ks_b * e, axis=1, keepdims=True)
        attn = jnp.where(coli == blk_base + sl, col, attn)
    o_intra = _dot(attn.astype(BF16), vb)
    return o_intra, cum


def _tri(C, group):
    r = lax.broadcasted_iota(I32, (C, C), 0)
    c = lax.broadcasted_iota(I32, (C, C), 1)
    return jnp.where((c <= r) & (r // group == c // group), 1.0, 0.0).astype(BF16)


def _state_update(q, k, v, cum, S, o_intra):
    C = q.shape[0]
    last = cum[C - 1:C, :]
    o = o_intra + _dot((q * jnp.exp(cum)).astype(BF16), S.astype(BF16))
    kst = (k * jnp.exp(last - cum)).astype(BF16)
    dfull = jnp.transpose(jnp.broadcast_to(jnp.exp(last), (S.shape[1], S.shape[0])))
    S_new = dfull * S + _dot_tn(kst, v.astype(BF16))
    return o, S_new


def _hgrn_prompt_kernel(q_ref, k_ref, v_ref, lf_ref, o_ref, s_out_ref, s_ref):
    t = pl.program_id(1)

    @pl.when(t == 0)
    def _():
        s_ref[...] = jnp.zeros(s_ref.shape, F32)

    C = HG_CHUNK
    tri = _tri(C, C)
    states = [s_ref[h] for h in range(HG_HEADS)]
    for c in range(q_ref.shape[0] // C):
        sl = slice(c * C, (c + 1) * C)
        for h in range(HG_HEADS):
            hl = slice(h * HG_DK, (h + 1) * HG_DK)
            q = q_ref[sl, hl].astype(F32)
            k = k_ref[sl, hl].astype(F32)
            v = v_ref[sl, hl].astype(F32)
            lf = lf_ref[sl, hl]
            o_intra, cum = _hgrn_chunk(q, k, v, lf, None, tri, C)
            o, states[h] = _state_update(q, k, v, cum, states[h], o_intra)
            o_ref[sl, hl] = o.astype(BF16)
    for h in range(HG_HEADS):
        s_ref[h] = states[h]

    @pl.when(t == pl.num_programs(1) - 1)
    def _():
        s_out_ref[0] = s_ref[...]


def _hgrn_prompt(hq, hk, hv, lf, geo):
    B, tpb, tp = geo["B"], geo["tpb"], geo["tp"]
    hgw = HG_HEADS * HG_DK

    def blk():
        return pl.BlockSpec((ROW_TILE, hgw), lambda b, t: (b * tpb + t, 0))

    return pl.pallas_call(
        _hgrn_prompt_kernel,
        grid=(B, tpb),
        in_specs=[blk(), blk(), blk(), blk()],
        out_specs=[blk(), pl.BlockSpec((1, HG_HEADS, HG_DK, HG_DV), lambda b, t: (b, 0, 0, 0))],
        out_shape=[jax.ShapeDtypeStruct((B * tp, hgw), BF16),
                   jax.ShapeDtypeStruct((B, HG_HEADS, HG_DK, HG_DV), F32)],
        scratch_shapes=[pltpu.VMEM((HG_HEADS, HG_DK, HG_DV), F32)],
        compiler_params=_cparams("arbitrary", "arbitrary"),
        name="hgrn_prompt",
    )(hq, hk, hv, lf)


def _hgrn_sample_kernel(q_ref, k_ref, v_ref, lf_ref, s_in_ref, o_ref, s_out_ref, *, steps):
    nb = s_in_ref.shape[0]
    C = nb * steps
    tri = _tri(C, steps)
    for h in range(HG_HEADS):
        sl = slice(h * HG_DK, (h + 1) * HG_DK)
        q = q_ref[:, sl].astype(F32)
        k = k_ref[:, sl].astype(F32)
        v = v_ref[:, sl].astype(F32)
        lf = lf_ref[:, sl]
        o_intra, cum = _hgrn_chunk(q, k, v, lf, None, tri, steps)
        for b in range(nb):
            r = slice(b * steps, (b + 1) * steps)
            o, S_new = _state_update(q[r], k[r], v[r], cum[r], s_in_ref[b, h], o_intra[r])
            o_ref[r, sl] = o.astype(BF16)
            s_out_ref[b, h] = S_new


def _hgrn_sample(hq, hk, hv, lf, state, row0, steps):
    DB = state.shape[0]
    nb = HG_SAMPLE_BATCH
    rows = nb * steps
    hgw = HG_HEADS * HG_DK
    blk0 = row0 // rows

    def tok():
        return pl.BlockSpec((rows, hgw), lambda i: (blk0 + i, 0))

    st = pl.BlockSpec((nb, HG_HEADS, HG_DK, HG_DV), lambda i: (i, 0, 0, 0))
    return pl.pallas_call(
        functools.partial(_hgrn_sample_kernel, steps=steps),
        grid=(DB // nb,),
        in_specs=[tok(), tok(), tok(), tok(), st],
        out_specs=[pl.BlockSpec((rows, hgw), lambda i: (i, 0)), st],
        out_shape=[jax.ShapeDtypeStruct((DB * steps, hgw), BF16),
                   jax.ShapeDtypeStruct(state.shape, F32)],
        compiler_params=_cparams("arbitrary"),
        name="hgrn_sample",
    )(hq, hk, hv, lf, state)


def _layer_norm(x, g, b):
    mu = jnp.mean(x, axis=-1, keepdims=True)
    xc = x - mu
    var = jnp.mean(xc * xc, axis=-1, keepdims=True)
    return xc * lax.rsqrt(var + NORM_EPS) * g + b


def _tail_kernel(xp_ref, xh_ref, xs_ref, olp_ref, ols_ref, ohp_ref, ohs_ref, sg_ref, sm_ref, sh_ref, wo_ref, wbh_ref,
                 wout_ref, hgn_ref, g1_ref, b1_ref, rw_ref, rb_ref, x1_ref, x1p_ref, idx_ref, gate_ref, rank_ref,
                 count_ref, cnt_ref, *, tiles_per_batch, n_prompt_tiles):
    i = pl.program_id(0)
    tm = xh_ref.shape[0]
    x_in = _select_x(i, xp_ref, xh_ref, xs_ref, tiles_per_batch, n_prompt_tiles)
    is_sample = i >= n_prompt_tiles
    mla = _dot(jnp.where(is_sample, ols_ref[...], olp_ref[...]), wo_ref[...])
    oh = jnp.where(is_sample, ohs_ref[...], ohp_ref[...]).astype(F32)
    parts = []
    for h in range(HG_HEADS):
        y = oh[:, h * HG_DV:(h + 1) * HG_DV]
        parts.append(y * lax.rsqrt(jnp.mean(y * y, axis=-1, keepdims=True) + NORM_EPS) * hgn_ref[...])
    hg = (jnp.concatenate(parts, axis=1) * sg_ref[...].astype(F32)).astype(BF16)
    merged = sm_ref[...].astype(F32) * mla + sh_ref[...].astype(F32) * _dot(hg, wbh_ref[...])
    x1 = _layer_norm(ALPHA * x_in + _dot(merged.astype(BF16), wout_ref[...]), g1_ref[...], b1_ref[...])
    x1_ref[...] = x1
    x1p_ref[...] = pltpu.bitcast(_pack_bf16_pairs(x1), I32)

    x1_hi = x1.astype(BF16)
    x1_lo = (x1 - x1_hi.astype(F32)).astype(BF16)
    logits = _dot(x1_hi, rw_ref[0]) + _dot(x1_lo, rw_ref[0]) + _dot(x1_hi, rw_ref[1])
    scores = jax.nn.sigmoid(logits)
    lane = lax.broadcasted_iota(I32, (tm, LANES), 1).astype(F32)
    remaining = jnp.where(lane < N_EXPERTS, scores + rb_ref[...], -jnp.inf)
    idx_out = jnp.zeros((tm, LANES), F32)
    gate_out = jnp.zeros((tm, LANES), F32)
    hits = []
    for kk in range(TOP_K):
        mx = jnp.max(remaining, axis=1, keepdims=True)
        pick = jnp.min(jnp.where(remaining == mx, lane, float(LANES)), axis=1, keepdims=True)
        hit = lane == pick
        hits.append(hit)
        gval = jnp.sum(jnp.where(hit, scores, 0.0), axis=1, keepdims=True)
        idx_out = jnp.where(lane == kk, pick, idx_out)
        gate_out = jnp.where(lane == kk, gval, gate_out)
        remaining = jnp.where(hit, -jnp.inf, remaining)
    gate_out = gate_out / jnp.sum(gate_out, axis=1, keepdims=True) * ROUTED_SCALE
    idx_ref[...] = idx_out.astype(I32)
    gate_ref[...] = gate_out

    @pl.when(pl.program_id(0) == 0)
    def _():
        cnt_ref[...] = jnp.zeros(cnt_ref.shape, F32)

    sel = jnp.where(remaining == -jnp.inf, 1.0, 0.0) * jnp.where(lane < N_EXPERTS, 1.0, 0.0)
    r_i = lax.broadcasted_iota(I32, (tm, tm), 0)
    c_i = lax.broadcasted_iota(I32, (tm, tm), 1)
    before = _dot(jnp.where(c_i < r_i, 1.0, 0.0).astype(BF16), sel.astype(BF16)) + cnt_ref[0:1, :]
    rank_out = jnp.zeros((tm, LANES), F32)
    for kk in range(TOP_K):
        rk = jnp.sum(jnp.where(hits[kk], before, 0.0), axis=1, keepdims=True)
        rank_out = jnp.where(lane == kk, rk, rank_out)
    rank_ref[...] = rank_out.astype(I32)
    total = cnt_ref[0:1, :] + jnp.sum(sel, axis=0, keepdims=True)
    cnt_ref[...] = jnp.broadcast_to(total, cnt_ref.shape)
    count_ref[...] = jnp.broadcast_to(total, count_ref.shape).astype(I32)


def _layer_tail(x_prompt, x_head, x_smp, o_lat_p, o_lat_s, o_hg_p, o_hg_s, sg, sm, sh, wo, w_br_hg, w_out, hg_norm,
                ln1_g, ln1_b, rw_p, rb_p, geo):
    R = sg.shape[0]
    npt = geo["npt"]

    def rows(w):
        return pl.BlockSpec((ROW_TILE, w), lambda i: (i, 0))

    def full(a):
        return pl.BlockSpec(a.shape, lambda i: (0,) * a.ndim)

    hgw = HG_HEADS * HG_DV
    return pl.pallas_call(
        functools.partial(_tail_kernel, tiles_per_batch=geo["tpb"], n_prompt_tiles=npt),
        grid=(R // ROW_TILE,),
        in_specs=_x_specs(geo, x_prompt.shape[1] // ROW_TILE)
        + _split_specs(MLA_HEADS * LANES, npt) + _split_specs(hgw, npt)
        + [rows(hgw), rows(D_MODEL), rows(D_MODEL),
           full(wo), full(w_br_hg), full(w_out), full(hg_norm), full(ln1_g), full(ln1_b),
           full(rw_p), full(rb_p)],
        out_specs=[rows(D_MODEL), rows(D_MODEL // 2), rows(LANES), rows(LANES), rows(LANES),
                   pl.BlockSpec((8, LANES), lambda i: (0, 0))],
        out_shape=[jax.ShapeDtypeStruct((R, D_MODEL), F32), jax.ShapeDtypeStruct((R, D_MODEL // 2), I32),
                   jax.ShapeDtypeStruct((R, LANES), I32), jax.ShapeDtypeStruct((R, LANES), F32),
                   jax.ShapeDtypeStruct((R, LANES), I32), jax.ShapeDtypeStruct((8, LANES), I32)],
        scratch_shapes=[pltpu.VMEM((8, LANES), F32)],
        compiler_params=_cparams("arbitrary"),
        name="layer_tail",
    )(x_prompt, x_head, x_smp, o_lat_p, o_lat_s, o_hg_p, o_hg_s, sg, sm, sh, wo, w_br_hg, w_out, hg_norm,
      ln1_g, ln1_b, rw_p, rb_p)


def _sc_row_scatter(x, dest, n_slots):
    R, d = x.shape
    nw = SC_CORES * SC_SUBCORES
    T = SC_SCATTER_TOKENS
    rows = T * TOP_K
    per_w = R // nw
    assert R % (nw * 2 * T) == 0
    n_pairs = per_w // (2 * T)
    order = jnp.transpose(dest.reshape(R // T, T, TOP_K), (0, 2, 1)).reshape(R * TOP_K)
    mesh = plsc.VectorSubcoreMesh(core_axis_name="c", subcore_axis_name="s",
                                  num_cores=SC_CORES, num_subcores=SC_SUBCORES)

    @functools.partial(
        pl.kernel, mesh=mesh, out_type=jax.ShapeDtypeStruct((n_slots, d), x.dtype),
        scratch_types=[pltpu.VMEM((rows,), I32), pltpu.VMEM((rows,), I32),
                       pltpu.VMEM((rows, d), x.dtype), pltpu.VMEM((rows, d), x.dtype),
                       pltpu.SemaphoreType.DMA, pltpu.SemaphoreType.DMA, pltpu.SemaphoreType.DMA],
        name="moe_dispatch_sc")
    def scatter(x_hbm, idx_hbm, out_hbm, idx_a, idx_b, rows_a, rows_b, sem_a, sem_b, sem_out):
        wid = lax.axis_index("s") * SC_CORES + lax.axis_index("c")
        tbase = wid * per_w
        bufs = ((idx_a, rows_a, sem_a), (idx_b, rows_b, sem_b))

        def loads(c, buf):
            idx_v, rows_v, sem = buf
            t0 = pl.multiple_of(tbase + c * T, T)
            off = pl.multiple_of(t0 * TOP_K, rows)
            cps = [pltpu.make_async_copy(idx_hbm.at[pl.ds(off, rows)], idx_v, sem)]
            for k in range(TOP_K):
                cps.append(pltpu.make_async_copy(x_hbm.at[pl.ds(t0, T)], rows_v.at[pl.ds(k * T, T)], sem))
            return cps

        def load(c, buf):
            for cp in loads(c, buf):
                cp.start()

        def send(c, buf):
            idx_v, rows_v, _ = buf
            for cp in loads(c, buf):
                cp.wait()
            pltpu.async_copy(rows_v, out_hbm.at[idx_v], sem_out).wait()

        load(0, bufs[0])

        def body(p, carry):
            c = 2 * p
            load(c + 1, bufs[1])
            send(c, bufs[0])

            @pl.when(p + 1 < n_pairs)
            def _():
                load(c + 2, bufs[0])

            send(c + 1, bufs[1])
            return carry

        lax.fori_loop(0, n_pairs, body, 0)

    return scatter(x, order)


def _expert_kernel(be_ref, nused_ref, nvalid_ref, xs_ref, wg_ref, wu_ref, wd_ref, ys_ref, wgb_ref, wub_ref, wdb_ref):
    i = pl.program_id(0)

    @pl.when((i == 0) | (be_ref[i] != be_ref[jnp.maximum(i - 1, 0)]))
    def _():
        wgb_ref[...] = wg_ref[0].astype(BF16)
        wub_ref[...] = wu_ref[0].astype(BF16)
        wdb_ref[...] = wd_ref[0].astype(BF16)

    @pl.when(i < nused_ref[0])
    def _():
        row = lax.broadcasted_iota(I32, xs_ref.shape, 0)
        xw = jnp.where(row < nvalid_ref[i], xs_ref[...], 0)
        xb = _unpack_bf16_pairs(pltpu.bitcast(xw, U32)).astype(BF16)
        g = _dot(xb, wgb_ref[...])
        u = _dot(xb, wub_ref[...])
        hmid = (g * jax.nn.sigmoid(g) * u).astype(BF16)
        ys_ref[...] = pltpu.bitcast(_pack_bf16_pairs(_dot(hmid, wdb_ref[...])), I32)

    @pl.when(i >= nused_ref[0])
    def _():
        ys_ref[...] = jnp.zeros(ys_ref.shape, I32)


def _moe_experts(block_expert, n_used, n_valid, xs, wg, wu, wd):
    n_slots, W = xs.shape
    nblk = n_slots // MOE_BLOCK

    def xmap(i, be, nu, nv):
        return (jnp.minimum(i, nu[0] - 1), 0)

    def wmap(i, be, nu, nv):
        return (be[i], 0, 0)

    grid_spec = pltpu.PrefetchScalarGridSpec(
        num_scalar_prefetch=3,
        grid=(nblk,),
        in_specs=[pl.BlockSpec((MOE_BLOCK, W), xmap),
                  pl.BlockSpec((1, D_MODEL, D_EXPERT), wmap),
                  pl.BlockSpec((1, D_MODEL, D_EXPERT), wmap),
                  pl.BlockSpec((1, D_EXPERT, D_MODEL), wmap)],
        out_specs=pl.BlockSpec((MOE_BLOCK, W), lambda i, be, nu, nv: (i, 0)),
        scratch_shapes=[pltpu.VMEM((D_MODEL, D_EXPERT), BF16), pltpu.VMEM((D_MODEL, D_EXPERT), BF16),
                        pltpu.VMEM((D_EXPERT, D_MODEL), BF16)])
    return pl.pallas_call(
        _expert_kernel,
        grid_spec=grid_spec,
        out_shape=jax.ShapeDtypeStruct((n_slots, W), I32),
        compiler_params=_cparams("arbitrary"),
        name="moe_experts",
    )(block_expert, n_used, n_valid, xs, wg, wu, wd)


def _sc_row_gather(table, idx):
    n, d = idx.shape[0], table.shape[1]
    nw = SC_CORES * SC_SUBCORES
    per_w = n // nw
    rows = SC_GATHER_ROWS
    assert n % (nw * 2 * rows) == 0
    n_pairs = per_w // (2 * rows)
    mesh = plsc.VectorSubcoreMesh(core_axis_name="c", subcore_axis_name="s",
                                  num_cores=SC_CORES, num_subcores=SC_SUBCORES)

    @functools.partial(
        pl.kernel, mesh=mesh, out_type=jax.ShapeDtypeStruct((n, d), table.dtype),
        scratch_types=[pltpu.VMEM((per_w,), I32),
                       pltpu.VMEM((rows, d), table.dtype), pltpu.VMEM((rows, d), table.dtype),
                       pltpu.SemaphoreType.DMA, pltpu.SemaphoreType.DMA],
        name="moe_gather_sc")
    def gather(table_hbm, idx_hbm, out_hbm, idx_all, rows_a, rows_b, sem_a, sem_b):
        wid = lax.axis_index("s") * SC_CORES + lax.axis_index("c")
        base = wid * per_w
        bufs = ((rows_a, sem_a), (rows_b, sem_b))
        pltpu.sync_copy(idx_hbm.at[pl.ds(pl.multiple_of(base, rows), per_w)], idx_all)

        def stream(c, buf):
            rows_v, sem = buf
            idx_v = idx_all.at[pl.ds(pl.multiple_of(c * rows, rows), rows)]
            return pltpu.make_async_copy(table_hbm.at[idx_v], rows_v, sem)

        def start(c, buf):
            stream(c, buf).start()

        def finish(c, buf):
            off = pl.multiple_of(base + c * rows, rows)
            stream(c, buf).wait()
            pltpu.sync_copy(buf[0], out_hbm.at[pl.ds(off, rows)])

        start(0, bufs[0])

        def body(p, carry):
            c = 2 * p
            start(c + 1, bufs[1])
            finish(c, bufs[0])

            @pl.when(p + 1 < n_pairs)
            def _():
                start(c + 2, bufs[0])

            finish(c + 1, bufs[1])
            return carry

        lax.fori_loop(0, n_pairs, body, 0)

    return gather(table, idx)


def _shared_expert_kernel(x1_ref, wsg_ref, wsu_ref, wsd_ref, ff_ref):
    xb = x1_ref[...].astype(BF16)
    g = _dot(xb, wsg_ref[...])
    u = _dot(xb, wsu_ref[...])
    ff_ref[...] = _dot((g * jax.nn.sigmoid(g) * u).astype(BF16), wsd_ref[...])


def _shared_expert(x1, wsg, wsu, wsd):
    R = x1.shape[0]

    def full(a):
        return pl.BlockSpec(a.shape, lambda i: (0,) * a.ndim)

    return pl.pallas_call(
        _shared_expert_kernel,
        grid=(R // ROW_TILE,),
        in_specs=[pl.BlockSpec((ROW_TILE, D_MODEL), lambda i: (i, 0)), full(wsg), full(wsu), full(wsd)],
        out_specs=pl.BlockSpec((ROW_TILE, D_MODEL), lambda i: (i, 0)),
        out_shape=jax.ShapeDtypeStruct((R, D_MODEL), F32),
        compiler_params=_cparams("arbitrary"),
        name="shared_expert",
    )(x1, wsg, wsu, wsd)


def _combine_kernel(x1_ref, gate_ref, yk_ref, ffs_ref, g2_ref, b2_ref, yp_ref, ysmp_ref, *, n_prompt_tiles):
    i = pl.program_id(0)
    tm = x1_ref.shape[0]
    x1 = x1_ref[...]
    ff = ffs_ref[...]
    gate = gate_ref[...]
    for kk in range(TOP_K):
        ff = ff + gate[:, kk:kk + 1] * _unpack_bf16_pairs(pltpu.bitcast(yk_ref[kk * tm:(kk + 1) * tm, :], U32))
    y = _layer_norm(ALPHA * x1 + ff, g2_ref[...], b2_ref[...])

    @pl.when(i < n_prompt_tiles)
    def _():
        yp_ref[0] = y

    @pl.when(i >= n_prompt_tiles)
    def _():
        ysmp_ref[...] = y


def _moe_combine(dest, x1, gate, ys, ffs, ln2_g, ln2_b, geo, seq):
    R = x1.shape[0]
    nt = R // ROW_TILE
    W = ys.shape[1]
    B, tpb, npt = geo["B"], geo["tpb"], geo["npt"]
    assert seq == (tpb - 1) * ROW_TILE
    order = jnp.transpose(dest.reshape(nt, ROW_TILE, TOP_K), (0, 2, 1)).reshape(R * TOP_K)
    yk = _sc_row_gather(ys, order)

    def rows(w):
        return pl.BlockSpec((ROW_TILE, w), lambda i: (i, 0))

    def full(a):
        return pl.BlockSpec(a.shape, lambda i: (0,) * a.ndim)

    def rows_k(w):
        return pl.BlockSpec((TOP_K * ROW_TILE, w), lambda i: (i, 0))

    def yp_map(i):
        in_prompt = i < npt
        return (jnp.minimum(i // tpb, B - 1), jnp.where(in_prompt, jnp.maximum(i % tpb - 1, 0), tpb - 2), 0)

    return pl.pallas_call(
        functools.partial(_combine_kernel, n_prompt_tiles=npt),
        grid=(nt,),
        in_specs=[rows(D_MODEL), rows(LANES), rows_k(W), rows(D_MODEL), full(ln2_g), full(ln2_b)],
        out_specs=[pl.BlockSpec((1, ROW_TILE, D_MODEL), yp_map),
                   pl.BlockSpec((ROW_TILE, D_MODEL), lambda i: (jnp.maximum(i - npt, 0), 0))],
        out_shape=[jax.ShapeDtypeStruct((B, seq, D_MODEL), F32),
                   jax.ShapeDtypeStruct((R - npt * ROW_TILE, D_MODEL), F32)],
        compiler_params=_cparams("arbitrary"),
        name="moe_combine",
    )(x1, gate, yk, ffs, ln2_g, ln2_b)


def _routing_tables(idx8, rank8, counts, n_slots_blocks):
    R = idx8.shape[0]
    padded = (counts + MOE_BLOCK - 1) // MOE_BLOCK * MOE_BLOCK
    pends = jnp.cumsum(padded)
    pstarts = pends - padded
    experts = jnp.arange(N_EXPERTS, dtype=I32)
    start8 = jnp.sum(jnp.where(idx8[:, :, None] == experts[None, None, :], pstarts[None, None, :], 0), axis=-1)
    dest = start8 + rank8
    blk_start = jnp.arange(n_slots_blocks, dtype=I32) * MOE_BLOCK
    block_expert = jnp.minimum(jnp.sum((blk_start[:, None] >= pends[None, :]).astype(I32), axis=1), N_EXPERTS - 1)
    n_used = (pends[-1] // MOE_BLOCK).astype(I32).reshape(1)
    hit = block_expert[:, None] == experts[None, :]
    left = jnp.sum(jnp.where(hit, (counts + pstarts)[None, :], 0), axis=1) - blk_start
    n_valid = jnp.clip(left, 0, MOE_BLOCK).astype(I32)
    return dest.astype(I32), block_expert.astype(I32), n_used, n_valid


def kernel(x_prompt, x_sample, cache_mla_ckv, cache_mla_krope, state_hgrn, page_table, meta_tokens, hg_lb_logits,
           w_in, q_norm, kv_norm, w_uq, w_uk, w_uv, hg_norm, w_br_mla, w_br_hg, w_out, ln1_g, ln1_b, router_w,
           router_bias, w_exp_gate, w_exp_up, w_exp_down, w_sh_gate, w_sh_up, w_sh_down, ln2_g, ln2_b):
    assert w_in.shape[0] == DEPTH
    B, seq, _ = x_prompt.shape
    DB, steps, _ = x_sample.shape
    n_meta = meta_tokens.shape[0]
    n_pages, page = page_table.shape[1], cache_mla_ckv.shape[2]
    past = n_pages * page
    T = n_meta + seq
    tp = -(-T // ROW_TILE) * ROW_TILE
    pad_front = tp - T
    tpb = tp // ROW_TILE
    n_s = DB * steps
    assert n_s % ROW_TILE == 0 and DB % HG_SAMPLE_BATCH == 0
    geo = dict(B=B, tp=tp, tpb=tpb, pad_front=pad_front, npt=B * tpb)
    Rp = B * tp
    R = Rp + n_s

    assert pad_front + n_meta == ROW_TILE and seq % ROW_TILE == 0
    x_head = jnp.concatenate([jnp.zeros((pad_front, D_MODEL), F32), meta_tokens.astype(F32)], axis=0)
    x_smp = x_sample.reshape(n_s, D_MODEL)

    pos_p = jnp.maximum(jnp.arange(tp) - pad_front, 0)
    pos = jnp.concatenate([jnp.tile(pos_p, B), jnp.tile(past + jnp.arange(steps), DB)]).astype(F32)
    half = MLA_ROPE // 2
    inv = ROPE_THETA ** (-jnp.arange(half, dtype=F32) / half)
    ang = pos[:, None] * inv[None, :]
    cos8 = jnp.tile(jnp.cos(ang), (1, LANES // half))
    sin8 = jnp.tile(jnp.sin(ang), (1, LANES // half))

    l = 0
    win = w_in[l]
    kr_end = MLA_Q_RANK + MLA_KV_RANK + MLA_ROPE
    w_in_p = jnp.concatenate([win[:, :kr_end], jnp.zeros((D_MODEL, LANES - MLA_ROPE), F32), win[:, kr_end:]],
                             axis=1).astype(BF16)
    assert w_in_p.shape[1] == D_IN_PACKED
    wq3 = w_uq[l].reshape(MLA_Q_RANK, MLA_HEADS, MLA_NOPE + MLA_ROPE)
    wqr = (jnp.concatenate([wq3[:, :, MLA_NOPE:MLA_NOPE + half].reshape(MLA_Q_RANK, MLA_HEADS * half),
                            wq3[:, :, MLA_NOPE + half:].reshape(MLA_Q_RANK, MLA_HEADS * half)], axis=1)
           * Q_SCALE).astype(BF16)
    src = jnp.arange(2 * LANES)
    hh, ii = (src % LANES) // half, src % half
    dst = hh * LANES + ii + jnp.where(src >= LANES, half, 0)
    perm = (dst[:, None] == jnp.arange(MLA_HEADS * LANES)[None, :]).astype(BF16)
    lb = jnp.cumsum(jax.nn.softmax(hg_lb_logits.astype(F32), axis=0), axis=0)[l].reshape(1, -1)

    wqlat, wo = _weight_prep(w_uq[l], w_uk[l], w_uv[l], w_br_mla[l])

    (qcat, kcat, ckv, kr, hq, hk, hv, lf, sg, sm, sh) = _in_projection(
        x_prompt, x_head, x_smp, w_in_p, q_norm[l].reshape(1, -1), kv_norm[l].reshape(1, -1), wqlat, wqr, perm,
        cos8, sin8, lb, geo)

    o_lat_p = _prompt_attention(qcat, kcat, geo)
    q_s = qcat[:, Rp:].reshape(MLA_HEADS, DB, steps, QCAT)
    knew_s = kcat[Rp:].reshape(DB, steps, 2 * LANES)
    cache_krt = jnp.swapaxes(cache_mla_krope[l], 1, 2)
    o_lat_s = _sample_attention(page_table, q_s, knew_s, cache_mla_ckv[l], cache_krt)
    o_lat_s = o_lat_s.reshape(n_s, MLA_HEADS * LANES)
    o_hg_p, s_prompt = _hgrn_prompt(hq, hk, hv, lf, geo)
    o_hg_s, s_sample = _hgrn_sample(hq, hk, hv, lf, state_hgrn[l], Rp, steps)

    rw_f = jnp.concatenate([router_w[l], jnp.zeros((D_MODEL, LANES - N_EXPERTS), F32)], axis=1)
    rw_hi = rw_f.astype(BF16)
    rw_p = jnp.stack([rw_hi, (rw_f - rw_hi.astype(F32)).astype(BF16)])
    rb_p = jnp.concatenate([router_bias[l].astype(F32), jnp.zeros((LANES - N_EXPERTS,), F32)]).reshape(1, -1)
    x1, x1p, idx, gate, rank, count = _layer_tail(
        x_prompt, x_head, x_smp, o_lat_p, o_lat_s, o_hg_p, o_hg_s, sg, sm, sh, wo, w_br_hg[l].astype(BF16),
        w_out[l].astype(BF16), hg_norm[l].reshape(1, -1), ln1_g[l].reshape(1, -1), ln1_b[l].reshape(1, -1),
        rw_p, rb_p, geo)

    nblk = R * TOP_K // MOE_BLOCK + N_EXPERTS
    dest, block_expert, n_used, n_valid = _routing_tables(
        idx[:, :TOP_K], rank[:, :TOP_K], count[0, :N_EXPERTS], nblk)
    xs = _sc_row_scatter(x1p, dest, nblk * MOE_BLOCK)
    ffs = _shared_expert(x1, w_sh_gate[l].astype(BF16), w_sh_up[l].astype(BF16), w_sh_down[l].astype(BF16))
    ys = _moe_experts(block_expert, n_used, n_valid, xs, w_exp_gate[l], w_exp_up[l], w_exp_down[l])
    yp, y_smp = _moe_combine(dest, x1, gate, ys, ffs, ln2_g[l].reshape(1, -1), ln2_b[l].reshape(1, -1), geo, seq)

    ys_out = y_smp.reshape(DB, steps, D_MODEL)
    ckv_p = ckv[:Rp].reshape(B, tp, MLA_KV_RANK)[:, pad_front:][None]
    kr_p = kr[:Rp].reshape(B, tp, LANES)[:, pad_front:, :MLA_ROPE][None]
    ckv_s = ckv[Rp:].reshape(DB, steps, MLA_KV_RANK)[None]
    kr_s = kr[Rp:, :MLA_ROPE].reshape(DB, steps, MLA_ROPE)[None]
    return (yp, ys_out, ckv_p, kr_p, s_prompt[None], ckv_s, kr_s, s_sample[None])
```

```python
import functools

import jax
import jax.numpy as jnp
from jax import lax
from jax.experimental import pallas as pl
from jax.experimental.pallas import tpu as pltpu
from jax.experimental.pallas import tpu_sc as plsc

F32 = jnp.float32
BF16 = jnp.bfloat16
U32 = jnp.uint32
I32 = jnp.int32

D_MODEL = 1024
MLA_HEADS = 8
MLA_Q_RANK = 256
MLA_KV_RANK = 128
MLA_NOPE = 64
MLA_ROPE = 32
MLA_V = 64
MLA_SCALE = (MLA_NOPE + MLA_ROPE) ** -0.5
LOG2E = 1.4426950408889634
Q_SCALE = MLA_SCALE * LOG2E
ROPE_THETA = 10000.0
HG_HEADS = 4
HG_DK = 128
HG_DV = 128
HG_CHUNK = 64
N_EXPERTS = 64
TOP_K = 8
D_EXPERT = 256
ROUTED_SCALE = 2.5
NORM_EPS = 1e-6
DEPTH = 1
ALPHA = (2.0 * DEPTH) ** 0.25

LANES = 128
ROW_TILE = 256
QCAT = 256
ATT_TK = 256
MOE_BLOCK = 1024
PAGES_PER_CHUNK = 64
ONES_COL = LANES + MLA_ROPE
HG_SAMPLE_BATCH = 8
SATT_BATCH = 1
SC_CORES = 2
SC_SUBCORES = 16
SC_GATHER_ROWS = 64
SC_SCATTER_TOKENS = 8
NEG_BIG = -1e30
VMEM_LIMIT = 52 * 1024 * 1024

C_CQ = (0, 256)
C_CKV = (256, 384)
C_KR = (384, 512)
C_HQ = (512, 1024)
C_HF = (1024, 1536)
C_HI = (1536, 2048)
C_HG = (2048, 2560)
C_GM = (2560, 3584)
C_GH = (3584, 4608)
D_IN_PACKED = 4608


def _cparams(*sem):
    return pltpu.CompilerParams(dimension_semantics=sem, vmem_limit_bytes=VMEM_LIMIT)


def _dot(a, b):
    return jnp.dot(a, b, preferred_element_type=F32)


def _dot_nt(a, b):
    return lax.dot_general(a, b, (((1,), (1,)), ((), ())), preferred_element_type=F32)


def _dot_tn(a, b):
    return lax.dot_general(a, b, (((0,), (0,)), ((), ())), preferred_element_type=F32)


def _pack_bf16_pairs(x):
    w = x.shape[1] // 2
    bits = pltpu.bitcast(x.astype(BF16).astype(F32), U32)
    return bits[:, w:] | (bits[:, :w] >> 16)


def _unpack_bf16_pairs(p):
    lo = pltpu.bitcast(p << 16, F32)
    hi = pltpu.bitcast(p & jnp.uint32(0xFFFF0000), F32)
    return jnp.concatenate([lo, hi], axis=1)


def _x_specs(geo, seq_tiles):
    B, tpb, npt = geo["B"], geo["tpb"], geo["npt"]

    def xp_map(i):
        return (jnp.minimum(i // tpb, B - 1), jnp.where(i < npt, jnp.maximum(i % tpb - 1, 0), seq_tiles - 1), 0)

    return [pl.BlockSpec((1, ROW_TILE, D_MODEL), xp_map),
            pl.BlockSpec((ROW_TILE, D_MODEL), lambda i: (0, 0)),
            pl.BlockSpec((ROW_TILE, D_MODEL), lambda i: (jnp.maximum(i - npt, 0), 0))]


def _select_x(i, xp_ref, xh_ref, xs_ref, tiles_per_batch, n_prompt_tiles):
    is_head = (i < n_prompt_tiles) & (i % tiles_per_batch == 0)
    return jnp.where(i >= n_prompt_tiles, xs_ref[...], jnp.where(is_head, xh_ref[...], xp_ref[0]))


def _split_specs(width, npt):
    return [pl.BlockSpec((ROW_TILE, width), lambda i: (jnp.minimum(i, npt - 1), 0)),
            pl.BlockSpec((ROW_TILE, width), lambda i: (jnp.maximum(i - npt, 0), 0))]


def _wprep_kernel(wq_nope_ref, wuk_ref, wuv_ref, wbr_ref, wqlat_ref, wo_ref):
    hp = lax.Precision.HIGHEST
    for h in range(MLA_HEADS):
        a = wq_nope_ref[h]
        b = wuk_ref[h]
        ql = lax.dot_general(a, b, (((1,), (1,)), ((), ())), precision=hp, preferred_element_type=F32)
        wqlat_ref[:, h * MLA_KV_RANK:(h + 1) * MLA_KV_RANK] = (ql * Q_SCALE).astype(BF16)
        c = wuv_ref[h]
        d = wbr_ref[h]
        wo_ref[h * MLA_KV_RANK:(h + 1) * MLA_KV_RANK, :] = jnp.dot(
            c, d, precision=hp, preferred_element_type=F32).astype(BF16)


def _weight_prep(w_uq, w_uk, w_uv, w_br_mla):
    wq3 = w_uq.reshape(MLA_Q_RANK, MLA_HEADS, MLA_NOPE + MLA_ROPE)
    wq_nope = jnp.transpose(wq3[:, :, :MLA_NOPE], (1, 0, 2))
    wuk = jnp.transpose(w_uk, (1, 0, 2))
    wuv = jnp.transpose(w_uv, (1, 0, 2))
    wbr = w_br_mla.reshape(MLA_HEADS, MLA_V, D_MODEL)
    return pl.pallas_call(
        _wprep_kernel,
        out_shape=(jax.ShapeDtypeStruct((MLA_Q_RANK, MLA_HEADS * MLA_KV_RANK), BF16),
                   jax.ShapeDtypeStruct((MLA_HEADS * MLA_KV_RANK, D_MODEL), BF16)),
        compiler_params=pltpu.CompilerParams(vmem_limit_bytes=VMEM_LIMIT),
        name="weight_prep",
    )(wq_nope, wuk, wuv, wbr)


def _inproj_kernel(xp_ref, xh_ref, xs_ref, w_ref, qn_ref, kvn_ref, wqlat_ref, wqr_ref, perm_ref, cos_ref, sin_ref,
                   lb_ref, qcat_ref, kcat_ref, ckv_ref, kr_ref, hq_ref, hk_ref, hv_ref, lf_ref, sg_ref, sm_ref,
                   sh_ref, *, pad_front, tiles_per_batch, n_prompt_tiles):
    i = pl.program_id(0)
    tm = xh_ref.shape[0]
    xb = _select_x(i, xp_ref, xh_ref, xs_ref, tiles_per_batch, n_prompt_tiles).astype(BF16)

    def proj(c):
        return _dot(xb, w_ref[:, c[0]:c[1]])

    cos8 = cos_ref[...]
    sin8 = sin_ref[...]

    cq = proj(C_CQ)
    cqn = cq * lax.rsqrt(jnp.mean(cq * cq, axis=-1, keepdims=True) + NORM_EPS) * qn_ref[...]
    cqb = cqn.astype(BF16)
    qlat = _dot(cqb, wqlat_ref[...])
    qr = _dot(cqb, wqr_ref[...])
    x1, x2 = qr[:, :LANES], qr[:, LANES:]
    qrot = jnp.concatenate([x1 * cos8 - x2 * sin8, x2 * cos8 + x1 * sin8], axis=1).astype(BF16)
    qrh = _dot(qrot, perm_ref[...])
    for h in range(MLA_HEADS):
        qcat_ref[h, :, :LANES] = qlat[:, h * LANES:(h + 1) * LANES].astype(BF16)
        qcat_ref[h, :, LANES:] = qrh[:, h * LANES:(h + 1) * LANES].astype(BF16)

    kv = proj(C_CKV)
    ckv = kv * lax.rsqrt(jnp.mean(kv * kv, axis=-1, keepdims=True) + NORM_EPS) * kvn_ref[...]
    ckv_ref[...] = ckv
    kcat_ref[:, :LANES] = ckv.astype(BF16)
    krr = proj(C_KR)
    lane = lax.broadcasted_iota(I32, (tm, LANES), 1)
    half = MLA_ROPE // 2
    rot = jnp.where(lane < half, -pltpu.roll(krr, LANES - half, 1), pltpu.roll(krr, half, 1))
    kr = jnp.where(lane < MLA_ROPE, krr * cos8 + rot * sin8, 0.0)
    kr_ref[...] = kr
    kcat_ref[:, LANES:] = jnp.where(lane == ONES_COL - LANES, 1.0, kr).astype(BF16)

    row = lax.broadcasted_iota(I32, (tm, 1), 0)
    is_pad = (i < n_prompt_tiles) & (i % tiles_per_batch == 0) & (row < pad_front)
    keep = jnp.where(is_pad, 0.0, 1.0)
    hq_ref[...] = proj(C_HQ).astype(BF16)
    lb = lb_ref[...]
    f = lb + (1.0 - lb) * jax.nn.sigmoid(proj(C_HF))
    lf_ref[...] = jnp.log(f) * keep
    hk_ref[...] = ((1.0 - f) * keep).astype(BF16)
    hv_ref[...] = proj(C_HI).astype(BF16)

    g = proj(C_HG)
    sg_ref[...] = (g * jax.nn.sigmoid(g)).astype(BF16)
    sm_ref[...] = jax.nn.sigmoid(proj(C_GM)).astype(BF16)
    sh_ref[...] = jax.nn.sigmoid(proj(C_GH)).astype(BF16)


def _in_projection(x_prompt, x_head, x_smp, w_in_p, q_norm, kv_norm, wqlat, wqr, perm, cos8, sin8, lb, geo):
    R = cos8.shape[0]
    nt = R // ROW_TILE
    hgw = HG_HEADS * HG_DK
    x_specs = _x_specs(geo, x_prompt.shape[1] // ROW_TILE)

    def rows(w):
        return pl.BlockSpec((ROW_TILE, w), lambda i: (i, 0))

    def full(a):
        return pl.BlockSpec(a.shape, lambda i: (0,) * a.ndim)

    out_widths = [(2 * LANES, BF16), (LANES, F32), (LANES, F32),
                  (hgw, BF16), (hgw, BF16), (hgw, BF16), (hgw, F32), (hgw, BF16),
                  (D_MODEL, BF16), (D_MODEL, BF16)]
    qcat_spec = pl.BlockSpec((MLA_HEADS, ROW_TILE, QCAT), lambda i: (0, i, 0))
    return pl.pallas_call(
        functools.partial(_inproj_kernel, pad_front=geo["pad_front"], tiles_per_batch=geo["tpb"],
                          n_prompt_tiles=geo["npt"]),
        grid=(nt,),
        in_specs=x_specs + [full(w_in_p), full(q_norm), full(kv_norm), full(wqlat), full(wqr), full(perm),
                            rows(LANES), rows(LANES), full(lb)],
        out_specs=[qcat_spec] + [rows(w) for w, _ in out_widths],
        out_shape=[jax.ShapeDtypeStruct((MLA_HEADS, R, QCAT), BF16)]
        + [jax.ShapeDtypeStruct((R, w), dt) for w, dt in out_widths],
        compiler_params=_cparams("arbitrary"),
        name="in_projection",
    )(x_prompt, x_head, x_smp, w_in_p, q_norm, kv_norm, wqlat, wqr, perm, cos8, sin8, lb)


def _softmax_step(s, v_b, m_ref, l_ref, acc_ref):
    n = s.shape[1] // LANES
    m_prev = m_ref[...]
    m_next = jnp.maximum(m_prev, jnp.max(s, axis=1, keepdims=True))
    p = jnp.concatenate([jnp.exp2(s[:, j * LANES:(j + 1) * LANES] - m_next) for j in range(n)], axis=1)
    alpha = jnp.exp2(m_prev - m_next)
    pv = _dot(p.astype(BF16), v_b)
    if v_b.shape[1] == LANES:
        l_ref[...] = alpha * l_ref[...] + jnp.sum(p, axis=1, keepdims=True)
    else:
        l_ref[...] = alpha * l_ref[...] + pv[:, LANES:]
    acc_ref[...] = alpha * acc_ref[...] + pv[:, :LANES]
    m_ref[...] = m_next


def _pattn_kernel(q_ref, k_ref, o_ref, m_ref, l_ref, acc_ref, *, pad_front):
    qi = pl.program_id(1)
    nh, tq, _ = q_ref.shape
    tk = ATT_TK
    rows = nh * tq

    q = q_ref[...].reshape(rows, QCAT)
    m_ref[...] = jnp.full(m_ref.shape, -jnp.inf, F32)
    l_ref[...] = jnp.zeros(l_ref.shape, F32)
    acc_ref[...] = jnp.zeros(acc_ref.shape, F32)

    def step(kb, masked):
        kblk = k_ref[pl.ds(pl.multiple_of(kb * tk, tk), tk), :]
        s = _dot_nt(q, kblk)
        if masked:
            qrow = qi * tq + lax.broadcasted_iota(I32, (rows, tk), 0) % tq
            krow = kb * tk + lax.broadcasted_iota(I32, (rows, tk), 1)
            s = jnp.where((krow <= qrow) & (krow >= pad_front), s, NEG_BIG)
        _softmax_step(s, kblk, m_ref, l_ref, acc_ref)

    step(0, True)

    def body(kb, c):
        step(kb, False)
        return c

    lax.fori_loop(1, qi, body, 0)

    @pl.when(qi > 0)
    def _():
        step(qi, True)

    row_sum = l_ref[:, ONES_COL - LANES:ONES_COL - LANES + 1]
    o = (acc_ref[...] / row_sum).astype(BF16)
    for h in range(nh):
        o_ref[:, h * LANES:(h + 1) * LANES] = o[h * tq:(h + 1) * tq]


def _prompt_attention(qcat, kcat, geo):
    B, tpb, tp = geo["B"], geo["tpb"], geo["tp"]
    rows = MLA_HEADS * ROW_TILE
    return pl.pallas_call(
        functools.partial(_pattn_kernel, pad_front=geo["pad_front"]),
        grid=(B, tpb),
        in_specs=[pl.BlockSpec((MLA_HEADS, ROW_TILE, QCAT), lambda b, i: (0, b * tpb + i, 0)),
                  pl.BlockSpec((tp, 2 * LANES), lambda b, i: (b, 0))],
        out_specs=pl.BlockSpec((ROW_TILE, MLA_HEADS * LANES), lambda b, i: (b * tpb + i, 0)),
        out_shape=jax.ShapeDtypeStruct((B * tp, MLA_HEADS * LANES), BF16),
        scratch_shapes=[pltpu.VMEM((rows, LANES), F32)] * 3,
        compiler_params=_cparams("arbitrary", "arbitrary"),
        name="prompt_attention",
    )(qcat, kcat)


def _sattn_kernel(pt_ref, q_ref, knew_ref, ckv_hbm, kr_hbm, o_ref,
                  ckv_buf, kr_buf, sem, m_ref, l_ref, acc_ref, *, n_chunks, page):
    g = pl.program_id(0)
    ng = pl.num_programs(0)
    ch = PAGES_PER_CHUNK
    nh, nbt, S = q_ref.shape[0], q_ref.shape[1], q_ref.shape[2]
    rows = nh * S

    def chunk_copies(gg, c, slot):
        cps = []
        for bb in range(nbt):
            for j in range(ch):
                pg = pt_ref[gg * nbt + bb, c * ch + j]
                cps.append(pltpu.make_async_copy(ckv_hbm.at[pg], ckv_buf.at[slot, bb, pl.ds(j * page, page), :],
                                                 sem.at[0, slot]))
                cps.append(pltpu.make_async_copy(kr_hbm.at[pg], kr_buf.at[slot, bb, :, pl.ds(j * page, page)],
                                                 sem.at[1, slot]))
        return cps

    @pl.when(g == 0)
    def _():
        for cp in chunk_copies(0, 0, 0):
            cp.start()

    qlat, qrope = [], []
    for bb in range(nbt):
        q = q_ref[:, bb].reshape(rows, QCAT)
        qlat.append(q[:, :LANES])
        qrope.append(q[:, LANES:LANES + MLA_ROPE])
        knew = knew_ref[bb]
        s_new = _dot_nt(q, knew)
        qtok = lax.broadcasted_iota(I32, (rows, S), 0) % S
        ktok = lax.broadcasted_iota(I32, (rows, S), 1)
        s_new = jnp.where(ktok <= qtok, s_new, NEG_BIG)
        m0 = jnp.max(s_new, axis=1, keepdims=True)
        p0 = jnp.exp2(s_new - m0)
        m_ref[bb] = jnp.broadcast_to(m0, (rows, LANES))
        l_ref[bb] = jnp.broadcast_to(jnp.sum(p0, axis=1, keepdims=True), (rows, LANES))
        acc_ref[bb] = _dot(p0.astype(BF16), knew[:, :LANES])

    for c in range(n_chunks):
        slot = c % 2 if n_chunks % 2 == 0 else (g * n_chunks + c) % 2
        if c + 1 < n_chunks:
            for cp in chunk_copies(g, c + 1, 1 - slot):
                cp.start()
        else:
            @pl.when(g + 1 < ng)
            def _():
                for cp in chunk_copies(g + 1, 0, 1 - slot):
                    cp.start()
        for cp in chunk_copies(g, c, slot):
            cp.wait()
        for bb in range(nbt):
            ckv_b = ckv_buf[slot, bb].astype(BF16)
            krt_b = kr_buf[slot, bb].astype(BF16)
            s = _dot_nt(qlat[bb], ckv_b) + _dot(qrope[bb], krt_b)
            _softmax_step(s, ckv_b, m_ref.at[bb], l_ref.at[bb], acc_ref.at[bb])

    for bb in range(nbt):
        o = (acc_ref[bb] / l_ref[bb]).astype(BF16)
        for h in range(nh):
            o_ref[bb, :, h * LANES:(h + 1) * LANES] = o[h * S:(h + 1) * S]


def _sample_attention(page_table, q_s, knew_s, cache_ckv, cache_krt):
    nh, DB, S, _ = q_s.shape
    rows = nh * S
    n_pages = page_table.shape[1]
    page = cache_ckv.shape[1]
    assert n_pages % PAGES_PER_CHUNK == 0
    n_chunks = n_pages // PAGES_PER_CHUNK
    ck = PAGES_PER_CHUNK * page
    nbt = SATT_BATCH
    assert DB % nbt == 0
    grid_spec = pltpu.PrefetchScalarGridSpec(
        num_scalar_prefetch=1,
        grid=(DB // nbt,),
        in_specs=[pl.BlockSpec((nh, nbt, S, QCAT), lambda g, pt: (0, g, 0, 0)),
                  pl.BlockSpec((nbt, S, 2 * LANES), lambda g, pt: (g, 0, 0)),
                  pl.BlockSpec(memory_space=pl.ANY),
                  pl.BlockSpec(memory_space=pl.ANY)],
        out_specs=pl.BlockSpec((nbt, S, nh * LANES), lambda g, pt: (g, 0, 0)),
        scratch_shapes=[pltpu.VMEM((2, nbt, ck, MLA_KV_RANK), F32),
                        pltpu.VMEM((2, nbt, MLA_ROPE, ck), F32),
                        pltpu.SemaphoreType.DMA((2, 2)),
                        pltpu.VMEM((nbt, rows, LANES), F32),
                        pltpu.VMEM((nbt, rows, LANES), F32),
                        pltpu.VMEM((nbt, rows, LANES), F32)])
    return pl.pallas_call(
        functools.partial(_sattn_kernel, n_chunks=n_chunks, page=page),
        grid_spec=grid_spec,
        out_shape=jax.ShapeDtypeStruct((DB, S, nh * LANES), BF16),
        compiler_params=_cparams("arbitrary"),
        name="sample_attention",
    )(page_table, q_s, knew_s, cache_ckv, cache_krt)


def _split3(x):
    hi = x.astype(BF16)
    r1 = x - hi.astype(F32)
    mid = r1.astype(BF16)
    lo = (r1 - mid.astype(F32)).astype(BF16)
    return hi, mid, lo


def _hgrn_chunk(q, k, v, lf, S0, tri, group):
    C = q.shape[0]
    hi, mid, lo = _split3(lf)
    cum = _dot(tri, hi) + _dot(tri, mid) + _dot(tri, lo)
    vb = v.astype(BF16)
    rowi = lax.broadcasted_iota(I32, (C, C), 0)
    coli = lax.broadcasted_iota(I32, (C, C), 1)
    attn = jnp.zeros((C, C), F32)

    bd = min(16, group)
    hs = group // 2
    rid = lax.broadcasted_iota(I32, (C, 1), 0)
    while hs >= bd:
        npair = C // (2 * hs)
        ref = jnp.concatenate(
            [jnp.broadcast_to(cum[(2 * j + 1) * hs - 1:(2 * j + 1) * hs, :], (2 * hs, cum.shape[1]))
             for j in range(npair)], axis=0)
        odd = ((rid // hs) % 2) == 1
        e = jnp.exp(jnp.where(odd, cum - ref, ref - cum))
        qs = jnp.where(odd, q * e, 0.0).astype(BF16)
        ks = jnp.where(odd, 0.0, k * e).astype(BF16)
        a = _dot_nt(qs, ks)
        attn = attn + jnp.where((rowi // (2 * hs)) == (coli // (2 * hs)), a, 0.0)
        hs //= 2

    nblk = C // bd
    k3 = k.reshape(nblk, bd, k.shape[1])
    c3 = cum.reshape(nblk, bd, cum.shape[1])
    tl = rid % bd
    blk_base = (rowi // bd) * bd
    for sl in range(bd):
        ks_b = jnp.broadcast_to(k3[:, sl:sl + 1, :], k3.shape).reshape(C, k.shape[1])
        cs_b = jnp.broadcast_to(c3[:, sl:sl + 1, :], c3.shape).reshape(C, k.shape[1])
        e = jnp.exp(jnp.where(tl >= sl, cum - cs_b, NEG_BIG))
        col = jnp.sum(q * ks_b * e, axis=1, keepdims=True)
        attn = jnp.where(coli == blk_base + sl, col, attn)
    o_intra = _dot(attn.astype(BF16), vb)
    return o_intra, cum


def _tri(C, group):
    r = lax.broadcasted_iota(I32, (C, C), 0)
    c = lax.broadcasted_iota(I32, (C, C), 1)
    return jnp.where((c <= r) & (r // group == c // group), 1.0, 0.0).astype(BF16)


def _state_update(q, k, v, cum, S, o_intra):
    C = q.shape[0]
    last = cum[C - 1:C, :]
    o = o_intra + _dot((q * jnp.exp(cum)).astype(BF16), S.astype(BF16))
    kst = (k * jnp.exp(last - cum)).astype(BF16)
    dfull = jnp.transpose(jnp.broadcast_to(jnp.exp(last), (S.shape[1], S.shape[0])))
    S_new = dfull * S + _dot_tn(kst, v.astype(BF16))
    return o, S_new


def _hgrn_prompt_kernel(q_ref, k_ref, v_ref, lf_ref, o_ref, s_out_ref, s_ref):
    t = pl.program_id(1)

    @pl.when(t == 0)
    def _():
        s_ref[...] = jnp.zeros(s_ref.shape, F32)

    C = HG_CHUNK
    tri = _tri(C, C)
    states = [s_ref[h] for h in range(HG_HEADS)]
    for c in range(q_ref.shape[0] // C):
        sl = slice(c * C, (c + 1) * C)
        for h in range(HG_HEADS):
            hl = slice(h * HG_DK, (h + 1) * HG_DK)
            q = q_ref[sl, hl].astype(F32)
            k = k_ref[sl, hl].astype(F32)
            v = v_ref[sl, hl].astype(F32)
            lf = lf_ref[sl, hl]
            o_intra, cum = _hgrn_chunk(q, k, v, lf, None, tri, C)
            o, states[h] = _state_update(q, k, v, cum, states[h], o_intra)
            o_ref[sl, hl] = o.astype(BF16)
    for h in range(HG_HEADS):
        s_ref[h] = states[h]

    @pl.when(t == pl.num_programs(1) - 1)
    def _():
        s_out_ref[0] = s_ref[...]


def _hgrn_prompt(hq, hk, hv, lf, geo):
    B, tpb, tp = geo["B"], geo["tpb"], geo["tp"]
    hgw = HG_HEADS * HG_DK

    def blk():
        return pl.BlockSpec((ROW_TILE, hgw), lambda b, t: (b * tpb + t, 0))

    return pl.pallas_call(
        _hgrn_prompt_kernel,
        grid=(B, tpb),
        in_specs=[blk(), blk(), blk(), blk()],
        out_specs=[blk(), pl.BlockSpec((1, HG_HEADS, HG_DK, HG_DV), lambda b, t: (b, 0, 0, 0))],
        out_shape=[jax.ShapeDtypeStruct((B * tp, hgw), BF16),
                   jax.ShapeDtypeStruct((B, HG_HEADS, HG_DK, HG_DV), F32)],
        scratch_shapes=[pltpu.VMEM((HG_HEADS, HG_DK, HG_DV), F32)],
        compiler_params=_cparams("arbitrary", "arbitrary"),
        name="hgrn_prompt",
    )(hq, hk, hv, lf)


def _hgrn_sample_kernel(q_ref, k_ref, v_ref, lf_ref, s_in_ref, o_ref, s_out_ref, *, steps):
    nb = s_in_ref.shape[0]
    C = nb * steps
    tri = _tri(C, steps)
    for h in range(HG_HEADS):
        sl = slice(h * HG_DK, (h + 1) * HG_DK)
        q = q_ref[:, sl].astype(F32)
        k = k_ref[:, sl].astype(F32)
        v = v_ref[:, sl].astype(F32)
        lf = lf_ref[:, sl]
        o_intra, cum = _hgrn_chunk(q, k, v, lf, None, tri, steps)
        for b in range(nb):
            r = slice(b * steps, (b + 1) * steps)
            o, S_new = _state_update(q[r], k[r], v[r], cum[r], s_in_ref[b, h], o_intra[r])
            o_ref[r, sl] = o.astype(BF16)
            s_out_ref[b, h] = S_new


def _hgrn_sample(hq, hk, hv, lf, state, row0, steps):
    DB = state.shape[0]
    nb = HG_SAMPLE_BATCH
    rows = nb * steps
    hgw = HG_HEADS * HG_DK
    blk0 = row0 // rows

    def tok():
        return pl.BlockSpec((rows, hgw), lambda i: (blk0 + i, 0))

    st = pl.BlockSpec((nb, HG_HEADS, HG_DK, HG_DV), lambda i: (i, 0, 0, 0))
    return pl.pallas_call(
        functools.partial(_hgrn_sample_kernel, steps=steps),
        grid=(DB // nb,),
        in_specs=[tok(), tok(), tok(), tok(), st],
        out_specs=[pl.BlockSpec((rows, hgw), lambda i: (i, 0)), st],
        out_shape=[jax.ShapeDtypeStruct((DB * steps, hgw), BF16),
                   jax.ShapeDtypeStruct(state.shape, F32)],
        compiler_params=_cparams("arbitrary"),
        name="hgrn_sample",
    )(hq, hk, hv, lf, state)


def _layer_norm(x, g, b):
    mu = jnp.mean(x, axis=-1, keepdims=True)
    xc = x - mu
    var = jnp.mean(xc * xc, axis=-1, keepdims=True)
    return xc * lax.rsqrt(var + NORM_EPS) * g + b


def _tail_kernel(xp_ref, xh_ref, xs_ref, olp_ref, ols_ref, ohp_ref, ohs_ref, sg_ref, sm_ref, sh_ref, wo_ref, wbh_ref,
                 wout_ref, hgn_ref, g1_ref, b1_ref, rw_ref, rb_ref, x1_ref, x1p_ref, idx_ref, gate_ref, rank_ref,
                 count_ref, cnt_ref, *, tiles_per_batch, n_prompt_tiles):
    i = pl.program_id(0)
    tm = xh_ref.shape[0]
    x_in = _select_x(i, xp_ref, xh_ref, xs_ref, tiles_per_batch, n_prompt_tiles)
    is_sample = i >= n_prompt_tiles
    mla = _dot(jnp.where(is_sample, ols_ref[...], olp_ref[...]), wo_ref[...])
    oh = jnp.where(is_sample, ohs_ref[...], ohp_ref[...]).astype(F32)
    parts = []
    for h in range(HG_HEADS):
        y = oh[:, h * HG_DV:(h + 1) * HG_DV]
        parts.append(y * lax.rsqrt(jnp.mean(y * y, axis=-1, keepdims=True) + NORM_EPS) * hgn_ref[...])
    hg = (jnp.concatenate(parts, axis=1) * sg_ref[...].astype(F32)).astype(BF16)
    merged = sm_ref[...].astype(F32) * mla + sh_ref[...].astype(F32) * _dot(hg, wbh_ref[...])
    x1 = _layer_norm(ALPHA * x_in + _dot(merged.astype(BF16), wout_ref[...]), g1_ref[...], b1_ref[...])
    x1_ref[...] = x1
    x1p_ref[...] = pltpu.bitcast(_pack_bf16_pairs(x1), I32)

    x1_hi = x1.astype(BF16)
    x1_lo = (x1 - x1_hi.astype(F32)).astype(BF16)
    logits = _dot(x1_hi, rw_ref[0]) + _dot(x1_lo, rw_ref[0]) + _dot(x1_hi, rw_ref[1])
    scores = jax.nn.sigmoid(logits)
    lane = lax.broadcasted_iota(I32, (tm, LANES), 1).astype(F32)
    remaining = jnp.where(lane < N_EXPERTS, scores + rb_ref[...], -jnp.inf)
    idx_out = jnp.zeros((tm, LANES), F32)
    gate_out = jnp.zeros((tm, LANES), F32)
    hits = []
    for kk in range(TOP_K):
        mx = jnp.max(remaining, axis=1, keepdims=True)
        pick = jnp.min(jnp.where(remaining == mx, lane, float(LANES)), axis=1, keepdims=True)
        hit = lane == pick
        hits.append(hit)
        gval = jnp.sum(jnp.where(hit, scores, 0.0), axis=1, keepdims=True)
        idx_out = jnp.where(lane == kk, pick, idx_out)
        gate_out = jnp.where(lane == kk, gval, gate_out)
        remaining = jnp.where(hit, -jnp.inf, remaining)
    gate_out = gate_out / jnp.sum(gate_out, axis=1, keepdims=True) * ROUTED_SCALE
    idx_ref[...] = idx_out.astype(I32)
    gate_ref[...] = gate_out

    @pl.when(pl.program_id(0) == 0)
    def _():
        cnt_ref[...] = jnp.zeros(cnt_ref.shape, F32)

    sel = jnp.where(remaining == -jnp.inf, 1.0, 0.0) * jnp.where(lane < N_EXPERTS, 1.0, 0.0)
    r_i = lax.broadcasted_iota(I32, (tm, tm), 0)
    c_i = lax.broadcasted_iota(I32, (tm, tm), 1)
    before = _dot(jnp.where(c_i < r_i, 1.0, 0.0).astype(BF16), sel.astype(BF16)) + cnt_ref[0:1, :]
    rank_out = jnp.zeros((tm, LANES), F32)
    for kk in range(TOP_K):
        rk = jnp.sum(jnp.where(hits[kk], before, 0.0), axis=1, keepdims=True)
        rank_out = jnp.where(lane == kk, rk, rank_out)
    rank_ref[...] = rank_out.astype(I32)
    total = cnt_ref[0:1, :] + jnp.sum(sel, axis=0, keepdims=True)
    cnt_ref[...] = jnp.broadcast_to(total, cnt_ref.shape)
    count_ref[...] = jnp.broadcast_to(total, count_ref.shape).astype(I32)


def _layer_tail(x_prompt, x_head, x_smp, o_lat_p, o_lat_s, o_hg_p, o_hg_s, sg, sm, sh, wo, w_br_hg, w_out, hg_norm,
                ln1_g, ln1_b, rw_p, rb_p, geo):
    R = sg.shape[0]
    npt = geo["npt"]

    def rows(w):
        return pl.BlockSpec((ROW_TILE, w), lambda i: (i, 0))

    def full(a):
        return pl.BlockSpec(a.shape, lambda i: (0,) * a.ndim)

    hgw = HG_HEADS * HG_DV
    return pl.pallas_call(
        functools.partial(_tail_kernel, tiles_per_batch=geo["tpb"], n_prompt_tiles=npt),
        grid=(R // ROW_TILE,),
        in_specs=_x_specs(geo, x_prompt.shape[1] // ROW_TILE)
        + _split_specs(MLA_HEADS * LANES, npt) + _split_specs(hgw, npt)
        + [rows(hgw), rows(D_MODEL), rows(D_MODEL),
           full(wo), full(w_br_hg), full(w_out), full(hg_norm), full(ln1_g), full(ln1_b),
           full(rw_p), full(rb_p)],
        out_specs=[rows(D_MODEL), rows(D_MODEL // 2), rows(LANES), rows(LANES), rows(LANES),
                   pl.BlockSpec((8, LANES), lambda i: (0, 0))],
        out_shape=[jax.ShapeDtypeStruct((R, D_MODEL), F32), jax.ShapeDtypeStruct((R, D_MODEL // 2), I32),
                   jax.ShapeDtypeStruct((R, LANES), I32), jax.ShapeDtypeStruct((R, LANES), F32),
                   jax.ShapeDtypeStruct((R, LANES), I32), jax.ShapeDtypeStruct((8, LANES), I32)],
        scratch_shapes=[pltpu.VMEM((8, LANES), F32)],
        compiler_params=_cparams("arbitrary"),
        name="layer_tail",
    )(x_prompt, x_head, x_smp, o_lat_p, o_lat_s, o_hg_p, o_hg_s, sg, sm, sh, wo, w_br_hg, w_out, hg_norm,
      ln1_g, ln1_b, rw_p, rb_p)


def _sc_row_scatter(x, dest, n_slots):
    R, d = x.shape
    nw = SC_CORES * SC_SUBCORES
    T = SC_SCATTER_TOKENS
    rows = T * TOP_K
    per_w = R // nw
    assert R % (nw * 2 * T) == 0
    n_pairs = per_w // (2 * T)
    order = jnp.transpose(dest.reshape(R // T, T, TOP_K), (0, 2, 1)).reshape(R * TOP_K)
    mesh = plsc.VectorSubcoreMesh(core_axis_name="c", subcore_axis_name="s",
                                  num_cores=SC_CORES, num_subcores=SC_SUBCORES)

    @functools.partial(
        pl.kernel, mesh=mesh, out_type=jax.ShapeDtypeStruct((n_slots, d), x.dtype),
        scratch_types=[pltpu.VMEM((rows,), I32), pltpu.VMEM((rows,), I32),
                       pltpu.VMEM((rows, d), x.dtype), pltpu.VMEM((rows, d), x.dtype),
                       pltpu.SemaphoreType.DMA, pltpu.SemaphoreType.DMA, pltpu.SemaphoreType.DMA],
        name="moe_dispatch_sc")
    def scatter(x_hbm, idx_hbm, out_hbm, idx_a, idx_b, rows_a, rows_b, sem_a, sem_b, sem_out):
        wid = lax.axis_index("s") * SC_CORES + lax.axis_index("c")
        tbase = wid * per_w
        bufs = ((idx_a, rows_a, sem_a), (idx_b, rows_b, sem_b))

        def loads(c, buf):
            idx_v, rows_v, sem = buf
            t0 = pl.multiple_of(tbase + c * T, T)
            off = pl.multiple_of(t0 * TOP_K, rows)
            cps = [pltpu.make_async_copy(idx_hbm.at[pl.ds(off, rows)], idx_v, sem)]
            for k in range(TOP_K):
                cps.append(pltpu.make_async_copy(x_hbm.at[pl.ds(t0, T)], rows_v.at[pl.ds(k * T, T)], sem))
            return cps

        def load(c, buf):
            for cp in loads(c, buf):
                cp.start()

        def send(c, buf):
            idx_v, rows_v, _ = buf
            for cp in loads(c, buf):
                cp.wait()
            pltpu.async_copy(rows_v, out_hbm.at[idx_v], sem_out).wait()

        load(0, bufs[0])

        def body(p, carry):
            c = 2 * p
            load(c + 1, bufs[1])
            send(c, bufs[0])

            @pl.when(p + 1 < n_pairs)
            def _():
                load(c + 2, bufs[0])

            send(c + 1, bufs[1])
            return carry

        lax.fori_loop(0, n_pairs, body, 0)

    return scatter(x, order)


def _expert_kernel(be_ref, nused_ref, nvalid_ref, xs_ref, wg_ref, wu_ref, wd_ref, ys_ref, wgb_ref, wub_ref, wdb_ref):
    i = pl.program_id(0)

    @pl.when((i == 0) | (be_ref[i] != be_ref[jnp.maximum(i - 1, 0)]))
    def _():
        wgb_ref[...] = wg_ref[0].astype(BF16)
        wub_ref[...] = wu_ref[0].astype(BF16)
        wdb_ref[...] = wd_ref[0].astype(BF16)

    @pl.when(i < nused_ref[0])
    def _():
        row = lax.broadcasted_iota(I32, xs_ref.shape, 0)
        xw = jnp.where(row < nvalid_ref[i], xs_ref[...], 0)
        xb = _unpack_bf16_pairs(pltpu.bitcast(xw, U32)).astype(BF16)
        g = _dot(xb, wgb_ref[...])
        u = _dot(xb, wub_ref[...])
        hmid = (g * jax.nn.sigmoid(g) * u).astype(BF16)
        ys_ref[...] = pltpu.bitcast(_pack_bf16_pairs(_dot(hmid, wdb_ref[...])), I32)

    @pl.when(i >= nused_ref[0])
    def _():
        ys_ref[...] = jnp.zeros(ys_ref.shape, I32)


def _moe_experts(block_expert, n_used, n_valid, xs, wg, wu, wd):
    n_slots, W = xs.shape
    nblk = n_slots // MOE_BLOCK

    def xmap(i, be, nu, nv):
        return (jnp.minimum(i, nu[0] - 1), 0)

    def wmap(i, be, nu, nv):
        return (be[i], 0, 0)

    grid_spec = pltpu.PrefetchScalarGridSpec(
        num_scalar_prefetch=3,
        grid=(nblk,),
        in_specs=[pl.BlockSpec((MOE_BLOCK, W), xmap),
                  pl.BlockSpec((1, D_MODEL, D_EXPERT), wmap),
                  pl.BlockSpec((1, D_MODEL, D_EXPERT), wmap),
                  pl.BlockSpec((1, D_EXPERT, D_MODEL), wmap)],
        out_specs=pl.BlockSpec((MOE_BLOCK, W), lambda i, be, nu, nv: (i, 0)),
        scratch_shapes=[pltpu.VMEM((D_MODEL, D_EXPERT), BF16), pltpu.VMEM((D_MODEL, D_EXPERT), BF16),
                        pltpu.VMEM((D_EXPERT, D_MODEL), BF16)])
    return pl.pallas_call(
        _expert_kernel,
        grid_spec=grid_spec,
        out_shape=jax.ShapeDtypeStruct((n_slots, W), I32),
        compiler_params=_cparams("arbitrary"),
        name="moe_experts",
    )(block_expert, n_used, n_valid, xs, wg, wu, wd)


def _sc_row_gather(table, idx):
    n, d = idx.shape[0], table.shape[1]
    nw = SC_CORES * SC_SUBCORES
    per_w = n // nw
    rows = SC_GATHER_ROWS
    assert n % (nw * 2 * rows) == 0
    n_pairs = per_w // (2 * rows)
    mesh = plsc.VectorSubcoreMesh(core_axis_name="c", subcore_axis_name="s",
                                  num_cores=SC_CORES, num_subcores=SC_SUBCORES)

    @functools.partial(
        pl.kernel, mesh=mesh, out_type=jax.ShapeDtypeStruct((n, d), table.dtype),
        scratch_types=[pltpu.VMEM((per_w,), I32),
                       pltpu.VMEM((rows, d), table.dtype), pltpu.VMEM((rows, d), table.dtype),
                       pltpu.SemaphoreType.DMA, pltpu.SemaphoreType.DMA],
        name="moe_gather_sc")
    def gather(table_hbm, idx_hbm, out_hbm, idx_all, rows_a, rows_b, sem_a, sem_b):
        wid = lax.axis_index("s") * SC_CORES + lax.axis_index("c")
        base = wid * per_w
        bufs = ((rows_a, sem_a), (rows_b, sem_b))
        pltpu.sync_copy(idx_hbm.at[pl.ds(pl.multiple_of(base, rows), per_w)], idx_all)

        def stream(c, buf):
            rows_v, sem = buf
            idx_v = idx_all.at[pl.ds(pl.multiple_of(c * rows, rows), rows)]
            return pltpu.make_async_copy(table_hbm.at[idx_v], rows_v, sem)

        def start(c, buf):
            stream(c, buf).start()

        def finish(c, buf):
            off = pl.multiple_of(base + c * rows, rows)
            stream(c, buf).wait()
            pltpu.sync_copy(buf[0], out_hbm.at[pl.ds(off, rows)])

        start(0, bufs[0])

        def body(p, carry):
            c = 2 * p
            start(c + 1, bufs[1])
            finish(c, bufs[0])

            @pl.when(p + 1 < n_pairs)
            def _():
                start(c + 2, bufs[0])

            finish(c + 1, bufs[1])
            return carry

        lax.fori_loop(0, n_pairs, body, 0)

    return gather(table, idx)


def _combine_kernel(x1_ref, gate_ref, yk_ref, wsg_ref, wsu_ref, wsd_ref, g2_ref, b2_ref, yp_ref, ysmp_ref,
                    *, n_prompt_tiles):
    i = pl.program_id(0)
    tm = x1_ref.shape[0]
    x1 = x1_ref[...]
    xb = x1.astype(BF16)
    g = _dot(xb, wsg_ref[...])
    u = _dot(xb, wsu_ref[...])
    ff = _dot((g * jax.nn.sigmoid(g) * u).astype(BF16), wsd_ref[...])
    gate = gate_ref[...]
    for kk in range(TOP_K):
        ff = ff + gate[:, kk:kk + 1] * _unpack_bf16_pairs(pltpu.bitcast(yk_ref[kk * tm:(kk + 1) * tm, :], U32))
    y = _layer_norm(ALPHA * x1 + ff, g2_ref[...], b2_ref[...])

    @pl.when(i < n_prompt_tiles)
    def _():
        yp_ref[0] = y

    @pl.when(i >= n_prompt_tiles)
    def _():
        ysmp_ref[...] = y


def _moe_combine(dest, x1, gate, ys, wsg, wsu, wsd, ln2_g, ln2_b, geo, seq):
    R = x1.shape[0]
    nt = R // ROW_TILE
    W = ys.shape[1]
    B, tpb, npt = geo["B"], geo["tpb"], geo["npt"]
    assert seq == (tpb - 1) * ROW_TILE
    order = jnp.transpose(dest.reshape(nt, ROW_TILE, TOP_K), (0, 2, 1)).reshape(R * TOP_K)
    yk = _sc_row_gather(ys, order)

    def rows(w):
        return pl.BlockSpec((ROW_TILE, w), lambda i: (i, 0))

    def full(a):
        return pl.BlockSpec(a.shape, lambda i: (0,) * a.ndim)

    def rows_k(w):
        return pl.BlockSpec((TOP_K * ROW_TILE, w), lambda i: (i, 0))

    def yp_map(i):
        in_prompt = i < npt
        return (jnp.minimum(i // tpb, B - 1), jnp.where(in_prompt, jnp.maximum(i % tpb - 1, 0), tpb - 2), 0)

    return pl.pallas_call(
        functools.partial(_combine_kernel, n_prompt_tiles=npt),
        grid=(nt,),
        in_specs=[rows(D_MODEL), rows(LANES), rows_k(W),
                  full(wsg), full(wsu), full(wsd), full(ln2_g), full(ln2_b)],
        out_specs=[pl.BlockSpec((1, ROW_TILE, D_MODEL), yp_map),
                   pl.BlockSpec((ROW_TILE, D_MODEL), lambda i: (jnp.maximum(i - npt, 0), 0))],
        out_shape=[jax.ShapeDtypeStruct((B, seq, D_MODEL), F32),
                   jax.ShapeDtypeStruct((R - npt * ROW_TILE, D_MODEL), F32)],
        compiler_params=_cparams("arbitrary"),
        name="moe_combine",
    )(x1, gate, yk, wsg, wsu, wsd, ln2_g, ln2_b)


def _routing_tables(idx8, rank8, counts, n_slots_blocks):
    R = idx8.shape[0]
    padded = (counts + MOE_BLOCK - 1) // MOE_BLOCK * MOE_BLOCK
    pends = jnp.cumsum(padded)
    pstarts = pends - padded
    experts = jnp.arange(N_EXPERTS, dtype=I32)
    start8 = jnp.sum(jnp.where(idx8[:, :, None] == experts[None, None, :], pstarts[None, None, :], 0), axis=-1)
    dest = start8 + rank8
    blk_start = jnp.arange(n_slots_blocks, dtype=I32) * MOE_BLOCK
    block_expert = jnp.minimum(jnp.sum((blk_start[:, None] >= pends[None, :]).astype(I32), axis=1), N_EXPERTS - 1)
    n_used = (pends[-1] // MOE_BLOCK).astype(I32).reshape(1)
    hit = block_expert[:, None] == experts[None, :]
    left = jnp.sum(jnp.where(hit, (counts + pstarts)[None, :], 0), axis=1) - blk_start
    n_valid = jnp.clip(left, 0, MOE_BLOCK).astype(I32)
    return dest.astype(I32), block_expert.astype(I32), n_used, n_valid


def kernel(x_prompt, x_sample, cache_mla_ckv, cache_mla_krope, state_hgrn, page_table, meta_tokens, hg_lb_logits,
           w_in, q_norm, kv_norm, w_uq, w_uk, w_uv, hg_norm, w_br_mla, w_br_hg, w_out, ln1_g, ln1_b, router_w,
           router_bias, w_exp_gate, w_exp_up, w_exp_down, w_sh_gate, w_sh_up, w_sh_down, ln2_g, ln2_b):
    assert w_in.shape[0] == DEPTH
    B, seq, _ = x_prompt.shape
    DB, steps, _ = x_sample.shape
    n_meta = meta_tokens.shape[0]
    n_pages, page = page_table.shape[1], cache_mla_ckv.shape[2]
    past = n_pages * page
    T = n_meta + seq
    tp = -(-T // ROW_TILE) * ROW_TILE
    pad_front = tp - T
    tpb = tp // ROW_TILE
    n_s = DB * steps
    assert n_s % ROW_TILE == 0 and DB % HG_SAMPLE_BATCH == 0
    geo = dict(B=B, tp=tp, tpb=tpb, pad_front=pad_front, npt=B * tpb)
    Rp = B * tp
    R = Rp + n_s

    assert pad_front + n_meta == ROW_TILE and seq % ROW_TILE == 0
    x_head = jnp.concatenate([jnp.zeros((pad_front, D_MODEL), F32), meta_tokens.astype(F32)], axis=0)
    x_smp = x_sample.reshape(n_s, D_MODEL)

    pos_p = jnp.maximum(jnp.arange(tp) - pad_front, 0)
    pos = jnp.concatenate([jnp.tile(pos_p, B), jnp.tile(past + jnp.arange(steps), DB)]).astype(F32)
    half = MLA_ROPE // 2
    inv = ROPE_THETA ** (-jnp.arange(half, dtype=F32) / half)
    ang = pos[:, None] * inv[None, :]
    cos8 = jnp.tile(jnp.cos(ang), (1, LANES // half))
    sin8 = jnp.tile(jnp.sin(ang), (1, LANES // half))

    l = 0
    win = w_in[l]
    kr_end = MLA_Q_RANK + MLA_KV_RANK + MLA_ROPE
    w_in_p = jnp.concatenate([win[:, :kr_end], jnp.zeros((D_MODEL, LANES - MLA_ROPE), F32), win[:, kr_end:]],
                             axis=1).astype(BF16)
    assert w_in_p.shape[1] == D_IN_PACKED
    wq3 = w_uq[l].reshape(MLA_Q_RANK, MLA_HEADS, MLA_NOPE + MLA_ROPE)
    wqr = (jnp.concatenate([wq3[:, :, MLA_NOPE:MLA_NOPE + half].reshape(MLA_Q_RANK, MLA_HEADS * half),
                            wq3[:, :, MLA_NOPE + half:].reshape(MLA_Q_RANK, MLA_HEADS * half)], axis=1)
           * Q_SCALE).astype(BF16)
    src = jnp.arange(2 * LANES)
    hh, ii = (src % LANES) // half, src % half
    dst = hh * LANES + ii + jnp.where(src >= LANES, half, 0)
    perm = (dst[:, None] == jnp.arange(MLA_HEADS * LANES)[None, :]).astype(BF16)
    lb = jnp.cumsum(jax.nn.softmax(hg_lb_logits.astype(F32), axis=0), axis=0)[l].reshape(1, -1)

    wqlat, wo = _weight_prep(w_uq[l], w_uk[l], w_uv[l], w_br_mla[l])

    (qcat, kcat, ckv, kr, hq, hk, hv, lf, sg, sm, sh) = _in_projection(
        x_prompt, x_head, x_smp, w_in_p, q_norm[l].reshape(1, -1), kv_norm[l].reshape(1, -1), wqlat, wqr, perm,
        cos8, sin8, lb, geo)

    o_lat_p = _prompt_attention(qcat, kcat, geo)
    q_s = qcat[:, Rp:].reshape(MLA_HEADS, DB, steps, QCAT)
    knew_s = kcat[Rp:].reshape(DB, steps, 2 * LANES)
    cache_krt = jnp.swapaxes(cache_mla_krope[l], 1, 2)
    o_lat_s = _sample_attention(page_table, q_s, knew_s, cache_mla_ckv[l], cache_krt)
    o_lat_s = o_lat_s.reshape(n_s, MLA_HEADS * LANES)
    o_hg_p, s_prompt = _hgrn_prompt(hq, hk, hv, lf, geo)
    o_hg_s, s_sample = _hgrn_sample(hq, hk, hv, lf, state_hgrn[l], Rp, steps)

    rw_f = jnp.concatenate([router_w[l], jnp.zeros((D_MODEL, LANES - N_EXPERTS), F32)], axis=1)
    rw_hi = rw_f.astype(BF16)
    rw_p = jnp.stack([rw_hi, (rw_f - rw_hi.astype(F32)).astype(BF16)])
    rb_p = jnp.concatenate([router_bias[l].astype(F32), jnp.zeros((LANES - N_EXPERTS,), F32)]).reshape(1, -1)
    x1, x1p, idx, gate, rank, count = _layer_tail(
        x_prompt, x_head, x_smp, o_lat_p, o_lat_s, o_hg_p, o_hg_s, sg, sm, sh, wo, w_br_hg[l].astype(BF16),
        w_out[l].astype(BF16), hg_norm[l].reshape(1, -1), ln1_g[l].reshape(1, -1), ln1_b[l].reshape(1, -1),
        rw_p, rb_p, geo)

    nblk = R * TOP_K // MOE_BLOCK + N_EXPERTS
    dest, block_expert, n_used, n_valid = _routing_tables(
        idx[:, :TOP_K], rank[:, :TOP_K], count[0, :N_EXPERTS], nblk)
    xs = _sc_row_scatter(x1p, dest, nblk * MOE_BLOCK)
    ys = _moe_experts(block_expert, n_used, n_valid, xs, w_exp_gate[l], w_exp_up[l], w_exp_down[l])
    yp, y_smp = _moe_combine(dest, x1, gate, ys, w_sh_gate[l].astype(BF16), w_sh_up[l].astype(BF16),
                             w_sh_down[l].astype(BF16), ln2_g[l].reshape(1, -1), ln2_b[l].reshape(1, -1), geo, seq)

    ys_out = y_smp.reshape(DB, steps, D_MODEL)
    ckv_p = ckv[:Rp].reshape(B, tp, MLA_KV_RANK)[:, pad_front:][None]
    kr_p = kr[:Rp].reshape(B, tp, LANES)[:, pad_front:, :MLA_ROPE][None]
    ckv_s = ckv[Rp:].reshape(DB, steps, MLA_KV_RANK)[None]
    kr_s = kr[Rp:, :MLA_ROPE].reshape(DB, steps, MLA_ROPE)[None]
    return (yp, ys_out, ckv_p, kr_p, s_prompt[None], ckv_s, kr_s, s_sample[None])
```
